```python
import math
import jax, jax.numpy as jnp
from jax import lax
import numpy as np

D_MODEL = 1024
BATCH = 4
SEQ = 4096
DEPTH = 1

CHUNK = 64
MIX_WIDTH = D_MODEL
HG_WIDTH = MIX_WIDTH // 2
HG_HEAD_DIM = 128
HG_HEADS = HG_WIDTH // HG_HEAD_DIM
S5_WIDTH = MIX_WIDTH - HG_WIDTH
S5_GROUP = 16
S5_GROUPS = S5_WIDTH // S5_GROUP
S5_STATE = 64
IN_WIDTH = 4 * HG_WIDTH + S5_WIDTH
D_FF = ((8 * D_MODEL // 3) + 127) // 128 * 128
CONV_WIDTH = 3
EPS = 1e-6
DT_MIN = 1e-3
DT_MAX = 1e-1

kernel_name = "hgrn2_s5_parallel_hybrid_block"


def rmsnorm(x, g):
    xf = x.astype(jnp.float32)
    y = xf * lax.rsqrt(jnp.mean(xf * xf, axis=-1, keepdims=True) + EPS)
    return (y * g.astype(jnp.float32)).astype(x.dtype)


def hgrn2_mix(q, fz, v, gz, lb, norm_g):
    bsz, seq_len, _ = q.shape
    n_chunks = seq_len // CHUNK

    def heads(t):
        return t.astype(jnp.float32).reshape(bsz, n_chunks, CHUNK, HG_HEADS, HG_HEAD_DIM).transpose(0, 3, 1, 2, 4)

    f = lb + (1.0 - lb) * jax.nn.sigmoid(fz.astype(jnp.float32))
    logf = heads(jnp.log(f))
    k = heads(1.0 - f)
    qh = heads(q)
    vh = heads(v)

    b = jnp.cumsum(logf, axis=3)
    b_last = b[:, :, :, -1:, :]
    q_dec = qh * jnp.exp(b)
    k_dec = k * jnp.exp(-b)

    mask = jnp.tril(jnp.ones((CHUNK, CHUNK), dtype=bool))
    att = jnp.einsum('bhnck,bhnsk->bhncs', q_dec, k_dec)
    att = jnp.where(mask, att, 0.0)
    o_intra = jnp.einsum('bhncs,bhnsv->bhncv', att, vh)

    k_tail = k * jnp.exp(b_last - b)
    d_state = jnp.einsum('bhnck,bhncv->nbhkv', k_tail, vh)
    decay = jnp.exp(b_last[:, :, :, 0, :]).transpose(2, 0, 1, 3)

    def step(state, inp):
        dec, ds = inp
        return dec[..., None] * state + ds, state

    s0 = jnp.zeros((bsz, HG_HEADS, HG_HEAD_DIM, HG_HEAD_DIM), jnp.float32)
    _, s_start = lax.scan(step, s0, (decay, d_state))
    o_inter = jnp.einsum('bhnck,nbhkv->bhncv', q_dec, s_start)

    o = (o_intra + o_inter).transpose(0, 2, 3, 1, 4).reshape(bsz, seq_len, HG_HEADS, HG_HEAD_DIM)
    o = o * lax.rsqrt(jnp.mean(o * o, axis=-1, keepdims=True) + EPS)
    o = o.reshape(bsz, seq_len, HG_WIDTH) * norm_g.astype(jnp.float32) * jax.nn.silu(gz.astype(jnp.float32))
    return o.astype(q.dtype)


def s5_mix(u, a_re, a_im, log_dt, b_re, b_im, c_re, c_im, d_skip, w_glu, b_glu):
    bsz, seq_len, _ = u.shape
    uf = u.astype(jnp.float32).reshape(bsz, seq_len, S5_GROUPS, S5_GROUP)
    dt = jnp.exp(log_dt.astype(jnp.float32))[:, None]
    ar = a_re.astype(jnp.float32)
    ai = a_im.astype(jnp.float32)
    mag = jnp.exp(dt * ar)
    abar_re = mag * jnp.cos(dt * ai)
    abar_im = mag * jnp.sin(dt * ai)
    num_re = abar_re - 1.0
    num_im = abar_im
    den = ar * ar + ai * ai
    z_re = (num_re * ar + num_im * ai) / den
    z_im = (num_im * ar - num_re * ai) / den
    br = b_re.astype(jnp.float32)
    bi = b_im.astype(jnp.float32)
    bbar_re = z_re[..., None] * br - z_im[..., None] * bi
    bbar_im = z_re[..., None] * bi + z_im[..., None] * br

    bu_re = jnp.einsum('blgp,gnp->blgn', uf, bbar_re)
    bu_im = jnp.einsum('blgp,gnp->blgn', uf, bbar_im)
    a_re_t = jnp.broadcast_to(abar_re, bu_re.shape)
    a_im_t = jnp.broadcast_to(abar_im, bu_im.shape)

    def combine(e1, e2):
        a1r, a1i, x1r, x1i = e1
        a2r, a2i, x2r, x2i = e2
        return (a2r * a1r - a2i * a1i,
                a2r * a1i + a2i * a1r,
                a2r * x1r - a2i * x1i + x2r,
                a2r * x1i + a2i * x1r + x2i)

    _, _, x_re, x_im = lax.associative_scan(combine, (a_re_t, a_im_t, bu_re, bu_im), axis=1)
    y = (jnp.einsum('gpn,blgn->blgp', c_re.astype(jnp.float32), x_re)
         - jnp.einsum('gpn,blgn->blgp', c_im.astype(jnp.float32), x_im))
    y = y + d_skip.astype(jnp.float32).reshape(S5_GROUPS, S5_GROUP) * uf
    y = jax.nn.gelu(y.reshape(bsz, seq_len, S5_WIDTH))
    y = y * jax.nn.sigmoid(y @ w_glu.astype(jnp.float32) + b_glu.astype(jnp.float32))
    return y.astype(u.dtype)


def conv_ffn(x, w_up, conv_w, conv_b, w_down):
    seq_len = x.shape[1]
    hid = x @ w_up
    pad = jnp.pad(hid, ((0, 0), (CONV_WIDTH - 1, 0), (0, 0)))
    hid = conv_b + sum(pad[:, j:j + seq_len] * conv_w[j] for j in range(CONV_WIDTH))
    gate, val = jnp.split(hid, 2, axis=-1)
    return (jax.nn.silu(gate) * val) @ w_down


def setup_inputs(seed: int = 0) -> dict:
    key = jax.random.key(seed)
    ks = jax.random.split(key, 24)
    f32 = jnp.float32

    def nrm(k, shape, s):
        return jax.random.normal(k, shape, f32) * s

    def gain(k, shape):
        return 1.0 + 0.02 * jax.random.normal(k, shape, f32)

    a_im = jnp.broadcast_to(jnp.pi * jnp.arange(S5_STATE, dtype=f32), (DEPTH, S5_GROUPS, S5_STATE))
    return {
        "x": nrm(ks[0], (BATCH, SEQ, D_MODEL), 1.0),
        "in_norm_g": gain(ks[1], (DEPTH, D_MODEL)),
        "w_in": nrm(ks[2], (DEPTH, D_MODEL, IN_WIDTH), D_MODEL ** -0.5),
        "hg_lb": nrm(ks[3], (DEPTH + 1, HG_WIDTH), 0.1),
        "hg_norm_g": gain(ks[4], (DEPTH, HG_WIDTH)),
        "s5_a_re": -0.5 * jnp.exp(nrm(ks[5], (DEPTH, S5_GROUPS, S5_STATE), 0.02)),
        "s5_a_im": jnp.array(a_im),
        "s5_log_dt": jax.random.uniform(ks[6], (DEPTH, S5_GROUPS), f32, math.log(DT_MIN), math.log(DT_MAX)),
        "s5_b_re": nrm(ks[7], (DEPTH, S5_GROUPS, S5_STATE, S5_GROUP), (2 * S5_GROUP) ** -0.5),
        "s5_b_im": nrm(ks[8], (DEPTH, S5_GROUPS, S5_STATE, S5_GROUP), (2 * S5_GROUP) ** -0.5),
        "s5_c_re": nrm(ks[9], (DEPTH, S5_GROUPS, S5_GROUP, S5_STATE), S5_STATE ** -0.5),
        "s5_c_im": nrm(ks[10], (DEPTH, S5_GROUPS, S5_GROUP, S5_STATE), S5_STATE ** -0.5),
        "s5_d": nrm(ks[11], (DEPTH, S5_WIDTH), 1.0),
        "s5_w_glu": nrm(ks[12], (DEPTH, S5_WIDTH, S5_WIDTH), S5_WIDTH ** -0.5),
        "s5_b_glu": nrm(ks[13], (DEPTH, S5_WIDTH), 0.01),
        "w_out": nrm(ks[14], (DEPTH, MIX_WIDTH, D_MODEL), MIX_WIDTH ** -0.5),
        "ffn_norm_g": gain(ks[15], (DEPTH, D_MODEL)),
        "w_up": nrm(ks[16], (DEPTH, D_MODEL, 2 * D_FF), D_MODEL ** -0.5),
        "conv_w": nrm(ks[17], (DEPTH, CONV_WIDTH, 2 * D_FF), CONV_WIDTH ** -0.5),
        "conv_b": nrm(ks[18], (DEPTH, 2 * D_FF), 0.01),
        "w_down": nrm(ks[19], (DEPTH, D_FF, D_MODEL), D_FF ** -0.5),
        "final_norm_g": gain(ks[20], (D_MODEL,)),
    }


def reference(x, in_norm_g, w_in, hg_lb, hg_norm_g, s5_a_re, s5_a_im, s5_log_dt, s5_b_re, s5_b_im,
              s5_c_re, s5_c_im, s5_d, s5_w_glu, s5_b_glu, w_out, ffn_norm_g, w_up, conv_w, conv_b,
              w_down, final_norm_g):
    h = x
    lb_all = jnp.cumsum(jax.nn.softmax(hg_lb.astype(jnp.float32), axis=0), axis=0)
    for layer in range(DEPTH):
        xn = rmsnorm(h, in_norm_g[layer])
        proj = xn @ w_in[layer]
        q, fz, v, gz, u = jnp.split(proj, [HG_WIDTH, 2 * HG_WIDTH, 3 * HG_WIDTH, 4 * HG_WIDTH], axis=-1)
        o_hg = hgrn2_mix(q, fz, v, gz, lb_all[layer], hg_norm_g[layer])
        o_s5 = s5_mix(u, s5_a_re[layer], s5_a_im[layer], s5_log_dt[layer], s5_b_re[layer], s5_b_im[layer],
                      s5_c_re[layer], s5_c_im[layer], s5_d[layer], s5_w_glu[layer], s5_b_glu[layer])
        h = h + jnp.concatenate([o_hg, o_s5], axis=-1) @ w_out[layer]
        hn = rmsnorm(h, ffn_norm_g[layer])
        h = h + conv_ffn(hn, w_up[layer], conv_w[layer], conv_b[layer], w_down[layer])
    return rmsnorm(h, final_norm_g)
```

```python
import functools

import jax
import jax.numpy as jnp
from jax import lax
from jax.experimental import pallas as pl
from jax.experimental.pallas import tpu as pltpu

EPS = 1e-6
HG_CHUNK = 64
HG_HEAD_DIM = 128
S5_GROUP = 16
S5_STATE = 64
S5_SEG = 64
S5_SEQS = 8
CONV_WIDTH = 3
SUBLANES = 8
LANES = 128
VMEM_LIMIT = 56 * 1024 * 1024

F32 = jnp.float32
BF16 = jnp.bfloat16

_NT = (((1,), (1,)), ((), ()))
_TN = (((0,), (0,)), ((), ()))


def _rms(x, g):
    ms = jnp.mean(x * x, axis=-1, keepdims=True)
    return x * lax.rsqrt(ms + EPS) * g


def _cparams(*sem):
    return pltpu.CompilerParams(dimension_semantics=sem, vmem_limit_bytes=VMEM_LIMIT)


def _param_kernel(lb_ref, are_ref, aim_ref, ldt_ref, bre_ref, bim_ref,
                  lb_out, abr_out, abi_out, bbr_out, bbi_out):
    z = lb_ref[...]
    e = jnp.exp(z - jnp.max(z, axis=0, keepdims=True))
    sm = e / jnp.sum(e, axis=0, keepdims=True)
    lb_out[...] = sm[0:1, :]
    dt = jnp.exp(ldt_ref[...])
    ar = are_ref[...]
    ai = aim_ref[...]
    mag = jnp.exp(dt * ar)
    abr = mag * jnp.cos(dt * ai)
    abi = mag * jnp.sin(dt * ai)
    nr = abr - 1.0
    den = ar * ar + ai * ai
    zr = (nr * ar + abi * ai) / den
    zi = (abi * ar - nr * ai) / den
    br = bre_ref[...]
    bi = bim_ref[...]
    abr_out[...] = abr
    abi_out[...] = abi
    bbr_out[...] = zr * br - zi * bi
    bbi_out[...] = zr * bi + zi * br


def _inproj_kernel(x_ref, g_ref, w_ref, q_ref, f_ref, v_ref, gz_ref, u_ref, *, width):
    xb = _rms(x_ref[...], g_ref[...]).astype(BF16)
    for i, o in enumerate((q_ref, f_ref, v_ref, gz_ref)):
        o[...] = jnp.dot(xb, w_ref[:, i * width:(i + 1) * width], preferred_element_type=F32)
    u = jnp.dot(xb, w_ref[:, 4 * width:5 * width], preferred_element_type=F32)
    for j in range(u_ref.shape[0]):
        for k in range(2):
            r0 = (2 * j + k) * S5_SEG
            u_ref[j, :, k * width:(k + 1) * width] = u[r0:r0 + S5_SEG, :]


def _hgrn2_kernel(q_ref, f_ref, v_ref, g_ref, lb_ref, ng_ref, o_ref, st_ref, *, n_chunks, n_heads):
    @pl.when(pl.program_id(1) == 0)
    def _():
        st_ref[...] = jnp.zeros_like(st_ref)

    lb = lb_ref[...]
    ng = ng_ref[...]
    c = HG_CHUNK
    tri = lax.broadcasted_iota(jnp.int32, (c, c), 0) >= lax.broadcasted_iota(jnp.int32, (c, c), 1)
    tri_f = tri.astype(F32)

    def chunk(ci, carry):
        rows = pl.ds(pl.multiple_of(ci * c, c), c)
        f = lb + (1.0 - lb) * jax.nn.sigmoid(f_ref[rows, :])
        k = 1.0 - f
        b = jnp.dot(tri_f, jnp.log(f), precision=lax.Precision.HIGHEST, preferred_element_type=F32)
        dec = jnp.exp(b[c - 1:c, :])
        kd_f = k * jnp.exp(-b)
        qd = (q_ref[rows, :] * jnp.exp(b)).astype(BF16)
        kd = kd_f.astype(BF16)
        kt = (kd_f * dec).astype(BF16)
        vb = v_ref[rows, :].astype(BF16)
        gz = g_ref[rows, :]
        outs = []
        for hd in range(n_heads):
            cs = slice(hd * HG_HEAD_DIM, (hd + 1) * HG_HEAD_DIM)
            att = lax.dot_general(qd[:, cs], kd[:, cs], _NT, preferred_element_type=F32)
            att = jnp.where(tri, att, 0.0)
            o = jnp.dot(att.astype(BF16), vb[:, cs], preferred_element_type=F32)
            st = st_ref[hd]
            o = o + lax.dot_general(qd[:, cs], st.astype(BF16), _NT, preferred_element_type=F32)
            ds = lax.dot_general(vb[:, cs], kt[:, cs], _TN, preferred_element_type=F32)
            st_ref[hd] = st * dec[:, cs] + ds
            outs.append(o * lax.rsqrt(jnp.mean(o * o, axis=-1, keepdims=True) + EPS))
        o_ref[rows, :] = jnp.concatenate(outs, axis=1) * ng * (gz * jax.nn.sigmoid(gz))
        return carry

    lax.fori_loop(0, n_chunks, chunk, 0)


def _s5_kernel(u_ref, wb_ref, wc_ref, are_ref, aim_ref, d_ref, wglu_ref, bglu_ref, y_ref,
               bu_ref, pre_ref, pim_ref, carry_ref, *, n_blocks):
    h = S5_SEG
    s = S5_SEQS
    half = wb_ref.shape[2] // 2
    kw = wb_ref.shape[1]

    @pl.when(pl.program_id(0) == 0)
    def _():
        carry_ref[...] = jnp.zeros_like(carry_ref)
        ar = are_ref[...]
        ai = aim_ref[...]
        pre_ref[0:s, :] = ar
        pim_ref[0:s, :] = ai

        def powers(i, c):
            pr, pi = c
            npr = pr * ar - pi * ai
            npi = pr * ai + pi * ar
            r = pl.ds(pl.multiple_of(i * s, s), s)
            pre_ref[r, :] = npr
            pim_ref[r, :] = npi
            return npr, npi

        lax.fori_loop(1, h, powers, (ar, ai))

    u = u_ref[...]
    ub = u.astype(BF16)
    for m in range(n_blocks):
        bu_ref[:, 2 * half * m:2 * half * (m + 1)] = jnp.dot(
            ub[:, kw * m:kw * (m + 1)], wb_ref[m], preferred_element_type=F32)

    second = lax.broadcasted_iota(jnp.int32, (s, half), 0) % 2 == 1

    for m in range(n_blocks):
        lo = 2 * half * m
        re = slice(lo, lo + half)
        im = slice(lo + half, lo + 2 * half)
        am = slice(half * m, half * (m + 1))
        ar = are_ref[:, am]
        ai = aim_ref[:, am]

        def scan(i, c, re=re, im=im, ar=ar, ai=ai):
            xr, xi = c
            r = pl.ds(pl.multiple_of(i * s, s), s)
            nxr = ar * xr - ai * xi + bu_ref[r, re]
            nxi = ar * xi + ai * xr + bu_ref[r, im]
            bu_ref[r, re] = nxr
            bu_ref[r, im] = nxi
            return nxr, nxi

        er, ei = lax.fori_loop(0, h, scan, (carry_ref[:, re], carry_ref[:, im]), unroll=4)
        cr = jnp.where(second, pltpu.roll(er, 1, axis=0), 0.0)
        ci = jnp.where(second, pltpu.roll(ei, 1, axis=0), 0.0)

        def fix(i, c, re=re, im=im, am=am, cr=cr, ci=ci):
            r = pl.ds(pl.multiple_of(i * s, s), s)
            pr = pre_ref[r, am]
            pi = pim_ref[r, am]
            bu_ref[r, re] = bu_ref[r, re] + (pr * cr - pi * ci)
            bu_ref[r, im] = bu_ref[r, im] + (pr * ci + pi * cr)
            return c

        lax.fori_loop(0, h, fix, 0, unroll=4)
        last = slice((h - 1) * s, h * s)
        carry_ref[:, re] = jnp.where(second, 0.0, pltpu.roll(bu_ref[last, re], s - 1, axis=0))
        carry_ref[:, im] = jnp.where(second, 0.0, pltpu.roll(bu_ref[last, im], s - 1, axis=0))

    ys = [jnp.dot(bu_ref[:, 2 * half * m:2 * half * (m + 1)].astype(BF16), wc_ref[m],
                  preferred_element_type=F32) for m in range(n_blocks)]
    y = jnp.concatenate(ys, axis=1) + d_ref[...] * u
    y = jax.nn.gelu(y)
    z = jnp.dot(y.astype(BF16), wglu_ref[...], preferred_element_type=F32) + bglu_ref[...]
    y_ref[...] = y * jax.nn.sigmoid(z)


def _outproj_kernel(x_ref, ohg_ref, y_ref, w_ref, g_ref, h_ref, hn_ref, *, width):
    pieces = [y_ref[j, :, k * width:(k + 1) * width] for j in range(y_ref.shape[0]) for k in range(2)]
    ys = jnp.concatenate(pieces, axis=0)
    acc = jnp.dot(ohg_ref[...].astype(BF16), w_ref[0:width, :], preferred_element_type=F32)
    acc = acc + jnp.dot(ys.astype(BF16), w_ref[width:2 * width, :], preferred_element_type=F32)
    hres = x_ref[...] + acc
    h_ref[...] = hres
    hn_ref[...] = _rms(hres, g_ref[...]).astype(BF16)


def _ffn_kernel(hn_ref, h_ref, wup_ref, cw_ref, cb_ref, wdn_ref, g_ref, o_ref, tail_ref, *, d_ff, bf):
    @pl.when(pl.program_id(1) == 0)
    def _():
        tail_ref[...] = jnp.zeros_like(tail_ref)

    hn = hn_ref[...]
    tm = hn.shape[0]
    row = lax.broadcasted_iota(jnp.int32, (tm, bf), 0)

    def conv(off):
        pre = jnp.dot(hn, wup_ref[:, off:off + bf], preferred_element_type=F32)
        prev = tail_ref[:, off:off + bf]
        p1 = prev[SUBLANES - 1:SUBLANES, :]
        p2 = prev[SUBLANES - 2:SUBLANES - 1, :]
        s1 = jnp.where(row == 0, p1, pltpu.roll(pre, 1, axis=0))
        s2 = jnp.where(row == 0, p2, jnp.where(row == 1, p1, pltpu.roll(pre, 2, axis=0)))
        tail_ref[:, off:off + bf] = pre[tm - SUBLANES:tm, :]
        cw = cw_ref[:, off:off + bf]
        return cb_ref[:, off:off + bf] + cw[0:1, :] * s2 + cw[1:2, :] * s1 + cw[2:3, :] * pre

    acc = jnp.zeros(o_ref.shape, F32)
    for j in range(d_ff // bf):
        gate = conv(j * bf)
        val = conv(d_ff + j * bf)
        act = (gate * jax.nn.sigmoid(gate)) * val
        acc = acc + jnp.dot(act.astype(BF16), wdn_ref[j * bf:(j + 1) * bf, :], preferred_element_type=F32)
    o_ref[...] = _rms(h_ref[...] + acc, g_ref[...])


def _tile_sizes(seq_len):
    tm = min(512, seq_len)
    tl = min(256, seq_len)
    assert seq_len % tm == 0 and seq_len % tl == 0 and tm % (2 * S5_SEG) == 0 and tl % HG_CHUNK == 0
    return tm, tl


def kernel(x, in_norm_g, w_in, hg_lb, hg_norm_g, s5_a_re, s5_a_im, s5_log_dt, s5_b_re, s5_b_im,
           s5_c_re, s5_c_im, s5_d, s5_w_glu, s5_b_glu, w_out, ffn_norm_g, w_up, conv_w, conv_b,
           w_down, final_norm_g):
    bsz, seq_len, d_model = x.shape
    depth = w_in.shape[0]
    assert depth == 1, "single-layer block"
    width = hg_norm_g.shape[1]
    assert w_in.shape[2] == 5 * width and s5_d.shape[1] == width
    n_heads = width // HG_HEAD_DIM
    groups, n_state, n_chan = s5_b_re.shape[1:]
    assert (n_state, n_chan) == (S5_STATE, S5_GROUP) and groups * n_chan == width
    assert 2 * bsz == S5_SEQS
    d_ff = w_down.shape[1]
    tm, tl = _tile_sizes(seq_len)
    n_tiles = seq_len // (2 * S5_SEG)
    tiles_per_tm = tm // (2 * S5_SEG)
    row = lambda a: a.reshape(1, -1)

    gp = groups * n_chan
    rep = lambda a: jnp.repeat(a, n_chan, axis=0)
    ldt = jnp.broadcast_to(jnp.repeat(s5_log_dt[0], n_chan)[:, None], (gp, n_state))
    b_t = lambda a: a.transpose(0, 2, 1).reshape(gp, n_state)
    full = lambda shape: pl.BlockSpec(shape, lambda *_: (0,) * len(shape))
    lb, abr, abi, bbr, bbi = pl.pallas_call(
        _param_kernel,
        out_shape=[jax.ShapeDtypeStruct((1, width), F32)] + [jax.ShapeDtypeStruct((gp, n_state), F32)] * 4,
        name="params",
    )(hg_lb, rep(s5_a_re[0]), rep(s5_a_im[0]), ldt, b_t(s5_b_re[0]), b_t(s5_b_im[0]))

    gpb = LANES // n_chan
    n_blocks = groups // gpb
    eye = jnp.eye(gpb, dtype=F32)
    blockdiag_in = lambda w: jnp.einsum("mgpn,gh->mgphn", w.reshape(n_blocks, gpb, n_chan, n_state),
                                        eye).reshape(n_blocks, gpb * n_chan, gpb * n_state)
    blockdiag_out = lambda w: jnp.einsum("mgpn,gh->mhngp", w.reshape(n_blocks, gpb, n_chan, n_state),
                                         eye).reshape(n_blocks, gpb * n_state, gpb * n_chan)
    wb = jnp.concatenate([blockdiag_in(bbr), blockdiag_in(bbi)], axis=2).astype(BF16)
    wc = jnp.concatenate([blockdiag_out(s5_c_re[0]), -blockdiag_out(s5_c_im[0])], axis=1).astype(BF16)
    n_lanes = groups * n_state
    a_rows = lambda a: jnp.broadcast_to(a.reshape(groups, n_chan, n_state)[:, 0, :].reshape(1, n_lanes),
                                        (S5_SEQS, n_lanes))
    a_re8, a_im8 = a_rows(abr), a_rows(abi)

    tok = lambda w: pl.BlockSpec((None, tm, w), lambda b, t: (b, t, 0))
    const2 = lambda shape: pl.BlockSpec(shape, lambda b, t: (0,) * len(shape))
    il_spec = pl.BlockSpec((tiles_per_tm, S5_SEG, 2 * width), lambda b, t: (t, 0, b))
    q, fz, v, gz, u_il = pl.pallas_call(
        functools.partial(_inproj_kernel, width=width),
        grid=(bsz, seq_len // tm),
        in_specs=[tok(d_model), const2((1, d_model)), const2((d_model, 5 * width))],
        out_specs=[tok(width)] * 4 + [il_spec],
        out_shape=[jax.ShapeDtypeStruct((bsz, seq_len, width), F32)] * 4
        + [jax.ShapeDtypeStruct((n_tiles, S5_SEG, S5_SEQS * width), F32)],
        compiler_params=_cparams("parallel", "parallel"),
        name="in_proj",
    )(x, row(in_norm_g[0]), w_in[0].astype(BF16))

    ttok = pl.BlockSpec((None, tl, width), lambda b, t: (b, t, 0))
    o_hg = pl.pallas_call(
        functools.partial(_hgrn2_kernel, n_chunks=tl // HG_CHUNK, n_heads=n_heads),
        grid=(bsz, seq_len // tl),
        in_specs=[ttok] * 4 + [const2((1, width))] * 2,
        out_specs=ttok,
        out_shape=jax.ShapeDtypeStruct((bsz, seq_len, width), F32),
        scratch_shapes=[pltpu.VMEM((n_heads, HG_HEAD_DIM, HG_HEAD_DIM), F32)],
        compiler_params=_cparams("arbitrary", "arbitrary"),
        name="hgrn2",
    )(q, fz, v, gz, lb, row(hg_norm_g[0]))

    m_rows = S5_SEG * S5_SEQS
    c1 = lambda shape: pl.BlockSpec(shape, lambda j: (0,) * len(shape))
    il_rows = pl.BlockSpec((None, m_rows, width), lambda j: (j, 0, 0))
    y_il = pl.pallas_call(
        functools.partial(_s5_kernel, n_blocks=n_blocks),
        grid=(n_tiles,),
        in_specs=[il_rows, c1(wb.shape), c1(wc.shape), c1(a_re8.shape), c1(a_im8.shape),
                  c1((1, width)), c1((width, width)), c1((1, width))],
        out_specs=il_rows,
        out_shape=jax.ShapeDtypeStruct((n_tiles, m_rows, width), F32),
        scratch_shapes=[pltpu.VMEM((m_rows, 2 * n_lanes), F32),
                        pltpu.VMEM((m_rows, n_lanes), F32), pltpu.VMEM((m_rows, n_lanes), F32),
                        pltpu.VMEM((S5_SEQS, 2 * n_lanes), F32)],
        compiler_params=_cparams("arbitrary"),
        name="s5",
    )(u_il.reshape(n_tiles, m_rows, width), wb, wc, a_re8, a_im8, row(s5_d[0]),
      s5_w_glu[0].astype(BF16), row(s5_b_glu[0]))

    h1, hn = pl.pallas_call(
        functools.partial(_outproj_kernel, width=width),
        grid=(bsz, seq_len // tm),
        in_specs=[tok(d_model), tok(width), il_spec, const2((2 * width, d_model)), const2((1, d_model))],
        out_specs=[tok(d_model), tok(d_model)],
        out_shape=[jax.ShapeDtypeStruct((bsz, seq_len, d_model), F32),
                   jax.ShapeDtypeStruct((bsz, seq_len, d_model), BF16)],
        compiler_params=_cparams("parallel", "parallel"),
        name="out_proj",
    )(x, o_hg, y_il.reshape(n_tiles, S5_SEG, S5_SEQS * width), w_out[0].astype(BF16), row(ffn_norm_g[0]))

    bf = 256
    assert d_ff % bf == 0
    out = pl.pallas_call(
        functools.partial(_ffn_kernel, d_ff=d_ff, bf=bf),
        grid=(bsz, seq_len // tm),
        in_specs=[tok(d_model), tok(d_model), const2((d_model, 2 * d_ff)), const2((CONV_WIDTH, 2 * d_ff)),
                  const2((1, 2 * d_ff)), const2((d_ff, d_model)), const2((1, d_model))],
        out_specs=tok(d_model),
        out_shape=jax.ShapeDtypeStruct((bsz, seq_len, d_model), F32),
        scratch_shapes=[pltpu.VMEM((SUBLANES, 2 * d_ff), F32)],
        compiler_params=_cparams("arbitrary", "arbitrary"),
        name="ffn",
    )(hn, h1, w_up[0].astype(BF16), conv_w[0], row(conv_b[0]), w_down[0].astype(BF16), row(final_norm_g))
    return out
```

```python
import functools

import jax
import jax.numpy as jnp
from jax import lax
from jax.experimental import pallas as pl
from jax.experimental.pallas import tpu as pltpu

EPS = 1e-6
HG_CHUNK = 64
HG_HEAD_DIM = 128
S5_GROUP = 16
S5_STATE = 64
S5_SEG = 64
S5_SEQS = 8
S5_SCAN_BLOCKS = 2
S5_SCAN_UNROLL = 8
CONV_WIDTH = 3
SUBLANES = 8
LANES = 128
VMEM_LIMIT = 56 * 1024 * 1024

F32 = jnp.float32
BF16 = jnp.bfloat16

_NT = (((1,), (1,)), ((), ()))
_TN = (((0,), (0,)), ((), ()))


def _rms(x, g):
    ms = jnp.mean(x * x, axis=-1, keepdims=True)
    return x * lax.rsqrt(ms + EPS) * g


def _cparams(*sem):
    return pltpu.CompilerParams(dimension_semantics=sem, vmem_limit_bytes=VMEM_LIMIT)


def _param_kernel(lb_ref, are_ref, aim_ref, ldt_ref, bre_ref, bim_ref,
                  lb_out, abr_out, abi_out, bbr_out, bbi_out):
    z = lb_ref[...]
    e = jnp.exp(z - jnp.max(z, axis=0, keepdims=True))
    sm = e / jnp.sum(e, axis=0, keepdims=True)
    lb_out[...] = sm[0:1, :]
    dt = jnp.exp(ldt_ref[...])
    ar = are_ref[...]
    ai = aim_ref[...]
    mag = jnp.exp(dt * ar)
    abr = mag * jnp.cos(dt * ai)
    abi = mag * jnp.sin(dt * ai)
    nr = abr - 1.0
    den = ar * ar + ai * ai
    zr = (nr * ar + abi * ai) / den
    zi = (abi * ar - nr * ai) / den
    br = bre_ref[...]
    bi = bim_ref[...]
    abr_out[...] = abr
    abi_out[...] = abi
    bbr_out[...] = zr * br - zi * bi
    bbi_out[...] = zr * bi + zi * br


def _inproj_kernel(x_ref, g_ref, w_ref, q_ref, f_ref, v_ref, gz_ref, u_ref, *, width):
    xb = _rms(x_ref[...], g_ref[...]).astype(BF16)
    for i, o in enumerate((q_ref, f_ref, v_ref, gz_ref, u_ref)):
        o[...] = jnp.dot(xb, w_ref[:, i * width:(i + 1) * width], preferred_element_type=F32)


def _hgrn2_kernel(q_ref, f_ref, v_ref, g_ref, lb_ref, ng_ref, o_ref, st_ref, *, n_chunks, n_heads):
    @pl.when(pl.program_id(1) == 0)
    def _():
        st_ref[...] = jnp.zeros_like(st_ref)

    lb = lb_ref[...]
    ng = ng_ref[...]
    c = HG_CHUNK
    tri = lax.broadcasted_iota(jnp.int32, (c, c), 0) >= lax.broadcasted_iota(jnp.int32, (c, c), 1)
    tri_b = tri.astype(BF16)

    f = lb + (1.0 - lb) * jax.nn.sigmoid(f_ref[...])
    k = 1.0 - f
    logf = jnp.log(f)
    hi = logf.astype(BF16)
    r1 = logf - hi.astype(F32)
    mid = r1.astype(BF16)
    lo = (r1 - mid.astype(F32)).astype(BF16)
    bs = []
    for ci in range(n_chunks):
        rows = slice(ci * c, (ci + 1) * c)
        bs.append(jnp.dot(tri_b, hi[rows], preferred_element_type=F32)
                  + (jnp.dot(tri_b, mid[rows], preferred_element_type=F32)
                     + jnp.dot(tri_b, lo[rows], preferred_element_type=F32)))
    b = jnp.concatenate(bs, axis=0)
    kd_f = k * jnp.exp(-b)
    qd = (q_ref[...] * jnp.exp(b)).astype(BF16)
    kd = kd_f.astype(BF16)
    vb = v_ref[...].astype(BF16)
    gz = g_ref[...]
    gate = ng * (gz * jax.nn.sigmoid(gz))
    decs = [jnp.exp(b[(ci + 1) * c - 1:(ci + 1) * c, :]) for ci in range(n_chunks)]
    kts = [(kd_f[ci * c:(ci + 1) * c] * decs[ci]).astype(BF16) for ci in range(n_chunks)]

    for hd in range(n_heads):
        cs = slice(hd * HG_HEAD_DIM, (hd + 1) * HG_HEAD_DIM)
        st = st_ref[hd]
        for ci in range(n_chunks):
            rows = slice(ci * c, (ci + 1) * c)
            att = lax.dot_general(qd[rows, cs], kd[rows, cs], _NT, preferred_element_type=F32)
            att = jnp.where(tri, att, 0.0)
            o = jnp.dot(att.astype(BF16), vb[rows, cs], preferred_element_type=F32)
            o = o + lax.dot_general(qd[rows, cs], st.astype(BF16), _NT, preferred_element_type=F32)
            ds = lax.dot_general(vb[rows, cs], kts[ci][:, cs], _TN, preferred_element_type=F32)
            st = st * decs[ci][:, cs] + ds
            o = o * lax.rsqrt(jnp.mean(o * o, axis=-1, keepdims=True) + EPS)
            o_ref[rows, cs] = o * gate[rows, cs]
        st_ref[hd] = st


def _s5_kernel(u_ref, wb_ref, wc_ref, are_ref, aim_ref, d_ref, wglu_ref, bglu_ref, y_ref,
               il_ref, bu_ref, pre_ref, pim_ref, carry_ref, *, n_blocks):
    h = S5_SEG
    s = S5_SEQS
    half = wb_ref.shape[2] // 2
    kw = wb_ref.shape[1]

    @pl.when(pl.program_id(0) == 0)
    def _():
        carry_ref[...] = jnp.zeros_like(carry_ref)
        ar = are_ref[...]
        ai = aim_ref[...]
        pre_ref[0:s, :] = ar
        pim_ref[0:s, :] = ai

        def powers(i, c):
            pr, pi = c
            npr = pr * ar - pi * ai
            npi = pr * ai + pi * ar
            r = pl.ds(pl.multiple_of(i * s, s), s)
            pre_ref[r, :] = npr
            pim_ref[r, :] = npi
            return npr, npi

        lax.fori_loop(1, h, powers, (ar, ai))

    n_slabs = il_ref.shape[0]
    for bi in range(u_ref.shape[0]):
        for k in range(2):
            for j in range(n_slabs):
                il_ref[j, pl.ds(2 * bi + k, h, stride=s), :] = u_ref[bi, k * h:(k + 1) * h,
                                                                     j * LANES:(j + 1) * LANES]
    u = jnp.concatenate([il_ref[j] for j in range(n_slabs)], axis=1)
    ub = u.astype(BF16)
    for m in range(n_blocks):
        bu_ref[:, 2 * half * m:2 * half * (m + 1)] = jnp.dot(
            ub[:, kw * m:kw * (m + 1)], wb_ref[m], preferred_element_type=F32)

    second = lax.broadcasted_iota(jnp.int32, (s, half), 0) % 2 == 1
    last = slice((h - 1) * s, h * s)

    for m0 in range(0, n_blocks, S5_SCAN_BLOCKS):
        blocks = []
        for m in range(m0, m0 + S5_SCAN_BLOCKS):
            lo = 2 * half * m
            blocks.append((slice(lo, lo + half), slice(lo + half, lo + 2 * half),
                           slice(half * m, half * (m + 1))))
        coef = [(are_ref[:, am], aim_ref[:, am]) for _, _, am in blocks]

        def scan(i, c, blocks=blocks, coef=coef):
            r = pl.ds(pl.multiple_of(i * s, s), s)
            out = []
            for (re, im, _), (ar, ai), (xr, xi) in zip(blocks, coef, c):
                nxr = ar * xr - ai * xi + bu_ref[r, re]
                nxi = ar * xi + ai * xr + bu_ref[r, im]
                bu_ref[r, re] = nxr
                bu_ref[r, im] = nxi
                out.append((nxr, nxi))
            return tuple(out)

        init = tuple((carry_ref[:, re], carry_ref[:, im]) for re, im, _ in blocks)
        ends = lax.fori_loop(0, h, scan, init, unroll=S5_SCAN_UNROLL)
        starts = [(jnp.where(second, pltpu.roll(er, 1, axis=0), 0.0),
                   jnp.where(second, pltpu.roll(ei, 1, axis=0), 0.0)) for er, ei in ends]

        def fix(i, c, blocks=blocks, starts=starts):
            r = pl.ds(pl.multiple_of(i * s, s), s)
            for (re, im, am), (cr, ci) in zip(blocks, starts):
                pr = pre_ref[r, am]
                pi = pim_ref[r, am]
                bu_ref[r, re] = bu_ref[r, re] + (pr * cr - pi * ci)
                bu_ref[r, im] = bu_ref[r, im] + (pr * ci + pi * cr)
            return c

        lax.fori_loop(0, h, fix, 0, unroll=S5_SCAN_UNROLL)
        for re, im, _ in blocks:
            carry_ref[:, re] = jnp.where(second, 0.0, pltpu.roll(bu_ref[last, re], s - 1, axis=0))
            carry_ref[:, im] = jnp.where(second, 0.0, pltpu.roll(bu_ref[last, im], s - 1, axis=0))

    ys = [jnp.dot(bu_ref[:, 2 * half * m:2 * half * (m + 1)].astype(BF16), wc_ref[m],
                  preferred_element_type=F32) for m in range(n_blocks)]
    y = jnp.concatenate(ys, axis=1) + d_ref[...] * u
    y = jax.nn.gelu(y)
    z = jnp.dot(y.astype(BF16), wglu_ref[...], preferred_element_type=F32) + bglu_ref[...]
    y = y * jax.nn.sigmoid(z)
    for j in range(n_slabs):
        il_ref[j] = y[:, j * LANES:(j + 1) * LANES]
    for bi in range(y_ref.shape[0]):
        for k in range(2):
            for j in range(n_slabs):
                y_ref[bi, k * h:(k + 1) * h, j * LANES:(j + 1) * LANES] = il_ref[
                    j, pl.ds(2 * bi + k, h, stride=s), :]


def _outproj_kernel(x_ref, ohg_ref, y_ref, w_ref, g_ref, h_ref, hn_ref, *, width):
    acc = jnp.dot(ohg_ref[...].astype(BF16), w_ref[0:width, :], preferred_element_type=F32)
    acc = acc + jnp.dot(y_ref[...].astype(BF16), w_ref[width:2 * width, :], preferred_element_type=F32)
    hres = x_ref[...] + acc
    h_ref[...] = hres
    hn_ref[...] = _rms(hres, g_ref[...]).astype(BF16)


def _ffn_kernel(hn_ref, h_ref, wup_ref, cw_ref, cb_ref, wdn_ref, g_ref, o_ref, tail_ref, *, d_ff, bf):
    @pl.when(pl.program_id(1) == 0)
    def _():
        tail_ref[...] = jnp.zeros_like(tail_ref)

    hn = hn_ref[...]
    tm = hn.shape[0]
    row = lax.broadcasted_iota(jnp.int32, (tm, bf), 0)

    def conv(off):
        pre = jnp.dot(hn, wup_ref[:, off:off + bf], preferred_element_type=F32)
        prev = tail_ref[:, off:off + bf]
        p1 = prev[SUBLANES - 1:SUBLANES, :]
        p2 = prev[SUBLANES - 2:SUBLANES - 1, :]
        s1 = jnp.where(row == 0, p1, pltpu.roll(pre, 1, axis=0))
        s2 = jnp.where(row == 0, p2, jnp.where(row == 1, p1, pltpu.roll(pre, 2, axis=0)))
        tail_ref[:, off:off + bf] = pre[tm - SUBLANES:tm, :]
        cw = cw_ref[:, off:off + bf]
        return cb_ref[:, off:off + bf] + cw[0:1, :] * s2 + cw[1:2, :] * s1 + cw[2:3, :] * pre

    acc = jnp.zeros(o_ref.shape, F32)
    for j in range(d_ff // bf):
        gate = conv(j * bf)
        val = conv(d_ff + j * bf)
        act = (gate * jax.nn.sigmoid(gate)) * val
        acc = acc + jnp.dot(act.astype(BF16), wdn_ref[j * bf:(j + 1) * bf, :], preferred_element_type=F32)
    o_ref[...] = _rms(h_ref[...] + acc, g_ref[...])


def _tile_sizes(seq_len):
    tm = min(512, seq_len)
    tl = min(256, seq_len)
    assert seq_len % tm == 0 and seq_len % tl == 0 and tm % (2 * S5_SEG) == 0 and tl % HG_CHUNK == 0
    return tm, tl


def kernel(x, in_norm_g, w_in, hg_lb, hg_norm_g, s5_a_re, s5_a_im, s5_log_dt, s5_b_re, s5_b_im,
           s5_c_re, s5_c_im, s5_d, s5_w_glu, s5_b_glu, w_out, ffn_norm_g, w_up, conv_w, conv_b,
           w_down, final_norm_g):
    bsz, seq_len, d_model = x.shape
    depth = w_in.shape[0]
    assert depth == 1, "single-layer block"
    width = hg_norm_g.shape[1]
    assert w_in.shape[2] == 5 * width and s5_d.shape[1] == width
    n_heads = width // HG_HEAD_DIM
    groups, n_state, n_chan = s5_b_re.shape[1:]
    assert (n_state, n_chan) == (S5_STATE, S5_GROUP) and groups * n_chan == width
    assert 2 * bsz == S5_SEQS
    d_ff = w_down.shape[1]
    tm, tl = _tile_sizes(seq_len)
    n_tiles = seq_len // (2 * S5_SEG)
    row = lambda a: a.reshape(1, -1)

    gp = groups * n_chan
    rep = lambda a: jnp.repeat(a, n_chan, axis=0)
    ldt = jnp.broadcast_to(jnp.repeat(s5_log_dt[0], n_chan)[:, None], (gp, n_state))
    b_t = lambda a: a.transpose(0, 2, 1).reshape(gp, n_state)
    full = lambda shape: pl.BlockSpec(shape, lambda *_: (0,) * len(shape))
    lb, abr, abi, bbr, bbi = pl.pallas_call(
        _param_kernel,
        out_shape=[jax.ShapeDtypeStruct((1, width), F32)] + [jax.ShapeDtypeStruct((gp, n_state), F32)] * 4,
        name="params",
    )(hg_lb, rep(s5_a_re[0]), rep(s5_a_im[0]), ldt, b_t(s5_b_re[0]), b_t(s5_b_im[0]))

    gpb = LANES // n_chan
    n_blocks = groups // gpb
    eye = jnp.eye(gpb, dtype=F32)
    blockdiag_in = lambda w: jnp.einsum("mgpn,gh->mgphn", w.reshape(n_blocks, gpb, n_chan, n_state),
                                        eye).reshape(n_blocks, gpb * n_chan, gpb * n_state)
    blockdiag_out = lambda w: jnp.einsum("mgpn,gh->mhngp", w.reshape(n_blocks, gpb, n_chan, n_state),
                                         eye).reshape(n_blocks, gpb * n_state, gpb * n_chan)
    wb = jnp.concatenate([blockdiag_in(bbr), blockdiag_in(bbi)], axis=2).astype(BF16)
    wc = jnp.concatenate([blockdiag_out(s5_c_re[0]), -blockdiag_out(s5_c_im[0])], axis=1).astype(BF16)
    n_lanes = groups * n_state
    a_rows = lambda a: jnp.broadcast_to(a.reshape(groups, n_chan, n_state)[:, 0, :].reshape(1, n_lanes),
                                        (S5_SEQS, n_lanes))
    a_re8, a_im8 = a_rows(abr), a_rows(abi)

    tok = lambda w: pl.BlockSpec((None, tm, w), lambda b, t: (b, t, 0))
    const2 = lambda shape: pl.BlockSpec(shape, lambda b, t: (0,) * len(shape))
    q, fz, v, gz, u = pl.pallas_call(
        functools.partial(_inproj_kernel, width=width),
        grid=(bsz, seq_len // tm),
        in_specs=[tok(d_model), const2((1, d_model)), const2((d_model, 5 * width))],
        out_specs=[tok(width)] * 5,
        out_shape=[jax.ShapeDtypeStruct((bsz, seq_len, width), F32)] * 5,
        compiler_params=_cparams("parallel", "parallel"),
        name="in_proj",
    )(x, row(in_norm_g[0]), w_in[0].astype(BF16))

    ttok = pl.BlockSpec((None, tl, width), lambda b, t: (b, t, 0))
    o_hg = pl.pallas_call(
        functools.partial(_hgrn2_kernel, n_chunks=tl // HG_CHUNK, n_heads=n_heads),
        grid=(bsz, seq_len // tl),
        in_specs=[ttok] * 4 + [const2((1, width))] * 2,
        out_specs=ttok,
        out_shape=jax.ShapeDtypeStruct((bsz, seq_len, width), F32),
        scratch_shapes=[pltpu.VMEM((n_heads, HG_HEAD_DIM, HG_HEAD_DIM), F32)],
        compiler_params=_cparams("arbitrary", "arbitrary"),
        name="hgrn2",
    )(q, fz, v, gz, lb, row(hg_norm_g[0]))

    m_rows = S5_SEG * S5_SEQS
    c1 = lambda shape: pl.BlockSpec(shape, lambda j: (0,) * len(shape))
    tile_spec = pl.BlockSpec((bsz, 2 * S5_SEG, width), lambda j: (0, j, 0))
    y_s5 = pl.pallas_call(
        functools.partial(_s5_kernel, n_blocks=n_blocks),
        grid=(n_tiles,),
        in_specs=[tile_spec, c1(wb.shape), c1(wc.shape), c1(a_re8.shape), c1(a_im8.shape),
                  c1((1, width)), c1((width, width)), c1((1, width))],
        out_specs=tile_spec,
        out_shape=jax.ShapeDtypeStruct((bsz, seq_len, width), F32),
        scratch_shapes=[pltpu.VMEM((width // LANES, m_rows, LANES), F32), pltpu.VMEM((m_rows, 2 * n_lanes), F32),
                        pltpu.VMEM((m_rows, n_lanes), F32), pltpu.VMEM((m_rows, n_lanes), F32),
                        pltpu.VMEM((S5_SEQS, 2 * n_lanes), F32)],
        compiler_params=_cparams("arbitrary"),
        name="s5",
    )(u, wb, wc, a_re8, a_im8, row(s5_d[0]), s5_w_glu[0].astype(BF16), row(s5_b_glu[0]))

    h1, hn = pl.pallas_call(
        functools.partial(_outproj_kernel, width=width),
        grid=(bsz, seq_len // tm),
        in_specs=[tok(d_model), tok(width), tok(width), const2((2 * width, d_model)), const2((1, d_model))],
        out_specs=[tok(d_model), tok(d_model)],
        out_shape=[jax.ShapeDtypeStruct((bsz, seq_len, d_model), F32),
                   jax.ShapeDtypeStruct((bsz, seq_len, d_model), BF16)],
        compiler_params=_cparams("parallel", "parallel"),
        name="out_proj",
    )(x, o_hg, y_s5, w_out[0].astype(BF16), row(ffn_norm_g[0]))

    bf = 256
    assert d_ff % bf == 0
    out = pl.pallas_call(
        functools.partial(_ffn_kernel, d_ff=d_ff, bf=bf),
        grid=(bsz, seq_len // tm),
        in_specs=[tok(d_model), tok(d_model), const2((d_model, 2 * d_ff)), const2((CONV_WIDTH, 2 * d_ff)),
                  const2((1, 2 * d_ff)), const2((d_ff, d_model)), const2((1, d_model))],
        out_specs=tok(d_model),
        out_shape=jax.ShapeDtypeStruct((bsz, seq_len, d_model), F32),
        scratch_shapes=[pltpu.VMEM((SUBLANES, 2 * d_ff), F32)],
        compiler_params=_cparams("arbitrary", "arbitrary"),
        name="ffn",
    )(hn, h1, w_up[0].astype(BF16), conv_w[0], row(conv_b[0]), w_down[0].astype(BF16), row(final_norm_g))
    return out
```

```python
import functools

import jax
import jax.numpy as jnp
from jax import lax
from jax.experimental import pallas as pl
from jax.experimental.pallas import tpu as pltpu

EPS = 1e-6
HG_CHUNK = 64
HG_HEAD_DIM = 128
S5_GROUP = 16
S5_STATE = 64
S5_SEG = 64
S5_SEQS = 8
S5_SCAN_BLOCKS = 2
S5_SCAN_UNROLL = 8
CONV_WIDTH = 3
FFN_BLOCK = 256
FFN_ROWS = 16
SUBLANES = 8
LANES = 128
VMEM_LIMIT = 56 * 1024 * 1024

F32 = jnp.float32
BF16 = jnp.bfloat16

_NT = (((1,), (1,)), ((), ()))
_TN = (((0,), (0,)), ((), ()))


def _rms(x, g):
    ms = jnp.mean(x * x, axis=-1, keepdims=True)
    return x * lax.rsqrt(ms + EPS) * g


def _cparams(*sem):
    return pltpu.CompilerParams(dimension_semantics=sem, vmem_limit_bytes=VMEM_LIMIT)


def _param_kernel(lb_ref, are_ref, aim_ref, ldt_ref, bre_ref, bim_ref,
                  lb_out, abr_out, abi_out, bbr_out, bbi_out):
    z = lb_ref[...]
    e = jnp.exp(z - jnp.max(z, axis=0, keepdims=True))
    sm = e / jnp.sum(e, axis=0, keepdims=True)
    lb_out[...] = sm[0:1, :]
    dt = jnp.exp(ldt_ref[...])
    ar = are_ref[...]
    ai = aim_ref[...]
    mag = jnp.exp(dt * ar)
    abr = mag * jnp.cos(dt * ai)
    abi = mag * jnp.sin(dt * ai)
    nr = abr - 1.0
    den = ar * ar + ai * ai
    zr = (nr * ar + abi * ai) / den
    zi = (abi * ar - nr * ai) / den
    br = bre_ref[...]
    bi = bim_ref[...]
    abr_out[...] = abr
    abi_out[...] = abi
    bbr_out[...] = zr * br - zi * bi
    bbi_out[...] = zr * bi + zi * br


def _inproj_kernel(x_ref, g_ref, w_ref, q_ref, f_ref, v_ref, gz_ref, u_ref, *, width):
    xb = _rms(x_ref[...], g_ref[...]).astype(BF16)
    for i, o in enumerate((q_ref, f_ref, v_ref, gz_ref, u_ref)):
        o[...] = jnp.dot(xb, w_ref[:, i * width:(i + 1) * width], preferred_element_type=F32)


def _hgrn2_kernel(q_ref, f_ref, v_ref, g_ref, lb_ref, ng_ref, o_ref, st_ref, *, n_chunks, n_heads):
    @pl.when(pl.program_id(1) == 0)
    def _():
        st_ref[...] = jnp.zeros_like(st_ref)

    lb = lb_ref[...]
    ng = ng_ref[...]
    c = HG_CHUNK
    tri = lax.broadcasted_iota(jnp.int32, (c, c), 0) >= lax.broadcasted_iota(jnp.int32, (c, c), 1)
    tri_b = tri.astype(BF16)

    f = lb + (1.0 - lb) * jax.nn.sigmoid(f_ref[...])
    k = 1.0 - f
    logf = jnp.log(f)
    hi = logf.astype(BF16)
    r1 = logf - hi.astype(F32)
    mid = r1.astype(BF16)
    lo = (r1 - mid.astype(F32)).astype(BF16)
    bs = []
    for ci in range(n_chunks):
        rows = slice(ci * c, (ci + 1) * c)
        bs.append(jnp.dot(tri_b, hi[rows], preferred_element_type=F32)
                  + (jnp.dot(tri_b, mid[rows], preferred_element_type=F32)
                     + jnp.dot(tri_b, lo[rows], preferred_element_type=F32)))
    b = jnp.concatenate(bs, axis=0)
    kd_f = k * jnp.exp(-b)
    qd = (q_ref[...] * jnp.exp(b)).astype(BF16)
    kd = kd_f.astype(BF16)
    vb = v_ref[...].astype(BF16)
    gz = g_ref[...]
    gate = ng * (gz * jax.nn.sigmoid(gz))
    decs = [jnp.exp(b[(ci + 1) * c - 1:(ci + 1) * c, :]) for ci in range(n_chunks)]
    kts = [(kd_f[ci * c:(ci + 1) * c] * decs[ci]).astype(BF16) for ci in range(n_chunks)]

    for hd in range(n_heads):
        cs = slice(hd * HG_HEAD_DIM, (hd + 1) * HG_HEAD_DIM)
        st = st_ref[hd]
        for ci in range(n_chunks):
            rows = slice(ci * c, (ci + 1) * c)
            att = lax.dot_general(qd[rows, cs], kd[rows, cs], _NT, preferred_element_type=F32)
            att = jnp.where(tri, att, 0.0)
            o = jnp.dot(att.astype(BF16), vb[rows, cs], preferred_element_type=F32)
            o = o + lax.dot_general(qd[rows, cs], st.astype(BF16), _NT, preferred_element_type=F32)
            ds = lax.dot_general(vb[rows, cs], kts[ci][:, cs], _TN, preferred_element_type=F32)
            st = st * decs[ci][:, cs] + ds
            o = o * lax.rsqrt(jnp.mean(o * o, axis=-1, keepdims=True) + EPS)
            o_ref[rows, cs] = o * gate[rows, cs]
        st_ref[hd] = st


def _s5_kernel(u_ref, wb_ref, wc_ref, are_ref, aim_ref, d_ref, wglu_ref, bglu_ref, y_ref,
               il_ref, bu_ref, pre_ref, pim_ref, carry_ref, *, n_blocks):
    h = S5_SEG
    s = S5_SEQS
    half = wb_ref.shape[2] // 2
    kw = wb_ref.shape[1]

    @pl.when(pl.program_id(0) == 0)
    def _():
        carry_ref[...] = jnp.zeros_like(carry_ref)
        ar = are_ref[...]
        ai = aim_ref[...]
        pre_ref[0:s, :] = ar
        pim_ref[0:s, :] = ai

        def powers(i, c):
            pr, pi = c
            npr = pr * ar - pi * ai
            npi = pr * ai + pi * ar
            r = pl.ds(pl.multiple_of(i * s, s), s)
            pre_ref[r, :] = npr
            pim_ref[r, :] = npi
            return npr, npi

        lax.fori_loop(1, h, powers, (ar, ai))

    n_slabs = il_ref.shape[0]
    for bi in range(u_ref.shape[0]):
        for k in range(2):
            for j in range(n_slabs):
                il_ref[j, pl.ds(2 * bi + k, h, stride=s), :] = u_ref[bi, k * h:(k + 1) * h,
                                                                     j * LANES:(j + 1) * LANES]
    u = jnp.concatenate([il_ref[j] for j in range(n_slabs)], axis=1)
    ub = u.astype(BF16)
    for m in range(n_blocks):
        bu_ref[:, 2 * half * m:2 * half * (m + 1)] = jnp.dot(
            ub[:, kw * m:kw * (m + 1)], wb_ref[m], preferred_element_type=F32)

    second = lax.broadcasted_iota(jnp.int32, (s, half), 0) % 2 == 1
    last = slice((h - 1) * s, h * s)

    for m0 in range(0, n_blocks, S5_SCAN_BLOCKS):
        blocks = []
        for m in range(m0, m0 + S5_SCAN_BLOCKS):
            lo = 2 * half * m
            blocks.append((slice(lo, lo + half), slice(lo + half, lo + 2 * half),
                           slice(half * m, half * (m + 1))))
        coef = [(are_ref[:, am], aim_ref[:, am]) for _, _, am in blocks]

        def scan(i, c, blocks=blocks, coef=coef):
            r = pl.ds(pl.multiple_of(i * s, s), s)
            out = []
            for (re, im, _), (ar, ai), (xr, xi) in zip(blocks, coef, c):
                nxr = ar * xr - ai * xi + bu_ref[r, re]
                nxi = ar * xi + ai * xr + bu_ref[r, im]
                bu_ref[r, re] = nxr
                bu_ref[r, im] = nxi
                out.append((nxr, nxi))
            return tuple(out)

        init = tuple((carry_ref[:, re], carry_ref[:, im]) for re, im, _ in blocks)
        ends = lax.fori_loop(0, h, scan, init, unroll=S5_SCAN_UNROLL)
        starts = [(jnp.where(second, pltpu.roll(er, 1, axis=0), 0.0),
                   jnp.where(second, pltpu.roll(ei, 1, axis=0), 0.0)) for er, ei in ends]

        def fix(i, c, blocks=blocks, starts=starts):
            r = pl.ds(pl.multiple_of(i * s, s), s)
            for (re, im, am), (cr, ci) in zip(blocks, starts):
                pr = pre_ref[r, am]
                pi = pim_ref[r, am]
                bu_ref[r, re] = bu_ref[r, re] + (pr * cr - pi * ci)
                bu_ref[r, im] = bu_ref[r, im] + (pr * ci + pi * cr)
            return c

        lax.fori_loop(0, h, fix, 0, unroll=S5_SCAN_UNROLL)
        for re, im, _ in blocks:
            carry_ref[:, re] = jnp.where(second, 0.0, pltpu.roll(bu_ref[last, re], s - 1, axis=0))
            carry_ref[:, im] = jnp.where(second, 0.0, pltpu.roll(bu_ref[last, im], s - 1, axis=0))

    ys = [jnp.dot(bu_ref[:, 2 * half * m:2 * half * (m + 1)].astype(BF16), wc_ref[m],
                  preferred_element_type=F32) for m in range(n_blocks)]
    y = jnp.concatenate(ys, axis=1) + d_ref[...] * u
    y = jax.nn.gelu(y)
    z = jnp.dot(y.astype(BF16), wglu_ref[...], preferred_element_type=F32) + bglu_ref[...]
    y = y * jax.nn.sigmoid(z)
    for j in range(n_slabs):
        il_ref[j] = y[:, j * LANES:(j + 1) * LANES]
    for bi in range(y_ref.shape[0]):
        for k in range(2):
            for j in range(n_slabs):
                y_ref[bi, k * h:(k + 1) * h, j * LANES:(j + 1) * LANES] = il_ref[
                    j, pl.ds(2 * bi + k, h, stride=s), :]


def _outproj_kernel(x_ref, ohg_ref, y_ref, w_ref, g_ref, h_ref, hn_ref, *, width):
    acc = jnp.dot(ohg_ref[...].astype(BF16), w_ref[0:width, :], preferred_element_type=F32)
    acc = acc + jnp.dot(y_ref[...].astype(BF16), w_ref[width:2 * width, :], preferred_element_type=F32)
    hres = x_ref[...] + acc
    h_ref[...] = hres
    hn_ref[...] = _rms(hres, g_ref[...]).astype(BF16)


def _ffn_kernel(hn_ref, h_ref, wup_ref, cw_ref, cb_ref, wdn_ref, g_ref, o_ref,
                tail_ref, pre_ref, act_ref, *, n_blk):
    @pl.when(pl.program_id(1) == 0)
    def _():
        tail_ref[...] = jnp.zeros_like(tail_ref)

    tm = act_ref.shape[0]
    bf = pre_ref.shape[-1]
    rows = FFN_ROWS
    top = lax.broadcasted_iota(jnp.int32, (SUBLANES, bf), 0)

    def up(j, slot):
        for part in range(2):
            pre_ref[slot, part, rows:rows + tm] = jnp.dot(hn_ref[...], wup_ref[part * n_blk + j],
                                                          preferred_element_type=F32)

    def mid(j, slot):
        taps, prev = [], []
        for part in range(2):
            blk = part * n_blk + j
            pre_ref[slot, part, 0:rows] = tail_ref[blk]
            tail_ref[blk] = pre_ref[slot, part, tm:tm + rows]
            cw = cw_ref[blk]
            taps.append((cb_ref[blk], cw[0:1, :], cw[1:2, :], cw[2:3, :]))
            p = pre_ref[slot, part, 0:rows]
            prev.append((pltpu.roll(p, 1, axis=0), pltpu.roll(p, 2, axis=0)))
        for g in range(tm // rows):
            r0 = rows * (g + 1)
            outs = []
            for part in range(2):
                cur = pre_ref[slot, part, r0:r0 + rows]
                rc = (pltpu.roll(cur, 1, axis=0), pltpu.roll(cur, 2, axis=0))
                sh = [jnp.concatenate([jnp.where(top < d + 1, prev[part][d][0:SUBLANES], rc[d][0:SUBLANES]),
                                       rc[d][SUBLANES:]], axis=0) for d in range(2)]
                prev[part] = rc
                cb, w0, w1, w2 = taps[part]
                outs.append(cb + w0 * sh[1] + w1 * sh[0] + w2 * cur)
            gate, val = outs
            act_ref[rows * g:rows * (g + 1), j * bf:(j + 1) * bf] = (
                (gate * jax.nn.sigmoid(gate)) * val).astype(BF16)

    up(0, 0)
    for j in range(n_blk):
        if j + 1 < n_blk:
            up(j + 1, (j + 1) % 2)
        mid(j, j % 2)
    out = h_ref[...] + jnp.dot(act_ref[...], wdn_ref[...], preferred_element_type=F32)
    o_ref[...] = _rms(out, g_ref[...])


def _tile_sizes(seq_len):
    tm = min(512, seq_len)
    tl = min(256, seq_len)
    assert seq_len % tm == 0 and seq_len % tl == 0 and tm % (2 * S5_SEG) == 0 and tl % HG_CHUNK == 0
    return tm, tl


def kernel(x, in_norm_g, w_in, hg_lb, hg_norm_g, s5_a_re, s5_a_im, s5_log_dt, s5_b_re, s5_b_im,
           s5_c_re, s5_c_im, s5_d, s5_w_glu, s5_b_glu, w_out, ffn_norm_g, w_up, conv_w, conv_b,
           w_down, final_norm_g):
    bsz, seq_len, d_model = x.shape
    depth = w_in.shape[0]
    assert depth == 1, "single-layer block"
    width = hg_norm_g.shape[1]
    assert w_in.shape[2] == 5 * width and s5_d.shape[1] == width
    n_heads = width // HG_HEAD_DIM
    groups, n_state, n_chan = s5_b_re.shape[1:]
    assert (n_state, n_chan) == (S5_STATE, S5_GROUP) and groups * n_chan == width
    assert 2 * bsz == S5_SEQS
    d_ff = w_down.shape[1]
    tm, tl = _tile_sizes(seq_len)
    n_tiles = seq_len // (2 * S5_SEG)
    row = lambda a: a.reshape(1, -1)

    gp = groups * n_chan
    rep = lambda a: jnp.repeat(a, n_chan, axis=0)
    ldt = jnp.broadcast_to(jnp.repeat(s5_log_dt[0], n_chan)[:, None], (gp, n_state))
    b_t = lambda a: a.transpose(0, 2, 1).reshape(gp, n_state)
    lb, abr, abi, bbr, bbi = pl.pallas_call(
        _param_kernel,
        out_shape=[jax.ShapeDtypeStruct((1, width), F32)] + [jax.ShapeDtypeStruct((gp, n_state), F32)] * 4,
        name="params",
    )(hg_lb, rep(s5_a_re[0]), rep(s5_a_im[0]), ldt, b_t(s5_b_re[0]), b_t(s5_b_im[0]))

    gpb = LANES // n_chan
    n_blocks = groups // gpb
    eye = jnp.eye(gpb, dtype=F32)
    blockdiag_in = lambda w: jnp.einsum("mgpn,gh->mgphn", w.reshape(n_blocks, gpb, n_chan, n_state),
                                        eye).reshape(n_blocks, gpb * n_chan, gpb * n_state)
    blockdiag_out = lambda w: jnp.einsum("mgpn,gh->mhngp", w.reshape(n_blocks, gpb, n_chan, n_state),
                                         eye).reshape(n_blocks, gpb * n_state, gpb * n_chan)
    wb = jnp.concatenate([blockdiag_in(bbr), blockdiag_in(bbi)], axis=2).astype(BF16)
    wc = jnp.concatenate([blockdiag_out(s5_c_re[0]), -blockdiag_out(s5_c_im[0])], axis=1).astype(BF16)
    n_lanes = groups * n_state
    a_rows = lambda a: jnp.broadcast_to(a.reshape(groups, n_chan, n_state)[:, 0, :].reshape(1, n_lanes),
                                        (S5_SEQS, n_lanes))
    a_re8, a_im8 = a_rows(abr), a_rows(abi)

    tok = lambda w: pl.BlockSpec((None, tm, w), lambda b, t: (b, t, 0))
    const2 = lambda shape: pl.BlockSpec(shape, lambda b, t: (0,) * len(shape))
    q, fz, v, gz, u = pl.pallas_call(
        functools.partial(_inproj_kernel, width=width),
        grid=(bsz, seq_len // tm),
        in_specs=[tok(d_model), const2((1, d_model)), const2((d_model, 5 * width))],
        out_specs=[tok(width)] * 5,
        out_shape=[jax.ShapeDtypeStruct((bsz, seq_len, width), F32)] * 5,
        compiler_params=_cparams("parallel", "parallel"),
        name="in_proj",
    )(x, row(in_norm_g[0]), w_in[0].astype(BF16))

    ttok = pl.BlockSpec((None, tl, width), lambda b, t: (b, t, 0))
    o_hg = pl.pallas_call(
        functools.partial(_hgrn2_kernel, n_chunks=tl // HG_CHUNK, n_heads=n_heads),
        grid=(bsz, seq_len // tl),
        in_specs=[ttok] * 4 + [const2((1, width))] * 2,
        out_specs=ttok,
        out_shape=jax.ShapeDtypeStruct((bsz, seq_len, width), F32),
        scratch_shapes=[pltpu.VMEM((n_heads, HG_HEAD_DIM, HG_HEAD_DIM), F32)],
        compiler_params=_cparams("arbitrary", "arbitrary"),
        name="hgrn2",
    )(q, fz, v, gz, lb, row(hg_norm_g[0]))

    m_rows = S5_SEG * S5_SEQS
    c1 = lambda shape: pl.BlockSpec(shape, lambda j: (0,) * len(shape))
    tile_spec = pl.BlockSpec((bsz, 2 * S5_SEG, width), lambda j: (0, j, 0))
    y_s5 = pl.pallas_call(
        functools.partial(_s5_kernel, n_blocks=n_blocks),
        grid=(n_tiles,),
        in_specs=[tile_spec, c1(wb.shape), c1(wc.shape), c1(a_re8.shape), c1(a_im8.shape),
                  c1((1, width)), c1((width, width)), c1((1, width))],
        out_specs=tile_spec,
        out_shape=jax.ShapeDtypeStruct((bsz, seq_len, width), F32),
        scratch_shapes=[pltpu.VMEM((width // LANES, m_rows, LANES), F32), pltpu.VMEM((m_rows, 2 * n_lanes), F32),
                        pltpu.VMEM((m_rows, n_lanes), F32), pltpu.VMEM((m_rows, n_lanes), F32),
                        pltpu.VMEM((S5_SEQS, 2 * n_lanes), F32)],
        compiler_params=_cparams("arbitrary"),
        name="s5",
    )(u, wb, wc, a_re8, a_im8, row(s5_d[0]), s5_w_glu[0].astype(BF16), row(s5_b_glu[0]))

    h1, hn = pl.pallas_call(
        functools.partial(_outproj_kernel, width=width),
        grid=(bsz, seq_len // tm),
        in_specs=[tok(d_model), tok(width), tok(width), const2((2 * width, d_model)), const2((1, d_model))],
        out_specs=[tok(d_model), tok(d_model)],
        out_shape=[jax.ShapeDtypeStruct((bsz, seq_len, d_model), F32),
                   jax.ShapeDtypeStruct((bsz, seq_len, d_model), BF16)],
        compiler_params=_cparams("parallel", "parallel"),
        name="out_proj",
    )(x, o_hg, y_s5, w_out[0].astype(BF16), row(ffn_norm_g[0]))

    bf = FFN_BLOCK
    assert d_ff % bf == 0
    n_blk = d_ff // bf
    by_block = lambda a: a.reshape(a.shape[0], 2 * n_blk, bf).transpose(1, 0, 2)
    out = pl.pallas_call(
        functools.partial(_ffn_kernel, n_blk=n_blk),
        grid=(bsz, seq_len // tm),
        in_specs=[tok(d_model), tok(d_model), const2((2 * n_blk, d_model, bf)), const2((2 * n_blk, CONV_WIDTH, bf)),
                  const2((2 * n_blk, 1, bf)), const2((d_ff, d_model)), const2((1, d_model))],
        out_specs=tok(d_model),
        out_shape=jax.ShapeDtypeStruct((bsz, seq_len, d_model), F32),
        scratch_shapes=[pltpu.VMEM((2 * n_blk, FFN_ROWS, bf), F32), pltpu.VMEM((2, 2, tm + FFN_ROWS, bf), F32),
                        pltpu.VMEM((tm, d_ff), BF16)],
        compiler_params=_cparams("arbitrary", "arbitrary"),
        name="ffn",
    )(hn, h1, by_block(w_up[0]).astype(BF16), by_block(conv_w[0]), by_block(row(conv_b[0])),
      w_down[0].astype(BF16), row(final_norm_g))
    return out
```

```python
import functools

import jax
import jax.numpy as jnp
from jax import lax
from jax.experimental import pallas as pl
from jax.experimental.pallas import tpu as pltpu

EPS = 1e-6
HG_CHUNK = 64
HG_HEAD_DIM = 128
S5_GROUP = 16
S5_STATE = 64
S5_SEG = 64
S5_SEQS = 8
S5_SCAN_BLOCKS = 1
S5_SCAN_UNROLL = True
CONV_WIDTH = 3
FFN_BLOCK = 256
FFN_ROWS = 16
SUBLANES = 8
LANES = 128
VMEM_LIMIT = 56 * 1024 * 1024

F32 = jnp.float32
BF16 = jnp.bfloat16

_NT = (((1,), (1,)), ((), ()))
_TN = (((0,), (0,)), ((), ()))


def _rms(x, g):
    ms = jnp.mean(x * x, axis=-1, keepdims=True)
    return x * lax.rsqrt(ms + EPS) * g


def _cparams(*sem):
    return pltpu.CompilerParams(dimension_semantics=sem, vmem_limit_bytes=VMEM_LIMIT)


def _param_kernel(lb_ref, are_ref, aim_ref, ldt_ref, bre_ref, bim_ref,
                  lb_out, abr_out, abi_out, bbr_out, bbi_out):
    z = lb_ref[...]
    e = jnp.exp(z - jnp.max(z, axis=0, keepdims=True))
    sm = e / jnp.sum(e, axis=0, keepdims=True)
    lb_out[...] = sm[0:1, :]
    dt = jnp.exp(ldt_ref[...])
    ar = are_ref[...]
    ai = aim_ref[...]
    mag = jnp.exp(dt * ar)
    abr = mag * jnp.cos(dt * ai)
    abi = mag * jnp.sin(dt * ai)
    nr = abr - 1.0
    den = ar * ar + ai * ai
    zr = (nr * ar + abi * ai) / den
    zi = (abi * ar - nr * ai) / den
    br = bre_ref[...]
    bi = bim_ref[...]
    abr_out[...] = abr
    abi_out[...] = abi
    bbr_out[...] = zr * br - zi * bi
    bbi_out[...] = zr * bi + zi * br


def _inproj_kernel(x_ref, g_ref, w_ref, q_ref, f_ref, v_ref, gz_ref, u_ref, *, width):
    xb = _rms(x_ref[...], g_ref[...]).astype(BF16)
    for i, o in enumerate((q_ref, f_ref, v_ref, gz_ref, u_ref)):
        o[...] = jnp.dot(xb, w_ref[:, i * width:(i + 1) * width], preferred_element_type=F32)


def _hgrn2_kernel(q_ref, f_ref, v_ref, g_ref, lb_ref, ng_ref, o_ref, st_ref, *, n_chunks, n_heads):
    @pl.when(pl.program_id(1) == 0)
    def _():
        st_ref[...] = jnp.zeros_like(st_ref)

    lb = lb_ref[...]
    ng = ng_ref[...]
    c = HG_CHUNK
    tri = lax.broadcasted_iota(jnp.int32, (c, c), 0) >= lax.broadcasted_iota(jnp.int32, (c, c), 1)
    tri_b = tri.astype(BF16)

    f = lb + (1.0 - lb) * jax.nn.sigmoid(f_ref[...])
    k = 1.0 - f
    logf = jnp.log(f)
    hi = logf.astype(BF16)
    r1 = logf - hi.astype(F32)
    mid = r1.astype(BF16)
    lo = (r1 - mid.astype(F32)).astype(BF16)
    bs = []
    for ci in range(n_chunks):
        rows = slice(ci * c, (ci + 1) * c)
        bs.append(jnp.dot(tri_b, hi[rows], preferred_element_type=F32)
                  + (jnp.dot(tri_b, mid[rows], preferred_element_type=F32)
                     + jnp.dot(tri_b, lo[rows], preferred_element_type=F32)))
    b = jnp.concatenate(bs, axis=0)
    kd_f = k * jnp.exp(-b)
    qd = (q_ref[...] * jnp.exp(b)).astype(BF16)
    kd = kd_f.astype(BF16)
    vb = v_ref[...].astype(BF16)
    gz = g_ref[...]
    gate = ng * (gz * jax.nn.sigmoid(gz))
    decs = [jnp.exp(b[(ci + 1) * c - 1:(ci + 1) * c, :]) for ci in range(n_chunks)]
    kts = [(kd_f[ci * c:(ci + 1) * c] * decs[ci]).astype(BF16) for ci in range(n_chunks)]

    for hd in range(n_heads):
        cs = slice(hd * HG_HEAD_DIM, (hd + 1) * HG_HEAD_DIM)
        st = st_ref[hd]
        for ci in range(n_chunks):
            rows = slice(ci * c, (ci + 1) * c)
            att = lax.dot_general(qd[rows, cs], kd[rows, cs], _NT, preferred_element_type=F32)
            att = jnp.where(tri, att, 0.0)
            o = jnp.dot(att.astype(BF16), vb[rows, cs], preferred_element_type=F32)
            o = o + lax.dot_general(qd[rows, cs], st.astype(BF16), _NT, preferred_element_type=F32)
            ds = lax.dot_general(vb[rows, cs], kts[ci][:, cs], _TN, preferred_element_type=F32)
            st = st * decs[ci][:, cs] + ds
            o = o * lax.rsqrt(jnp.mean(o * o, axis=-1, keepdims=True) + EPS)
            o_ref[rows, cs] = o * gate[rows, cs]
        st_ref[hd] = st


def _s5_kernel(u_ref, wb_ref, wc_ref, are_ref, aim_ref, d_ref, wglu_ref, bglu_ref, y_ref,
               il_ref, bu_ref, pre_ref, pim_ref, carry_ref, *, n_blocks):
    h = S5_SEG
    s = S5_SEQS
    half = wb_ref.shape[2] // 2
    kw = wb_ref.shape[1]

    @pl.when(pl.program_id(0) == 0)
    def _():
        carry_ref[...] = jnp.zeros_like(carry_ref)
        ar = are_ref[...]
        ai = aim_ref[...]
        pre_ref[0:s, :] = ar
        pim_ref[0:s, :] = ai

        def powers(i, c):
            pr, pi = c
            npr = pr * ar - pi * ai
            npi = pr * ai + pi * ar
            r = pl.ds(pl.multiple_of(i * s, s), s)
            pre_ref[r, :] = npr
            pim_ref[r, :] = npi
            return npr, npi

        lax.fori_loop(1, h, powers, (ar, ai))

    n_slabs = il_ref.shape[0]
    for bi in range(u_ref.shape[0]):
        for k in range(2):
            for j in range(n_slabs):
                il_ref[j, pl.ds(2 * bi + k, h, stride=s), :] = u_ref[bi, k * h:(k + 1) * h,
                                                                     j * LANES:(j + 1) * LANES]
    u = jnp.concatenate([il_ref[j] for j in range(n_slabs)], axis=1)
    ub = u.astype(BF16)
    for m in range(n_blocks):
        bu_ref[:, 2 * half * m:2 * half * (m + 1)] = jnp.dot(
            ub[:, kw * m:kw * (m + 1)], wb_ref[m], preferred_element_type=F32)

    second = lax.broadcasted_iota(jnp.int32, (s, half), 0) % 2 == 1
    last = slice((h - 1) * s, h * s)

    for m0 in range(0, n_blocks, S5_SCAN_BLOCKS):
        blocks = []
        for m in range(m0, m0 + S5_SCAN_BLOCKS):
            lo = 2 * half * m
            blocks.append((slice(lo, lo + half), slice(lo + half, lo + 2 * half),
                           slice(half * m, half * (m + 1))))
        coef = [(are_ref[:, am], aim_ref[:, am]) for _, _, am in blocks]

        def scan(i, c, blocks=blocks, coef=coef):
            r = pl.ds(pl.multiple_of(i * s, s), s)
            out = []
            for (re, im, _), (ar, ai), (xr, xi) in zip(blocks, coef, c):
                nxr = ar * xr - ai * xi + bu_ref[r, re]
                nxi = ar * xi + ai * xr + bu_ref[r, im]
                bu_ref[r, re] = nxr
                bu_ref[r, im] = nxi
                out.append((nxr, nxi))
            return tuple(out)

        init = tuple((carry_ref[:, re], carry_ref[:, im]) for re, im, _ in blocks)
        ends = lax.fori_loop(0, h, scan, init, unroll=S5_SCAN_UNROLL)
        starts = [(jnp.where(second, pltpu.roll(er, 1, axis=0), 0.0),
                   jnp.where(second, pltpu.roll(ei, 1, axis=0), 0.0)) for er, ei in ends]

        def fix(i, c, blocks=blocks, starts=starts):
            r = pl.ds(pl.multiple_of(i * s, s), s)
            for (re, im, am), (cr, ci) in zip(blocks, starts):
                pr = pre_ref[r, am]
                pi = pim_ref[r, am]
                bu_ref[r, re] = bu_ref[r, re] + (pr * cr - pi * ci)
                bu_ref[r, im] = bu_ref[r, im] + (pr * ci + pi * cr)
            return c

        lax.fori_loop(0, h, fix, 0, unroll=S5_SCAN_UNROLL)
        for re, im, _ in blocks:
            carry_ref[:, re] = jnp.where(second, 0.0, pltpu.roll(bu_ref[last, re], s - 1, axis=0))
            carry_ref[:, im] = jnp.where(second, 0.0, pltpu.roll(bu_ref[last, im], s - 1, axis=0))

    ys = [jnp.dot(bu_ref[:, 2 * half * m:2 * half * (m + 1)].astype(BF16), wc_ref[m],
                  preferred_element_type=F32) for m in range(n_blocks)]
    y = jnp.concatenate(ys, axis=1) + d_ref[...] * u
    y = jax.nn.gelu(y)
    z = jnp.dot(y.astype(BF16), wglu_ref[...], preferred_element_type=F32) + bglu_ref[...]
    y = y * jax.nn.sigmoid(z)
    for j in range(n_slabs):
        il_ref[j] = y[:, j * LANES:(j + 1) * LANES]
    for bi in range(y_ref.shape[0]):
        for k in range(2):
            for j in range(n_slabs):
                y_ref[bi, k * h:(k + 1) * h, j * LANES:(j + 1) * LANES] = il_ref[
                    j, pl.ds(2 * bi + k, h, stride=s), :]


def _ffn_kernel(x_ref, ohg_ref, y_ref, wout_ref, gf_ref, wup_ref, cw_ref, cb_ref, wdn_ref, g_ref, o_ref,
                tail_ref, h_ref, hn_ref, pre_ref, act_ref, *, n_blk):
    @pl.when(pl.program_id(1) == 0)
    def _():
        tail_ref[...] = jnp.zeros_like(tail_ref)

    width = ohg_ref.shape[-1]
    mix = jnp.dot(ohg_ref[...].astype(BF16), wout_ref[0:width, :], preferred_element_type=F32)
    mix = mix + jnp.dot(y_ref[...].astype(BF16), wout_ref[width:2 * width, :], preferred_element_type=F32)
    hres = x_ref[...] + mix
    h_ref[...] = hres
    hn_ref[...] = _rms(hres, gf_ref[...]).astype(BF16)

    tm = act_ref.shape[0]
    bf = pre_ref.shape[-1]
    rows = FFN_ROWS
    top = lax.broadcasted_iota(jnp.int32, (SUBLANES, bf), 0)

    def up(j, slot):
        for part in range(2):
            pre_ref[slot, part, rows:rows + tm] = jnp.dot(hn_ref[...], wup_ref[part * n_blk + j],
                                                          preferred_element_type=F32)

    def mid(j, slot):
        taps, prev = [], []
        for part in range(2):
            blk = part * n_blk + j
            pre_ref[slot, part, 0:rows] = tail_ref[blk]
            tail_ref[blk] = pre_ref[slot, part, tm:tm + rows]
            cw = cw_ref[blk]
            taps.append((cb_ref[blk], cw[0:1, :], cw[1:2, :], cw[2:3, :]))
            p = pre_ref[slot, part, 0:rows]
            prev.append((pltpu.roll(p, 1, axis=0), pltpu.roll(p, 2, axis=0)))
        for g in range(tm // rows):
            r0 = rows * (g + 1)
            outs = []
            for part in range(2):
                cur = pre_ref[slot, part, r0:r0 + rows]
                rc = (pltpu.roll(cur, 1, axis=0), pltpu.roll(cur, 2, axis=0))
                sh = [jnp.concatenate([jnp.where(top < d + 1, prev[part][d][0:SUBLANES], rc[d][0:SUBLANES]),
                                       rc[d][SUBLANES:]], axis=0) for d in range(2)]
                prev[part] = rc
                cb, w0, w1, w2 = taps[part]
                outs.append(cb + w0 * sh[1] + w1 * sh[0] + w2 * cur)
            gate, val = outs
            act_ref[rows * g:rows * (g + 1), j * bf:(j + 1) * bf] = (
                (gate * jax.nn.sigmoid(gate)) * val).astype(BF16)

    up(0, 0)
    for j in range(n_blk):
        if j + 1 < n_blk:
            up(j + 1, (j + 1) % 2)
        mid(j, j % 2)
    out = h_ref[...] + jnp.dot(act_ref[...], wdn_ref[...], preferred_element_type=F32)
    o_ref[...] = _rms(out, g_ref[...])


def _tile_sizes(seq_len):
    tm = min(512, seq_len)
    tl = min(256, seq_len)
    assert seq_len % tm == 0 and seq_len % tl == 0 and tm % (2 * S5_SEG) == 0 and tl % HG_CHUNK == 0
    return tm, tl


def kernel(x, in_norm_g, w_in, hg_lb, hg_norm_g, s5_a_re, s5_a_im, s5_log_dt, s5_b_re, s5_b_im,
           s5_c_re, s5_c_im, s5_d, s5_w_glu, s5_b_glu, w_out, ffn_norm_g, w_up, conv_w, conv_b,
           w_down, final_norm_g):
    bsz, seq_len, d_model = x.shape
    depth = w_in.shape[0]
    assert depth == 1, "single-layer block"
    width = hg_norm_g.shape[1]
    assert w_in.shape[2] == 5 * width and s5_d.shape[1] == width
    n_heads = width // HG_HEAD_DIM
    groups, n_state, n_chan = s5_b_re.shape[1:]
    assert (n_state, n_chan) == (S5_STATE, S5_GROUP) and groups * n_chan == width
    assert 2 * bsz == S5_SEQS
    d_ff = w_down.shape[1]
    tm, tl = _tile_sizes(seq_len)
    n_tiles = seq_len // (2 * S5_SEG)
    row = lambda a: a.reshape(1, -1)

    gp = groups * n_chan
    rep = lambda a: jnp.repeat(a, n_chan, axis=0)
    ldt = jnp.broadcast_to(jnp.repeat(s5_log_dt[0], n_chan)[:, None], (gp, n_state))
    b_t = lambda a: a.transpose(0, 2, 1).reshape(gp, n_state)
    lb, abr, abi, bbr, bbi = pl.pallas_call(
        _param_kernel,
        out_shape=[jax.ShapeDtypeStruct((1, width), F32)] + [jax.ShapeDtypeStruct((gp, n_state), F32)] * 4,
        name="params",
    )(hg_lb, rep(s5_a_re[0]), rep(s5_a_im[0]), ldt, b_t(s5_b_re[0]), b_t(s5_b_im[0]))

    gpb = LANES // n_chan
    n_blocks = groups // gpb
    eye = jnp.eye(gpb, dtype=F32)
    blockdiag_in = lambda w: jnp.einsum("mgpn,gh->mgphn", w.reshape(n_blocks, gpb, n_chan, n_state),
                                        eye).reshape(n_blocks, gpb * n_chan, gpb * n_state)
    blockdiag_out = lambda w: jnp.einsum("mgpn,gh->mhngp", w.reshape(n_blocks, gpb, n_chan, n_state),
                                         eye).reshape(n_blocks, gpb * n_state, gpb * n_chan)
    wb = jnp.concatenate([blockdiag_in(bbr), blockdiag_in(bbi)], axis=2).astype(BF16)
    wc = jnp.concatenate([blockdiag_out(s5_c_re[0]), -blockdiag_out(s5_c_im[0])], axis=1).astype(BF16)
    n_lanes = groups * n_state
    a_rows = lambda a: jnp.broadcast_to(a.reshape(groups, n_chan, n_state)[:, 0, :].reshape(1, n_lanes),
                                        (S5_SEQS, n_lanes))
    a_re8, a_im8 = a_rows(abr), a_rows(abi)

    tok = lambda w: pl.BlockSpec((None, tm, w), lambda b, t: (b, t, 0))
    const2 = lambda shape: pl.BlockSpec(shape, lambda b, t: (0,) * len(shape))
    resident = lambda shape: pl.BlockSpec(shape, lambda b, t: (0,) * len(shape), pipeline_mode=pl.Buffered(1))
    q, fz, v, gz, u = pl.pallas_call(
        functools.partial(_inproj_kernel, width=width),
        grid=(bsz, seq_len // tm),
        in_specs=[tok(d_model), const2((1, d_model)), const2((d_model, 5 * width))],
        out_specs=[tok(width)] * 5,
        out_shape=[jax.ShapeDtypeStruct((bsz, seq_len, width), F32)] * 5,
        compiler_params=_cparams("parallel", "parallel"),
        name="in_proj",
    )(x, row(in_norm_g[0]), w_in[0].astype(BF16))

    ttok = pl.BlockSpec((None, tl, width), lambda b, t: (b, t, 0))
    o_hg = pl.pallas_call(
        functools.partial(_hgrn2_kernel, n_chunks=tl // HG_CHUNK, n_heads=n_heads),
        grid=(bsz, seq_len // tl),
        in_specs=[ttok] * 4 + [const2((1, width))] * 2,
        out_specs=ttok,
        out_shape=jax.ShapeDtypeStruct((bsz, seq_len, width), F32),
        scratch_shapes=[pltpu.VMEM((n_heads, HG_HEAD_DIM, HG_HEAD_DIM), F32)],
        compiler_params=_cparams("arbitrary", "arbitrary"),
        name="hgrn2",
    )(q, fz, v, gz, lb, row(hg_norm_g[0]))

    m_rows = S5_SEG * S5_SEQS
    c1 = lambda shape: pl.BlockSpec(shape, lambda j: (0,) * len(shape))
    tile_spec = pl.BlockSpec((bsz, 2 * S5_SEG, width), lambda j: (0, j, 0))
    y_s5 = pl.pallas_call(
        functools.partial(_s5_kernel, n_blocks=n_blocks),
        grid=(n_tiles,),
        in_specs=[tile_spec, c1(wb.shape), c1(wc.shape), c1(a_re8.shape), c1(a_im8.shape),
                  c1((1, width)), c1((width, width)), c1((1, width))],
        out_specs=tile_spec,
        out_shape=jax.ShapeDtypeStruct((bsz, seq_len, width), F32),
        scratch_shapes=[pltpu.VMEM((width // LANES, m_rows, LANES), F32), pltpu.VMEM((m_rows, 2 * n_lanes), F32),
                        pltpu.VMEM((m_rows, n_lanes), F32), pltpu.VMEM((m_rows, n_lanes), F32),
                        pltpu.VMEM((S5_SEQS, 2 * n_lanes), F32)],
        compiler_params=_cparams("arbitrary"),
        name="s5",
    )(u, wb, wc, a_re8, a_im8, row(s5_d[0]), s5_w_glu[0].astype(BF16), row(s5_b_glu[0]))

    bf = FFN_BLOCK
    assert d_ff % bf == 0
    n_blk = d_ff // bf
    by_block = lambda a: a.reshape(a.shape[0], 2 * n_blk, bf).transpose(1, 0, 2)
    out = pl.pallas_call(
        functools.partial(_ffn_kernel, n_blk=n_blk),
        grid=(bsz, seq_len // tm),
        in_specs=[tok(d_model), tok(width), tok(width), resident((2 * width, d_model)), const2((1, d_model)),
                  resident((2 * n_blk, d_model, bf)), const2((2 * n_blk, CONV_WIDTH, bf)),
                  const2((2 * n_blk, 1, bf)), resident((d_ff, d_model)), const2((1, d_model))],
        out_specs=tok(d_model),
        out_shape=jax.ShapeDtypeStruct((bsz, seq_len, d_model), F32),
        scratch_shapes=[pltpu.VMEM((2 * n_blk, FFN_ROWS, bf), F32), pltpu.VMEM((tm, d_model), F32),
                        pltpu.VMEM((tm, d_model), BF16), pltpu.VMEM((2, 2, tm + FFN_ROWS, bf), F32),
                        pltpu.VMEM((tm, d_ff), BF16)],
        compiler_params=_cparams("arbitrary", "arbitrary"),
        name="ffn",
    )(x, o_hg, y_s5, w_out[0].astype(BF16), row(ffn_norm_g[0]),
      by_block(w_up[0]).astype(BF16), by_block(conv_w[0]), by_block(row(conv_b[0])),
      w_down[0].astype(BF16), row(final_norm_g))
    return out
```

```python
import functools

import jax
import jax.numpy as jnp
from jax import lax
from jax.experimental import pallas as pl
from jax.experimental.pallas import tpu as pltpu

EPS = 1e-6
HG_CHUNK = 64
HG_HEAD_DIM = 128
S5_GROUP = 16
S5_STATE = 64
S5_SEG = 64
S5_SEQS = 8
S5_SCAN_BLOCKS = 2
S5_SCAN_UNROLL = True
CONV_WIDTH = 3
FFN_BLOCK = 256
FFN_ROWS = 16
SUBLANES = 8
LANES = 128
VMEM_LIMIT = 56 * 1024 * 1024

F32 = jnp.float32
BF16 = jnp.bfloat16

_NT = (((1,), (1,)), ((), ()))
_TN = (((0,), (0,)), ((), ()))


def _rms(x, g):
    ms = jnp.mean(x * x, axis=-1, keepdims=True)
    return x * lax.rsqrt(ms + EPS) * g


def _cparams(*sem):
    return pltpu.CompilerParams(dimension_semantics=sem, vmem_limit_bytes=VMEM_LIMIT)


def _param_kernel(lb_ref, are_ref, aim_ref, ldt_ref, bre_ref, bim_ref,
                  lb_out, abr_out, abi_out, bbr_out, bbi_out):
    z = lb_ref[...]
    e = jnp.exp(z - jnp.max(z, axis=0, keepdims=True))
    sm = e / jnp.sum(e, axis=0, keepdims=True)
    lb_out[...] = sm[0:1, :]
    dt = jnp.exp(ldt_ref[...])
    ar = are_ref[...]
    ai = aim_ref[...]
    mag = jnp.exp(dt * ar)
    abr = mag * jnp.cos(dt * ai)
    abi = mag * jnp.sin(dt * ai)
    nr = abr - 1.0
    den = ar * ar + ai * ai
    zr = (nr * ar + abi * ai) / den
    zi = (abi * ar - nr * ai) / den
    br = bre_ref[...]
    bi = bim_ref[...]
    abr_out[...] = abr
    abi_out[...] = abi
    bbr_out[...] = zr * br - zi * bi
    bbi_out[...] = zr * bi + zi * br


def _hgrn2_kernel(q_ref, f_ref, v_ref, g_ref, lb_ref, ng_ref, o_ref, st_ref, *, n_chunks, n_heads):
    @pl.when(pl.program_id(1) == 0)
    def _():
        st_ref[...] = jnp.zeros_like(st_ref)

    lb = lb_ref[...]
    ng = ng_ref[...]
    c = HG_CHUNK
    tri = lax.broadcasted_iota(jnp.int32, (c, c), 0) >= lax.broadcasted_iota(jnp.int32, (c, c), 1)
    tri_b = tri.astype(BF16)

    f = lb + (1.0 - lb) * jax.nn.sigmoid(f_ref[...])
    k = 1.0 - f
    logf = jnp.log(f)
    hi = logf.astype(BF16)
    r1 = logf - hi.astype(F32)
    mid = r1.astype(BF16)
    lo = (r1 - mid.astype(F32)).astype(BF16)
    bs = []
    for ci in range(n_chunks):
        rows = slice(ci * c, (ci + 1) * c)
        bs.append(jnp.dot(tri_b, hi[rows], preferred_element_type=F32)
                  + (jnp.dot(tri_b, mid[rows], preferred_element_type=F32)
                     + jnp.dot(tri_b, lo[rows], preferred_element_type=F32)))
    b = jnp.concatenate(bs, axis=0)
    kd_f = k * jnp.exp(-b)
    qd = (q_ref[...] * jnp.exp(b)).astype(BF16)
    kd = kd_f.astype(BF16)
    vb = v_ref[...].astype(BF16)
    gz = g_ref[...]
    gate = ng * (gz * jax.nn.sigmoid(gz))
    decs = [jnp.exp(b[(ci + 1) * c - 1:(ci + 1) * c, :]) for ci in range(n_chunks)]
    kts = [(kd_f[ci * c:(ci + 1) * c] * decs[ci]).astype(BF16) for ci in range(n_chunks)]

    for hd in range(n_heads):
        cs = slice(hd * HG_HEAD_DIM, (hd + 1) * HG_HEAD_DIM)
        st = st_ref[hd]
        for ci in range(n_chunks):
            rows = slice(ci * c, (ci + 1) * c)
            att = lax.dot_general(qd[rows, cs], kd[rows, cs], _NT, preferred_element_type=F32)
            att = jnp.where(tri, att, 0.0)
            o = jnp.dot(att.astype(BF16), vb[rows, cs], preferred_element_type=F32)
            o = o + lax.dot_general(qd[rows, cs], st.astype(BF16), _NT, preferred_element_type=F32)
            ds = lax.dot_general(vb[rows, cs], kts[ci][:, cs], _TN, preferred_element_type=F32)
            st = st * decs[ci][:, cs] + ds
            o = o * lax.rsqrt(jnp.mean(o * o, axis=-1, keepdims=True) + EPS)
            o_ref[rows, cs] = o * gate[rows, cs]
        st_ref[hd] = st


def _inproj_s5_kernel(x_ref, gin_ref, win_ref, wb_ref, wc_ref, are_ref, aim_ref, d_ref, wglu_ref, bglu_ref,
                      q_ref, f_ref, v_ref, gz_ref, y_ref,
                      il_ref, bu_ref, pre_ref, pim_ref, carry_ref, *, n_blocks):
    h = S5_SEG
    s = S5_SEQS
    half = wb_ref.shape[2] // 2
    kw = wb_ref.shape[1]
    bsz, tt, d_model = x_ref.shape
    width = y_ref.shape[-1]

    @pl.when(pl.program_id(0) == 0)
    def _():
        carry_ref[...] = jnp.zeros_like(carry_ref)
        ar = are_ref[...]
        ai = aim_ref[...]
        pre_ref[0:s, :] = ar
        pim_ref[0:s, :] = ai

        def powers(i, c):
            pr, pi = c
            npr = pr * ar - pi * ai
            npi = pr * ai + pi * ar
            r = pl.ds(pl.multiple_of(i * s, s), s)
            pre_ref[r, :] = npr
            pim_ref[r, :] = npi
            return npr, npi

        lax.fori_loop(1, h, powers, (ar, ai))

    xb = _rms(x_ref[...].reshape(bsz * tt, d_model), gin_ref[...]).astype(BF16)
    u_nat = jnp.dot(xb, win_ref[:, 4 * width:5 * width], preferred_element_type=F32)

    n_slabs = il_ref.shape[0]
    for bi in range(bsz):
        for k in range(2):
            r0 = bi * tt + k * h
            for j in range(n_slabs):
                il_ref[j, pl.ds(2 * bi + k, h, stride=s), :] = u_nat[r0:r0 + h, j * LANES:(j + 1) * LANES]
    u = jnp.concatenate([il_ref[j] for j in range(n_slabs)], axis=1)
    ub = u.astype(BF16)
    for m in range(n_blocks):
        bu_ref[:, 2 * half * m:2 * half * (m + 1)] = jnp.dot(
            ub[:, kw * m:kw * (m + 1)], wb_ref[m], preferred_element_type=F32)

    for i, o in enumerate((q_ref, f_ref, v_ref, gz_ref)):
        o[...] = jnp.dot(xb, win_ref[:, i * width:(i + 1) * width],
                         preferred_element_type=F32).reshape(bsz, tt, width).astype(o.dtype)

    second = lax.broadcasted_iota(jnp.int32, (s, half), 0) % 2 == 1
    last = slice((h - 1) * s, h * s)

    for m0 in range(0, n_blocks, S5_SCAN_BLOCKS):
        blocks = []
        for m in range(m0, m0 + S5_SCAN_BLOCKS):
            lo = 2 * half * m
            blocks.append((slice(lo, lo + half), slice(lo + half, lo + 2 * half),
                           slice(half * m, half * (m + 1))))
        coef = [(are_ref[:, am], aim_ref[:, am]) for _, _, am in blocks]

        def scan(i, c, blocks=blocks, coef=coef):
            r = pl.ds(pl.multiple_of(i * s, s), s)
            out = []
            for (re, im, _), (ar, ai), (xr, xi) in zip(blocks, coef, c):
                nxr = ar * xr - ai * xi + bu_ref[r, re]
                nxi = ar * xi + ai * xr + bu_ref[r, im]
                bu_ref[r, re] = nxr
                bu_ref[r, im] = nxi
                out.append((nxr, nxi))
            return tuple(out)

        init = tuple((carry_ref[:, re], carry_ref[:, im]) for re, im, _ in blocks)
        ends = lax.fori_loop(0, h, scan, init, unroll=S5_SCAN_UNROLL)
        starts = [(jnp.where(second, pltpu.roll(er, 1, axis=0), 0.0),
                   jnp.where(second, pltpu.roll(ei, 1, axis=0), 0.0)) for er, ei in ends]

        def fix(i, c, blocks=blocks, starts=starts):
            r = pl.ds(pl.multiple_of(i * s, s), s)
            for (re, im, am), (cr, ci) in zip(blocks, starts):
                pr = pre_ref[r, am]
                pi = pim_ref[r, am]
                bu_ref[r, re] = bu_ref[r, re] + (pr * cr - pi * ci)
                bu_ref[r, im] = bu_ref[r, im] + (pr * ci + pi * cr)
            return c

        lax.fori_loop(0, h, fix, 0, unroll=S5_SCAN_UNROLL)
        for re, im, _ in blocks:
            carry_ref[:, re] = jnp.where(second, 0.0, pltpu.roll(bu_ref[last, re], s - 1, axis=0))
            carry_ref[:, im] = jnp.where(second, 0.0, pltpu.roll(bu_ref[last, im], s - 1, axis=0))

    ys = [jnp.dot(bu_ref[:, 2 * half * m:2 * half * (m + 1)].astype(BF16), wc_ref[m],
                  preferred_element_type=F32) for m in range(n_blocks)]
    y = jnp.concatenate(ys, axis=1) + d_ref[...] * u
    y = jax.nn.gelu(y)
    z = jnp.dot(y.astype(BF16), wglu_ref[...], preferred_element_type=F32) + bglu_ref[...]
    y = y * jax.nn.sigmoid(z)
    for j in range(n_slabs):
        il_ref[j] = y[:, j * LANES:(j + 1) * LANES]
    for bi in range(y_ref.shape[0]):
        for k in range(2):
            for j in range(n_slabs):
                y_ref[bi, k * h:(k + 1) * h, j * LANES:(j + 1) * LANES] = il_ref[
                    j, pl.ds(2 * bi + k, h, stride=s), :]


def _ffn_kernel(x_ref, ohg_ref, y_ref, wout_ref, gf_ref, wup_ref, cw_ref, cb_ref, wdn_ref, g_ref, o_ref,
                tail_ref, h_ref, hn_ref, pre_ref, act_ref, *, n_blk):
    @pl.when(pl.program_id(1) == 0)
    def _():
        tail_ref[...] = jnp.zeros_like(tail_ref)

    width = ohg_ref.shape[-1]
    mix = jnp.dot(ohg_ref[...].astype(BF16), wout_ref[0:width, :], preferred_element_type=F32)
    mix = mix + jnp.dot(y_ref[...].astype(BF16), wout_ref[width:2 * width, :], preferred_element_type=F32)
    hres = x_ref[...] + mix
    h_ref[...] = hres
    hn_ref[...] = _rms(hres, gf_ref[...]).astype(BF16)

    tm = act_ref.shape[0]
    bf = pre_ref.shape[-1]
    rows = FFN_ROWS
    top = lax.broadcasted_iota(jnp.int32, (SUBLANES, bf), 0)

    def up(j, slot):
        for part in range(2):
            pre_ref[slot, part, rows:rows + tm] = jnp.dot(hn_ref[...], wup_ref[part * n_blk + j],
                                                          preferred_element_type=F32)

    def mid(j, slot):
        taps, prev = [], []
        for part in range(2):
            blk = part * n_blk + j
            pre_ref[slot, part, 0:rows] = tail_ref[blk]
            tail_ref[blk] = pre_ref[slot, part, tm:tm + rows]
            cw = cw_ref[blk]
            taps.append((cb_ref[blk], cw[0:1, :], cw[1:2, :], cw[2:3, :]))
            p = pre_ref[slot, part, 0:rows]
            prev.append((pltpu.roll(p, 1, axis=0), pltpu.roll(p, 2, axis=0)))
        for g in range(tm // rows):
            r0 = rows * (g + 1)
            outs = []
            for part in range(2):
                cur = pre_ref[slot, part, r0:r0 + rows]
                rc = (pltpu.roll(cur, 1, axis=0), pltpu.roll(cur, 2, axis=0))
                sh = [jnp.concatenate([jnp.where(top < d + 1, prev[part][d][0:SUBLANES], rc[d][0:SUBLANES]),
                                       rc[d][SUBLANES:]], axis=0) for d in range(2)]
                prev[part] = rc
                cb, w0, w1, w2 = taps[part]
                outs.append(cb + w0 * sh[1] + w1 * sh[0] + w2 * cur)
            gate, val = outs
            act_ref[rows * g:rows * (g + 1), j * bf:(j + 1) * bf] = (
                (gate * jax.nn.sigmoid(gate)) * val).astype(BF16)

    up(0, 0)
    for j in range(n_blk):
        if j + 1 < n_blk:
            up(j + 1, (j + 1) % 2)
        mid(j, j % 2)
    out = h_ref[...] + jnp.dot(act_ref[...], wdn_ref[...], preferred_element_type=F32)
    o_ref[...] = _rms(out, g_ref[...])


def _tile_sizes(seq_len):
    tm = min(512, seq_len)
    tl = min(256, seq_len)
    assert seq_len % tm == 0 and seq_len % tl == 0 and tm % (2 * S5_SEG) == 0 and tl % HG_CHUNK == 0
    return tm, tl


def kernel(x, in_norm_g, w_in, hg_lb, hg_norm_g, s5_a_re, s5_a_im, s5_log_dt, s5_b_re, s5_b_im,
           s5_c_re, s5_c_im, s5_d, s5_w_glu, s5_b_glu, w_out, ffn_norm_g, w_up, conv_w, conv_b,
           w_down, final_norm_g):
    bsz, seq_len, d_model = x.shape
    depth = w_in.shape[0]
    assert depth == 1, "single-layer block"
    width = hg_norm_g.shape[1]
    assert w_in.shape[2] == 5 * width and s5_d.shape[1] == width
    n_heads = width // HG_HEAD_DIM
    groups, n_state, n_chan = s5_b_re.shape[1:]
    assert (n_state, n_chan) == (S5_STATE, S5_GROUP) and groups * n_chan == width
    assert 2 * bsz == S5_SEQS
    d_ff = w_down.shape[1]
    tm, tl = _tile_sizes(seq_len)
    n_tiles = seq_len // (2 * S5_SEG)
    row = lambda a: a.reshape(1, -1)

    gp = groups * n_chan
    rep = lambda a: jnp.repeat(a, n_chan, axis=0)
    ldt = jnp.broadcast_to(jnp.repeat(s5_log_dt[0], n_chan)[:, None], (gp, n_state))
    b_t = lambda a: a.transpose(0, 2, 1).reshape(gp, n_state)
    lb, abr, abi, bbr, bbi = pl.pallas_call(
        _param_kernel,
        out_shape=[jax.ShapeDtypeStruct((1, width), F32)] + [jax.ShapeDtypeStruct((gp, n_state), F32)] * 4,
        name="params",
    )(hg_lb, rep(s5_a_re[0]), rep(s5_a_im[0]), ldt, b_t(s5_b_re[0]), b_t(s5_b_im[0]))

    gpb = LANES // n_chan
    n_blocks = groups // gpb
    eye = jnp.eye(gpb, dtype=F32)
    blockdiag_in = lambda w: jnp.einsum("mgpn,gh->mgphn", w.reshape(n_blocks, gpb, n_chan, n_state),
                                        eye).reshape(n_blocks, gpb * n_chan, gpb * n_state)
    blockdiag_out = lambda w: jnp.einsum("mgpn,gh->mhngp", w.reshape(n_blocks, gpb, n_chan, n_state),
                                         eye).reshape(n_blocks, gpb * n_state, gpb * n_chan)
    wb = jnp.concatenate([blockdiag_in(bbr), blockdiag_in(bbi)], axis=2).astype(BF16)
    wc = jnp.concatenate([blockdiag_out(s5_c_re[0]), -blockdiag_out(s5_c_im[0])], axis=1).astype(BF16)
    n_lanes = groups * n_state
    a_rows = lambda a: jnp.broadcast_to(a.reshape(groups, n_chan, n_state)[:, 0, :].reshape(1, n_lanes),
                                        (S5_SEQS, n_lanes))
    a_re8, a_im8 = a_rows(abr), a_rows(abi)

    tok = lambda w: pl.BlockSpec((None, tm, w), lambda b, t: (b, t, 0))
    const2 = lambda shape: pl.BlockSpec(shape, lambda b, t: (0,) * len(shape))
    resident = lambda shape: pl.BlockSpec(shape, lambda b, t: (0,) * len(shape), pipeline_mode=pl.Buffered(1))

    m_rows = S5_SEG * S5_SEQS
    c1 = lambda shape: pl.BlockSpec(shape, lambda j: (0,) * len(shape))
    tile_spec = lambda w: pl.BlockSpec((bsz, 2 * S5_SEG, w), lambda j: (0, j, 0))
    act = lambda dt: jax.ShapeDtypeStruct((bsz, seq_len, width), dt)
    q, fz, v, gz, y_s5 = pl.pallas_call(
        functools.partial(_inproj_s5_kernel, n_blocks=n_blocks),
        grid=(n_tiles,),
        in_specs=[tile_spec(d_model), c1((1, d_model)), c1((d_model, 5 * width)),
                  c1(wb.shape), c1(wc.shape), c1(a_re8.shape), c1(a_im8.shape),
                  c1((1, width)), c1((width, width)), c1((1, width))],
        out_specs=[tile_spec(width)] * 5,
        out_shape=[act(F32)] * 5,
        scratch_shapes=[pltpu.VMEM((width // LANES, m_rows, LANES), F32), pltpu.VMEM((m_rows, 2 * n_lanes), F32),
                        pltpu.VMEM((m_rows, n_lanes), F32), pltpu.VMEM((m_rows, n_lanes), F32),
                        pltpu.VMEM((S5_SEQS, 2 * n_lanes), F32)],
        compiler_params=_cparams("arbitrary"),
        name="inproj_s5",
    )(x, row(in_norm_g[0]), w_in[0].astype(BF16), wb, wc, a_re8, a_im8, row(s5_d[0]),
      s5_w_glu[0].astype(BF16), row(s5_b_glu[0]))

    ttok = pl.BlockSpec((None, tl, width), lambda b, t: (b, t, 0))
    o_hg = pl.pallas_call(
        functools.partial(_hgrn2_kernel, n_chunks=tl // HG_CHUNK, n_heads=n_heads),
        grid=(bsz, seq_len // tl),
        in_specs=[ttok] * 4 + [const2((1, width))] * 2,
        out_specs=ttok,
        out_shape=jax.ShapeDtypeStruct((bsz, seq_len, width), F32),
        scratch_shapes=[pltpu.VMEM((n_heads, HG_HEAD_DIM, HG_HEAD_DIM), F32)],
        compiler_params=_cparams("arbitrary", "arbitrary"),
        name="hgrn2",
    )(q, fz, v, gz, lb, row(hg_norm_g[0]))

    bf = FFN_BLOCK
    assert d_ff % bf == 0
    n_blk = d_ff // bf
    by_block = lambda a: a.reshape(a.shape[0], 2 * n_blk, bf).transpose(1, 0, 2)
    out = pl.pallas_call(
        functools.partial(_ffn_kernel, n_blk=n_blk),
        grid=(bsz, seq_len // tm),
        in_specs=[tok(d_model), tok(width), tok(width), resident((2 * width, d_model)), const2((1, d_model)),
                  resident((2 * n_blk, d_model, bf)), const2((2 * n_blk, CONV_WIDTH, bf)),
                  const2((2 * n_blk, 1, bf)), resident((d_ff, d_model)), const2((1, d_model))],
        out_specs=tok(d_model),
        out_shape=jax.ShapeDtypeStruct((bsz, seq_len, d_model), F32),
        scratch_shapes=[pltpu.VMEM((2 * n_blk, FFN_ROWS, bf), F32), pltpu.VMEM((tm, d_model), F32),
                        pltpu.VMEM((tm, d_model), BF16), pltpu.VMEM((2, 2, tm + FFN_ROWS, bf), F32),
                        pltpu.VMEM((tm, d_ff), BF16)],
        compiler_params=_cparams("arbitrary", "arbitrary"),
        name="ffn",
    )(x, o_hg, y_s5, w_out[0].astype(BF16), row(ffn_norm_g[0]),
      by_block(w_up[0]).astype(BF16), by_block(conv_w[0]), by_block(row(conv_b[0])),
      w_down[0].astype(BF16), row(final_norm_g))
    return out
```

```python
import functools

import jax
import jax.numpy as jnp
from jax import lax
from jax.experimental import pallas as pl
from jax.experimental.pallas import tpu as pltpu

EPS = 1e-6
HG_CHUNK = 64
HG_HEAD_DIM = 128
S5_GROUP = 16
S5_STATE = 64
S5_SEG = 64
S5_SEQS = 8
S5_SCAN_BLOCKS = 2
S5_SCAN_UNROLL = True
CONV_WIDTH = 3
FFN_BLOCK = 256
FFN_ROWS = 16
SUBLANES = 8
LANES = 128
VMEM_LIMIT = 60 * 1024 * 1024

F32 = jnp.float32
BF16 = jnp.bfloat16

_NT = (((1,), (1,)), ((), ()))
_TN = (((0,), (0,)), ((), ()))


def _rms(x, g):
    ms = jnp.mean(x * x, axis=-1, keepdims=True)
    return x * lax.rsqrt(ms + EPS) * g


def _cparams(*sem):
    return pltpu.CompilerParams(dimension_semantics=sem, vmem_limit_bytes=VMEM_LIMIT)


def _param_kernel(lb_ref, are_ref, aim_ref, ldt_ref, bre_ref, bim_ref,
                  lb_out, abr_out, abi_out, bbr_out, bbi_out):
    z = lb_ref[...]
    e = jnp.exp(z - jnp.max(z, axis=0, keepdims=True))
    sm = e / jnp.sum(e, axis=0, keepdims=True)
    lb_out[...] = sm[0:1, :]
    dt = jnp.exp(ldt_ref[...])
    ar = are_ref[...]
    ai = aim_ref[...]
    mag = jnp.exp(dt * ar)
    abr = mag * jnp.cos(dt * ai)
    abi = mag * jnp.sin(dt * ai)
    nr = abr - 1.0
    den = ar * ar + ai * ai
    zr = (nr * ar + abi * ai) / den
    zi = (abi * ar - nr * ai) / den
    br = bre_ref[...]
    bi = bim_ref[...]
    abr_out[...] = abr
    abi_out[...] = abi
    bbr_out[...] = zr * br - zi * bi
    bbi_out[...] = zr * bi + zi * br


def _hgrn2_kernel(q_ref, f_ref, v_ref, g_ref, lb_ref, ng_ref, o_ref, st_ref, *, n_chunks, n_heads):
    @pl.when(pl.program_id(1) == 0)
    def _():
        st_ref[...] = jnp.zeros_like(st_ref)

    lb = lb_ref[...]
    ng = ng_ref[...]
    c = HG_CHUNK
    tri = lax.broadcasted_iota(jnp.int32, (c, c), 0) >= lax.broadcasted_iota(jnp.int32, (c, c), 1)
    tri_b = tri.astype(BF16)

    f = lb + (1.0 - lb) * jax.nn.sigmoid(f_ref[...])
    k = 1.0 - f
    logf = jnp.log(f)
    hi = logf.astype(BF16)
    r1 = logf - hi.astype(F32)
    mid = r1.astype(BF16)
    lo = (r1 - mid.astype(F32)).astype(BF16)
    bs = []
    for ci in range(n_chunks):
        rows = slice(ci * c, (ci + 1) * c)
        bs.append(jnp.dot(tri_b, hi[rows], preferred_element_type=F32)
                  + (jnp.dot(tri_b, mid[rows], preferred_element_type=F32)
                     + jnp.dot(tri_b, lo[rows], preferred_element_type=F32)))
    b = jnp.concatenate(bs, axis=0)
    kd_f = k * jnp.exp(-b)
    qd = (q_ref[...] * jnp.exp(b)).astype(BF16)
    kd = kd_f.astype(BF16)
    vb = v_ref[...].astype(BF16)
    gz = g_ref[...]
    gate = ng * (gz * jax.nn.sigmoid(gz))
    decs = [jnp.exp(b[(ci + 1) * c - 1:(ci + 1) * c, :]) for ci in range(n_chunks)]
    kts = [(kd_f[ci * c:(ci + 1) * c] * decs[ci]).astype(BF16) for ci in range(n_chunks)]

    for hd in range(n_heads):
        cs = slice(hd * HG_HEAD_DIM, (hd + 1) * HG_HEAD_DIM)
        st = st_ref[hd]
        for ci in range(n_chunks):
            rows = slice(ci * c, (ci + 1) * c)
            att = lax.dot_general(qd[rows, cs], kd[rows, cs], _NT, preferred_element_type=F32)
            att = jnp.where(tri, att, 0.0)
            o = jnp.dot(att.astype(BF16), vb[rows, cs], preferred_element_type=F32)
            o = o + lax.dot_general(qd[rows, cs], st.astype(BF16), _NT, preferred_element_type=F32)
            ds = lax.dot_general(vb[rows, cs], kts[ci][:, cs], _TN, preferred_element_type=F32)
            st = st * decs[ci][:, cs] + ds
            o = o * lax.rsqrt(jnp.mean(o * o, axis=-1, keepdims=True) + EPS)
            o_ref[rows, cs] = o * gate[rows, cs]
        st_ref[hd] = st


def _inproj_s5_kernel(x_ref, gin_ref, win32_ref, wb_ref, wc_ref, are_ref, aim_ref, d_ref, wglu32_ref, bglu_ref,
                      q_ref, f_ref, v_ref, gz_ref, y_ref,
                      win_ref, wglu_ref, il_ref, bu_ref, pre_ref, pim_ref, carry_ref, *, n_blocks):
    h = S5_SEG
    s = S5_SEQS
    half = wb_ref.shape[2] // 2
    kw = wb_ref.shape[1]
    bsz, tt, d_model = x_ref.shape
    width = y_ref.shape[-1]

    @pl.when(pl.program_id(0) == 0)
    def _():
        win_ref[...] = win32_ref[...].astype(BF16)
        wglu_ref[...] = wglu32_ref[...].astype(BF16)
        carry_ref[...] = jnp.zeros_like(carry_ref)
        ar = are_ref[...]
        ai = aim_ref[...]
        pre_ref[0:s, :] = ar
        pim_ref[0:s, :] = ai

        def powers(i, c):
            pr, pi = c
            npr = pr * ar - pi * ai
            npi = pr * ai + pi * ar
            r = pl.ds(pl.multiple_of(i * s, s), s)
            pre_ref[r, :] = npr
            pim_ref[r, :] = npi
            return npr, npi

        lax.fori_loop(1, h, powers, (ar, ai))

    xb = _rms(x_ref[...].reshape(bsz * tt, d_model), gin_ref[...]).astype(BF16)
    u_nat = jnp.dot(xb, win_ref[:, 4 * width:5 * width], preferred_element_type=F32)

    n_slabs = il_ref.shape[0]
    for bi in range(bsz):
        for k in range(2):
            r0 = bi * tt + k * h
            for j in range(n_slabs):
                il_ref[j, pl.ds(2 * bi + k, h, stride=s), :] = u_nat[r0:r0 + h, j * LANES:(j + 1) * LANES]
    u = jnp.concatenate([il_ref[j] for j in range(n_slabs)], axis=1)
    ub = u.astype(BF16)
    for m in range(n_blocks):
        bu_ref[:, 2 * half * m:2 * half * (m + 1)] = jnp.dot(
            ub[:, kw * m:kw * (m + 1)], wb_ref[m], preferred_element_type=F32)

    for i, o in enumerate((q_ref, f_ref, v_ref, gz_ref)):
        o[...] = jnp.dot(xb, win_ref[:, i * width:(i + 1) * width],
                         preferred_element_type=F32).reshape(bsz, tt, width).astype(o.dtype)

    second = lax.broadcasted_iota(jnp.int32, (s, half), 0) % 2 == 1
    last = slice((h - 1) * s, h * s)

    for m0 in range(0, n_blocks, S5_SCAN_BLOCKS):
        blocks = []
        for m in range(m0, m0 + S5_SCAN_BLOCKS):
            lo = 2 * half * m
            blocks.append((slice(lo, lo + half), slice(lo + half, lo + 2 * half),
                           slice(half * m, half * (m + 1))))
        coef = [(are_ref[:, am], aim_ref[:, am]) for _, _, am in blocks]

        def scan(i, c, blocks=blocks, coef=coef):
            r = pl.ds(pl.multiple_of(i * s, s), s)
            out = []
            for (re, im, _), (ar, ai), (xr, xi) in zip(blocks, coef, c):
                nxr = ar * xr - ai * xi + bu_ref[r, re]
                nxi = ar * xi + ai * xr + bu_ref[r, im]
                bu_ref[r, re] = nxr
                bu_ref[r, im] = nxi
                out.append((nxr, nxi))
            return tuple(out)

        init = tuple((carry_ref[:, re], carry_ref[:, im]) for re, im, _ in blocks)
        ends = lax.fori_loop(0, h, scan, init, unroll=S5_SCAN_UNROLL)
        starts = [(jnp.where(second, pltpu.roll(er, 1, axis=0), 0.0),
                   jnp.where(second, pltpu.roll(ei, 1, axis=0), 0.0)) for er, ei in ends]

        def fix(i, c, blocks=blocks, starts=starts):
            r = pl.ds(pl.multiple_of(i * s, s), s)
            for (re, im, am), (cr, ci) in zip(blocks, starts):
                pr = pre_ref[r, am]
                pi = pim_ref[r, am]
                bu_ref[r, re] = bu_ref[r, re] + (pr * cr - pi * ci)
                bu_ref[r, im] = bu_ref[r, im] + (pr * ci + pi * cr)
            return c

        lax.fori_loop(0, h, fix, 0, unroll=S5_SCAN_UNROLL)
        for re, im, _ in blocks:
            carry_ref[:, re] = jnp.where(second, 0.0, pltpu.roll(bu_ref[last, re], s - 1, axis=0))
            carry_ref[:, im] = jnp.where(second, 0.0, pltpu.roll(bu_ref[last, im], s - 1, axis=0))

    ys = [jnp.dot(bu_ref[:, 2 * half * m:2 * half * (m + 1)].astype(BF16), wc_ref[m],
                  preferred_element_type=F32) for m in range(n_blocks)]
    y = jnp.concatenate(ys, axis=1) + d_ref[...] * u
    y = jax.nn.gelu(y)
    z = jnp.dot(y.astype(BF16), wglu_ref[...], preferred_element_type=F32) + bglu_ref[...]
    y = y * jax.nn.sigmoid(z)
    for j in range(n_slabs):
        il_ref[j] = y[:, j * LANES:(j + 1) * LANES]
    for bi in range(y_ref.shape[0]):
        for k in range(2):
            for j in range(n_slabs):
                y_ref[bi, k * h:(k + 1) * h, j * LANES:(j + 1) * LANES] = il_ref[
                    j, pl.ds(2 * bi + k, h, stride=s), :]


def _ffn_kernel(x_ref, ohg_ref, y_ref, wout32_ref, gf_ref, wup_ref, cw_ref, cb_ref, wdn32_ref, g_ref, o_ref,
                wout_ref, wdn_ref, tail_ref, h_ref, hn_ref, pre_ref, act_ref, *, n_blk):
    @pl.when((pl.program_id(0) == 0) & (pl.program_id(1) == 0))
    def _():
        wout_ref[...] = wout32_ref[...].astype(BF16)
        wdn_ref[...] = wdn32_ref[...].astype(BF16)

    @pl.when(pl.program_id(1) == 0)
    def _():
        tail_ref[...] = jnp.zeros_like(tail_ref)

    width = ohg_ref.shape[-1]
    mix = jnp.dot(ohg_ref[...].astype(BF16), wout_ref[0:width, :], preferred_element_type=F32)
    mix = mix + jnp.dot(y_ref[...].astype(BF16), wout_ref[width:2 * width, :], preferred_element_type=F32)
    hres = x_ref[...] + mix
    h_ref[...] = hres
    hn_ref[...] = _rms(hres, gf_ref[...]).astype(BF16)

    tm = act_ref.shape[0]
    bf = pre_ref.shape[-1]
    rows = FFN_ROWS
    top = lax.broadcasted_iota(jnp.int32, (SUBLANES, bf), 0)

    def up(j, slot):
        for part in range(2):
            pre_ref[slot, part, rows:rows + tm] = jnp.dot(hn_ref[...], wup_ref[part * n_blk + j],
                                                          preferred_element_type=F32)

    def mid(j, slot):
        taps, prev = [], []
        for part in range(2):
            blk = part * n_blk + j
            pre_ref[slot, part, 0:rows] = tail_ref[blk]
            tail_ref[blk] = pre_ref[slot, part, tm:tm + rows]
            cw = cw_ref[blk]
            taps.append((cb_ref[blk], cw[0:1, :], cw[1:2, :], cw[2:3, :]))
            p = pre_ref[slot, part, 0:rows]
            prev.append((pltpu.roll(p, 1, axis=0), pltpu.roll(p, 2, axis=0)))
        for g in range(tm // rows):
            r0 = rows * (g + 1)
            outs = []
            for part in range(2):
                cur = pre_ref[slot, part, r0:r0 + rows]
                rc = (pltpu.roll(cur, 1, axis=0), pltpu.roll(cur, 2, axis=0))
                sh = [jnp.concatenate([jnp.where(top < d + 1, prev[part][d][0:SUBLANES], rc[d][0:SUBLANES]),
                                       rc[d][SUBLANES:]], axis=0) for d in range(2)]
                prev[part] = rc
                cb, w0, w1, w2 = taps[part]
                outs.append(cb + w0 * sh[1] + w1 * sh[0] + w2 * cur)
            gate, val = outs
            act_ref[rows * g:rows * (g + 1), j * bf:(j + 1) * bf] = (
                (gate * jax.nn.sigmoid(gate)) * val).astype(BF16)

    up(0, 0)
    for j in range(n_blk):
        if j + 1 < n_blk:
            up(j + 1, (j + 1) % 2)
        mid(j, j % 2)
    out = h_ref[...] + jnp.dot(act_ref[...], wdn_ref[...], preferred_element_type=F32)
    o_ref[...] = _rms(out, g_ref[...])


def _tile_sizes(seq_len):
    tm = min(512, seq_len)
    tl = min(256, seq_len)
    assert seq_len % tm == 0 and seq_len % tl == 0 and tm % (2 * S5_SEG) == 0 and tl % HG_CHUNK == 0
    return tm, tl


def kernel(x, in_norm_g, w_in, hg_lb, hg_norm_g, s5_a_re, s5_a_im, s5_log_dt, s5_b_re, s5_b_im,
           s5_c_re, s5_c_im, s5_d, s5_w_glu, s5_b_glu, w_out, ffn_norm_g, w_up, conv_w, conv_b,
           w_down, final_norm_g):
    bsz, seq_len, d_model = x.shape
    depth = w_in.shape[0]
    assert depth == 1, "single-layer block"
    width = hg_norm_g.shape[1]
    assert w_in.shape[2] == 5 * width and s5_d.shape[1] == width
    n_heads = width // HG_HEAD_DIM
    groups, n_state, n_chan = s5_b_re.shape[1:]
    assert (n_state, n_chan) == (S5_STATE, S5_GROUP) and groups * n_chan == width
    assert 2 * bsz == S5_SEQS
    d_ff = w_down.shape[1]
    tm, tl = _tile_sizes(seq_len)
    n_tiles = seq_len // (2 * S5_SEG)
    row = lambda a: a.reshape(1, -1)

    gp = groups * n_chan
    rep = lambda a: jnp.repeat(a, n_chan, axis=0)
    ldt = jnp.broadcast_to(jnp.repeat(s5_log_dt[0], n_chan)[:, None], (gp, n_state))
    b_t = lambda a: a.transpose(0, 2, 1).reshape(gp, n_state)
    lb, abr, abi, bbr, bbi = pl.pallas_call(
        _param_kernel,
        out_shape=[jax.ShapeDtypeStruct((1, width), F32)] + [jax.ShapeDtypeStruct((gp, n_state), F32)] * 4,
        name="params",
    )(hg_lb, rep(s5_a_re[0]), rep(s5_a_im[0]), ldt, b_t(s5_b_re[0]), b_t(s5_b_im[0]))

    gpb = LANES // n_chan
    n_blocks = groups // gpb
    eye = jnp.eye(gpb, dtype=F32)
    blockdiag_in = lambda w: jnp.einsum("mgpn,gh->mgphn", w.reshape(n_blocks, gpb, n_chan, n_state),
                                        eye).reshape(n_blocks, gpb * n_chan, gpb * n_state)
    blockdiag_out = lambda w: jnp.einsum("mgpn,gh->mhngp", w.reshape(n_blocks, gpb, n_chan, n_state),
                                         eye).reshape(n_blocks, gpb * n_state, gpb * n_chan)
    wb = jnp.concatenate([blockdiag_in(bbr), blockdiag_in(bbi)], axis=2).astype(BF16)
    wc = jnp.concatenate([blockdiag_out(s5_c_re[0]), -blockdiag_out(s5_c_im[0])], axis=1).astype(BF16)
    n_lanes = groups * n_state
    a_rows = lambda a: jnp.broadcast_to(a.reshape(groups, n_chan, n_state)[:, 0, :].reshape(1, n_lanes),
                                        (S5_SEQS, n_lanes))
    a_re8, a_im8 = a_rows(abr), a_rows(abi)

    tok = lambda w: pl.BlockSpec((None, tm, w), lambda b, t: (b, t, 0))
    const2 = lambda shape: pl.BlockSpec(shape, lambda b, t: (0,) * len(shape))
    resident = lambda shape: pl.BlockSpec(shape, lambda b, t: (0,) * len(shape), pipeline_mode=pl.Buffered(1))

    m_rows = S5_SEG * S5_SEQS
    c1 = lambda shape: pl.BlockSpec(shape, lambda j: (0,) * len(shape))
    res1 = lambda shape: pl.BlockSpec(shape, lambda j: (0,) * len(shape), pipeline_mode=pl.Buffered(1))
    tile_spec = lambda w: pl.BlockSpec((bsz, 2 * S5_SEG, w), lambda j: (0, j, 0))
    act = lambda dt: jax.ShapeDtypeStruct((bsz, seq_len, width), dt)
    q, fz, v, gz, y_s5 = pl.pallas_call(
        functools.partial(_inproj_s5_kernel, n_blocks=n_blocks),
        grid=(n_tiles,),
        in_specs=[tile_spec(d_model), c1((1, d_model)), res1((d_model, 5 * width)),
                  c1(wb.shape), c1(wc.shape), c1(a_re8.shape), c1(a_im8.shape),
                  c1((1, width)), res1((width, width)), c1((1, width))],
        out_specs=[tile_spec(width)] * 5,
        out_shape=[act(F32)] * 5,
        scratch_shapes=[pltpu.VMEM((d_model, 5 * width), BF16), pltpu.VMEM((width, width), BF16),
                        pltpu.VMEM((width // LANES, m_rows, LANES), F32), pltpu.VMEM((m_rows, 2 * n_lanes), F32),
                        pltpu.VMEM((m_rows, n_lanes), F32), pltpu.VMEM((m_rows, n_lanes), F32),
                        pltpu.VMEM((S5_SEQS, 2 * n_lanes), F32)],
        compiler_params=_cparams("arbitrary"),
        name="inproj_s5",
    )(x, row(in_norm_g[0]), w_in[0], wb, wc, a_re8, a_im8, row(s5_d[0]), s5_w_glu[0], row(s5_b_glu[0]))

    ttok = pl.BlockSpec((None, tl, width), lambda b, t: (b, t, 0))
    o_hg = pl.pallas_call(
        functools.partial(_hgrn2_kernel, n_chunks=tl // HG_CHUNK, n_heads=n_heads),
        grid=(bsz, seq_len // tl),
        in_specs=[ttok] * 4 + [const2((1, width))] * 2,
        out_specs=ttok,
        out_shape=jax.ShapeDtypeStruct((bsz, seq_len, width), F32),
        scratch_shapes=[pltpu.VMEM((n_heads, HG_HEAD_DIM, HG_HEAD_DIM), F32)],
        compiler_params=_cparams("arbitrary", "arbitrary"),
        name="hgrn2",
    )(q, fz, v, gz, lb, row(hg_norm_g[0]))

    bf = FFN_BLOCK
    assert d_ff % bf == 0
    n_blk = d_ff // bf
    by_block = lambda a: a.reshape(a.shape[0], 2 * n_blk, bf).transpose(1, 0, 2)
    out = pl.pallas_call(
        functools.partial(_ffn_kernel, n_blk=n_blk),
        grid=(bsz, seq_len // tm),
        in_specs=[tok(d_model), tok(width), tok(width), resident((2 * width, d_model)), const2((1, d_model)),
                  resident((2 * n_blk, d_model, bf)), const2((2 * n_blk, CONV_WIDTH, bf)),
                  const2((2 * n_blk, 1, bf)), resident((d_ff, d_model)), const2((1, d_model))],
        out_specs=tok(d_model),
        out_shape=jax.ShapeDtypeStruct((bsz, seq_len, d_model), F32),
        scratch_shapes=[pltpu.VMEM((2 * width, d_model), BF16), pltpu.VMEM((d_ff, d_model), BF16),
                        pltpu.VMEM((2 * n_blk, FFN_ROWS, bf), F32), pltpu.VMEM((tm, d_model), F32),
                        pltpu.VMEM((tm, d_model), BF16), pltpu.VMEM((2, 2, tm + FFN_ROWS, bf), F32),
                        pltpu.VMEM((tm, d_ff), BF16)],
        compiler_params=_cparams("arbitrary", "arbitrary"),
        name="ffn",
    )(x, o_hg, y_s5, w_out[0], row(ffn_norm_g[0]),
      by_block(w_up[0]).astype(BF16), by_block(conv_w[0]), by_block(row(conv_b[0])),
      w_down[0], row(final_norm_g))
    return out
```

```python
import functools
import itertools

import jax
import jax.numpy as jnp
from jax import lax
from jax.experimental import pallas as pl
from jax.experimental.pallas import tpu as pltpu

EPS = 1e-6
HG_CHUNK = 64
HG_HEAD_DIM = 128
S5_GROUP = 16
S5_STATE = 64
S5_SEG = 64
S5_SEQS = 8
S5_SCAN_BLOCKS = 2
CONV_WIDTH = 3
FFN_BLOCK = 256
FFN_ROWS = 16
SUBLANES = 8
LANES = 128
VMEM_LIMIT = 60 * 1024 * 1024

F32 = jnp.float32
BF16 = jnp.bfloat16

_NT = (((1,), (1,)), ((), ()))
_TN = (((0,), (0,)), ((), ()))


def _rms(x, g):
    ms = jnp.mean(x * x, axis=-1, keepdims=True)
    return x * lax.rsqrt(ms + EPS) * g


def _cparams(*sem):
    return pltpu.CompilerParams(dimension_semantics=sem, vmem_limit_bytes=VMEM_LIMIT)


def _param_kernel(lb_ref, are_ref, aim_ref, ldt_ref, bre_ref, bim_ref,
                  lb_out, abr_out, abi_out, bbr_out, bbi_out):
    z = lb_ref[...]
    e = jnp.exp(z - jnp.max(z, axis=0, keepdims=True))
    sm = e / jnp.sum(e, axis=0, keepdims=True)
    lb_out[...] = sm[0:1, :]
    dt = jnp.exp(ldt_ref[...])
    ar = are_ref[...]
    ai = aim_ref[...]
    mag = jnp.exp(dt * ar)
    abr = mag * jnp.cos(dt * ai)
    abi = mag * jnp.sin(dt * ai)
    nr = abr - 1.0
    den = ar * ar + ai * ai
    zr = (nr * ar + abi * ai) / den
    zi = (abi * ar - nr * ai) / den
    br = bre_ref[...]
    bi = bim_ref[...]
    abr_out[...] = abr
    abi_out[...] = abi
    bbr_out[...] = zr * br - zi * bi
    bbi_out[...] = zr * bi + zi * br


def _hgrn2_kernel(q_ref, f_ref, v_ref, g_ref, lb_ref, ng_ref, o_ref, st_ref, *, n_chunks, n_heads):
    @pl.when(pl.program_id(1) == 0)
    def _():
        st_ref[...] = jnp.zeros_like(st_ref)

    lb = lb_ref[...]
    ng = ng_ref[...]
    c = HG_CHUNK
    tri = lax.broadcasted_iota(jnp.int32, (c, c), 0) >= lax.broadcasted_iota(jnp.int32, (c, c), 1)
    tri_b = tri.astype(BF16)

    f = lb + (1.0 - lb) * jax.nn.sigmoid(f_ref[...])
    k = 1.0 - f
    logf = jnp.log(f)
    hi = logf.astype(BF16)
    r1 = logf - hi.astype(F32)
    mid = r1.astype(BF16)
    lo = (r1 - mid.astype(F32)).astype(BF16)
    bs = []
    for ci in range(n_chunks):
        rows = slice(ci * c, (ci + 1) * c)
        bs.append(jnp.dot(tri_b, hi[rows], preferred_element_type=F32)
                  + (jnp.dot(tri_b, mid[rows], preferred_element_type=F32)
                     + jnp.dot(tri_b, lo[rows], preferred_element_type=F32)))
    b = jnp.concatenate(bs, axis=0)
    kd_f = k * jnp.exp(-b)
    qd = (q_ref[...] * jnp.exp(b)).astype(BF16)
    kd = kd_f.astype(BF16)
    vb = v_ref[...].astype(BF16)
    gz = g_ref[...]
    gate = ng * (gz * jax.nn.sigmoid(gz))
    decs = [jnp.exp(b[(ci + 1) * c - 1:(ci + 1) * c, :]) for ci in range(n_chunks)]
    kts = [(kd_f[ci * c:(ci + 1) * c] * decs[ci]).astype(BF16) for ci in range(n_chunks)]

    for hd in range(n_heads):
        cs = slice(hd * HG_HEAD_DIM, (hd + 1) * HG_HEAD_DIM)
        st = st_ref[hd]
        for ci in range(n_chunks):
            rows = slice(ci * c, (ci + 1) * c)
            att = lax.dot_general(qd[rows, cs], kd[rows, cs], _NT, preferred_element_type=F32)
            att = jnp.where(tri, att, 0.0)
            o = jnp.dot(att.astype(BF16), vb[rows, cs], preferred_element_type=F32)
            o = o + lax.dot_general(qd[rows, cs], st.astype(BF16), _NT, preferred_element_type=F32)
            ds = lax.dot_general(vb[rows, cs], kts[ci][:, cs], _TN, preferred_element_type=F32)
            st = st * decs[ci][:, cs] + ds
            o = o * lax.rsqrt(jnp.mean(o * o, axis=-1, keepdims=True) + EPS)
            o_ref[rows, cs] = o * gate[rows, cs]
        st_ref[hd] = st


def _inproj_s5_kernel(x_ref, gin_ref, win32_ref, wb_ref, wc_ref, are_ref, aim_ref, d_ref, wglu32_ref, bglu_ref,
                      q_ref, f_ref, v_ref, gz_ref, y_ref,
                      win_ref, wglu_ref, xb_ref, il_ref, yc_ref, bu_ref, pre_ref, pim_ref, carry_ref, *, n_blocks):
    h = S5_SEG
    s = S5_SEQS
    half = wb_ref.shape[2] // 2
    kw = wb_ref.shape[1]
    bsz, tt, d_model = x_ref.shape
    width = y_ref.shape[-1]

    @pl.when(pl.program_id(0) == 0)
    def _():
        win_ref[...] = win32_ref[...].astype(BF16)
        wglu_ref[...] = wglu32_ref[...].astype(BF16)
        carry_ref[...] = jnp.zeros_like(carry_ref)
        ar = are_ref[...]
        ai = aim_ref[...]
        pre_ref[0:s, :] = ar
        pim_ref[0:s, :] = ai

        def powers(i, c):
            pr, pi = c
            npr = pr * ar - pi * ai
            npi = pr * ai + pi * ar
            r = pl.ds(pl.multiple_of(i * s, s), s)
            pre_ref[r, :] = npr
            pim_ref[r, :] = npi
            return npr, npi

        lax.fori_loop(1, h, powers, (ar, ai))

    n_slabs = il_ref.shape[0]
    nt = 2 * LANES
    second = lax.broadcasted_iota(jnp.int32, (s, half), 0) % 2 == 1
    last = slice((h - 1) * s, h * s)
    lanes = lambda m: slice(2 * half * m, 2 * half * (m + 1))

    assert kw == LANES

    def b_proj(m):
        bu_ref[:, lanes(m)] = jnp.dot(il_ref[m].astype(BF16), wb_ref[m], preferred_element_type=F32)

    def c_proj(m):
        yc_ref[:, kw * m:kw * (m + 1)] = jnp.dot(bu_ref[:, lanes(m)].astype(BF16), wc_ref[m],
                                                 preferred_element_type=F32)

    def hg_proj(o, i, c0):
        w0 = i * width + c0
        o[:, :, c0:c0 + nt] = jnp.dot(xb_ref[...], win_ref[:, w0:w0 + nt],
                                      preferred_element_type=F32).reshape(bsz, tt, nt).astype(o.dtype)

    def state_work(m0):
        blocks = [(slice(2 * half * m, 2 * half * m + half), slice(2 * half * m + half, 2 * half * (m + 1)),
                   slice(half * m, half * (m + 1))) for m in range(m0, m0 + S5_SCAN_BLOCKS)]
        coef = [(are_ref[:, am], aim_ref[:, am]) for _, _, am in blocks]
        state = [(carry_ref[:, re], carry_ref[:, im]) for re, im, _ in blocks]
        for i in range(h):
            r = slice(i * s, (i + 1) * s)
            for bi, ((re, im, _), (ar, ai)) in enumerate(zip(blocks, coef)):
                xr, xi = state[bi]
                nxr = ar * xr - ai * xi + bu_ref[r, re]
                nxi = ar * xi + ai * xr + bu_ref[r, im]
                bu_ref[r, re] = nxr
                bu_ref[r, im] = nxi
                state[bi] = (nxr, nxi)
            yield
        starts = [(jnp.where(second, pltpu.roll(er, 1, axis=0), 0.0),
                   jnp.where(second, pltpu.roll(ei, 1, axis=0), 0.0)) for er, ei in state]
        for i in range(h):
            r = slice(i * s, (i + 1) * s)
            for (re, im, am), (cr, ci) in zip(blocks, starts):
                pr = pre_ref[r, am]
                pi = pim_ref[r, am]
                bu_ref[r, re] = bu_ref[r, re] + (pr * cr - pi * ci)
                bu_ref[r, im] = bu_ref[r, im] + (pr * ci + pi * cr)
            yield
        for re, im, _ in blocks:
            carry_ref[:, re] = jnp.where(second, 0.0, pltpu.roll(bu_ref[last, re], s - 1, axis=0))
            carry_ref[:, im] = jnp.where(second, 0.0, pltpu.roll(bu_ref[last, im], s - 1, axis=0))

    def alternate(matmuls, work, n_steps):
        per = -(-n_steps // len(matmuls))
        for mm in matmuls:
            mm()
            for _ in range(per):
                next(work, None)
        for _ in work:
            pass

    @pl.when(pl.program_id(0) >= 0)
    def _phase_in():
        xb = _rms(x_ref[...].reshape(bsz * tt, d_model), gin_ref[...]).astype(BF16)
        xb_ref[...] = xb
        u_nat = jnp.dot(xb, win_ref[:, 4 * width:5 * width], preferred_element_type=F32)
        for bi in range(bsz):
            for k in range(2):
                r0 = bi * tt + k * h
                for j in range(n_slabs):
                    il_ref[j, pl.ds(2 * bi + k, h, stride=s), :] = u_nat[r0:r0 + h, j * LANES:(j + 1) * LANES]
        for m in range(n_blocks):
            b_proj(m)

    @pl.when(pl.program_id(0) >= -1)
    def _phase_scan():
        mms = [functools.partial(hg_proj, o, i, c0) for i, o in enumerate((q_ref, f_ref, v_ref, gz_ref))
               for c0 in range(0, width, nt)]
        groups = range(0, n_blocks, S5_SCAN_BLOCKS)
        alternate(mms, itertools.chain.from_iterable(state_work(g) for g in groups), 2 * h * len(groups))

    @pl.when(pl.program_id(0) >= -2)
    def _phase_out():
        for m in range(n_blocks):
            c_proj(m)
        u = jnp.concatenate([il_ref[j] for j in range(n_slabs)], axis=1)
        y = jax.nn.gelu(yc_ref[...] + d_ref[...] * u)
        z = jnp.dot(y.astype(BF16), wglu_ref[...], preferred_element_type=F32) + bglu_ref[...]
        y = y * jax.nn.sigmoid(z)
        for j in range(n_slabs):
            il_ref[j] = y[:, j * LANES:(j + 1) * LANES]
        for bi in range(bsz):
            for k in range(2):
                for j in range(n_slabs):
                    y_ref[bi, k * h:(k + 1) * h, j * LANES:(j + 1) * LANES] = il_ref[
                        j, pl.ds(2 * bi + k, h, stride=s), :]


def _ffn_kernel(x_ref, ohg_ref, y_ref, wout32_ref, gf_ref, wup_ref, cw_ref, cb_ref, wdn32_ref, g_ref, o_ref,
                wout_ref, wdn_ref, tail_ref, h_ref, hn_ref, pre_ref, act_ref, *, n_blk):
    @pl.when((pl.program_id(0) == 0) & (pl.program_id(1) == 0))
    def _():
        wout_ref[...] = wout32_ref[...].astype(BF16)
        wdn_ref[...] = wdn32_ref[...].astype(BF16)

    @pl.when(pl.program_id(1) == 0)
    def _():
        tail_ref[...] = jnp.zeros_like(tail_ref)

    width = ohg_ref.shape[-1]
    mix = jnp.dot(ohg_ref[...].astype(BF16), wout_ref[0:width, :], preferred_element_type=F32)
    mix = mix + jnp.dot(y_ref[...].astype(BF16), wout_ref[width:2 * width, :], preferred_element_type=F32)
    hres = x_ref[...] + mix
    h_ref[...] = hres
    hn_ref[...] = _rms(hres, gf_ref[...]).astype(BF16)

    tm = act_ref.shape[0]
    bf = pre_ref.shape[-1]
    rows = FFN_ROWS
    top = lax.broadcasted_iota(jnp.int32, (SUBLANES, bf), 0)

    def up(j, slot, part):
        pre_ref[slot, part, rows:rows + tm] = jnp.dot(hn_ref[...], wup_ref[part * n_blk + j],
                                                      preferred_element_type=F32)

    def mid(j, slot):
        taps, prev = [], []
        for part in range(2):
            blk = part * n_blk + j
            pre_ref[slot, part, 0:rows] = tail_ref[blk]
            tail_ref[blk] = pre_ref[slot, part, tm:tm + rows]
            cw = cw_ref[blk]
            taps.append((cb_ref[blk], cw[0:1, :], cw[1:2, :], cw[2:3, :]))
            p = pre_ref[slot, part, 0:rows]
            prev.append((pltpu.roll(p, 1, axis=0), pltpu.roll(p, 2, axis=0)))
        for g in range(tm // rows):
            r0 = rows * (g + 1)
            outs = []
            for part in range(2):
                cur = pre_ref[slot, part, r0:r0 + rows]
                rc = (pltpu.roll(cur, 1, axis=0), pltpu.roll(cur, 2, axis=0))
                sh = [jnp.concatenate([jnp.where(top < d + 1, prev[part][d][0:SUBLANES], rc[d][0:SUBLANES]),
                                       rc[d][SUBLANES:]], axis=0) for d in range(2)]
                prev[part] = rc
                cb, w0, w1, w2 = taps[part]
                outs.append(cb + w0 * sh[1] + w1 * sh[0] + w2 * cur)
            gate, val = outs
            act_ref[rows * g:rows * (g + 1), j * bf:(j + 1) * bf] = (
                (gate * jax.nn.sigmoid(gate)) * val).astype(BF16)
            yield

    up(0, 0, 0)
    up(0, 0, 1)
    n_groups = tm // rows
    for j in range(n_blk):
        work = mid(j, j % 2)
        for part in range(2):
            if j + 1 < n_blk:
                up(j + 1, (j + 1) % 2, part)
            for _ in range(n_groups // 2):
                next(work, None)
        for _ in work:
            pass
    out = h_ref[...] + jnp.dot(act_ref[...], wdn_ref[...], preferred_element_type=F32)
    o_ref[...] = _rms(out, g_ref[...])


def _tile_sizes(seq_len):
    tm = min(512, seq_len)
    tl = min(256, seq_len)
    assert seq_len % tm == 0 and seq_len % tl == 0 and tm % (2 * S5_SEG) == 0 and tl % HG_CHUNK == 0
    return tm, tl


def kernel(x, in_norm_g, w_in, hg_lb, hg_norm_g, s5_a_re, s5_a_im, s5_log_dt, s5_b_re, s5_b_im,
           s5_c_re, s5_c_im, s5_d, s5_w_glu, s5_b_glu, w_out, ffn_norm_g, w_up, conv_w, conv_b,
           w_down, final_norm_g):
    bsz, seq_len, d_model = x.shape
    depth = w_in.shape[0]
    assert depth == 1, "single-layer block"
    width = hg_norm_g.shape[1]
    assert w_in.shape[2] == 5 * width and s5_d.shape[1] == width
    n_heads = width // HG_HEAD_DIM
    groups, n_state, n_chan = s5_b_re.shape[1:]
    assert (n_state, n_chan) == (S5_STATE, S5_GROUP) and groups * n_chan == width
    assert 2 * bsz == S5_SEQS
    d_ff = w_down.shape[1]
    tm, tl = _tile_sizes(seq_len)
    n_tiles = seq_len // (2 * S5_SEG)
    row = lambda a: a.reshape(1, -1)

    gp = groups * n_chan
    rep = lambda a: jnp.repeat(a, n_chan, axis=0)
    ldt = jnp.broadcast_to(jnp.repeat(s5_log_dt[0], n_chan)[:, None], (gp, n_state))
    b_t = lambda a: a.transpose(0, 2, 1).reshape(gp, n_state)
    lb, abr, abi, bbr, bbi = pl.pallas_call(
        _param_kernel,
        out_shape=[jax.ShapeDtypeStruct((1, width), F32)] + [jax.ShapeDtypeStruct((gp, n_state), F32)] * 4,
        name="params",
    )(hg_lb, rep(s5_a_re[0]), rep(s5_a_im[0]), ldt, b_t(s5_b_re[0]), b_t(s5_b_im[0]))

    gpb = LANES // n_chan
    n_blocks = groups // gpb
    eye = jnp.eye(gpb, dtype=F32)
    blockdiag_in = lambda w: jnp.einsum("mgpn,gh->mgphn", w.reshape(n_blocks, gpb, n_chan, n_state),
                                        eye).reshape(n_blocks, gpb * n_chan, gpb * n_state)
    blockdiag_out = lambda w: jnp.einsum("mgpn,gh->mhngp", w.reshape(n_blocks, gpb, n_chan, n_state),
                                         eye).reshape(n_blocks, gpb * n_state, gpb * n_chan)
    wb = jnp.concatenate([blockdiag_in(bbr), blockdiag_in(bbi)], axis=2).astype(BF16)
    wc = jnp.concatenate([blockdiag_out(s5_c_re[0]), -blockdiag_out(s5_c_im[0])], axis=1).astype(BF16)
    n_lanes = groups * n_state
    a_rows = lambda a: jnp.broadcast_to(a.reshape(groups, n_chan, n_state)[:, 0, :].reshape(1, n_lanes),
                                        (S5_SEQS, n_lanes))
    a_re8, a_im8 = a_rows(abr), a_rows(abi)

    tok = lambda w: pl.BlockSpec((None, tm, w), lambda b, t: (b, t, 0))
    const2 = lambda shape: pl.BlockSpec(shape, lambda b, t: (0,) * len(shape))
    resident = lambda shape: pl.BlockSpec(shape, lambda b, t: (0,) * len(shape), pipeline_mode=pl.Buffered(1))

    m_rows = S5_SEG * S5_SEQS
    c1 = lambda shape: pl.BlockSpec(shape, lambda j: (0,) * len(shape))
    res1 = lambda shape: pl.BlockSpec(shape, lambda j: (0,) * len(shape), pipeline_mode=pl.Buffered(1))
    tile_spec = lambda w: pl.BlockSpec((bsz, 2 * S5_SEG, w), lambda j: (0, j, 0))
    act = lambda dt: jax.ShapeDtypeStruct((bsz, seq_len, width), dt)
    q, fz, v, gz, y_s5 = pl.pallas_call(
        functools.partial(_inproj_s5_kernel, n_blocks=n_blocks),
        grid=(n_tiles,),
        in_specs=[tile_spec(d_model), c1((1, d_model)), res1((d_model, 5 * width)),
                  c1(wb.shape), c1(wc.shape), c1(a_re8.shape), c1(a_im8.shape),
                  c1((1, width)), res1((width, width)), c1((1, width))],
        out_specs=[tile_spec(width)] * 5,
        out_shape=[act(F32)] * 5,
        scratch_shapes=[pltpu.VMEM((d_model, 5 * width), BF16), pltpu.VMEM((width, width), BF16),
                        pltpu.VMEM((m_rows, d_model), BF16), pltpu.VMEM((width // LANES, m_rows, LANES), F32),
                        pltpu.VMEM((m_rows, width), F32), pltpu.VMEM((m_rows, 2 * n_lanes), F32),
                        pltpu.VMEM((m_rows, n_lanes), F32), pltpu.VMEM((m_rows, n_lanes), F32),
                        pltpu.VMEM((S5_SEQS, 2 * n_lanes), F32)],
        compiler_params=_cparams("arbitrary"),
        name="inproj_s5",
    )(x, row(in_norm_g[0]), w_in[0], wb, wc, a_re8, a_im8, row(s5_d[0]), s5_w_glu[0], row(s5_b_glu[0]))

    ttok = pl.BlockSpec((None, tl, width), lambda b, t: (b, t, 0))
    o_hg = pl.pallas_call(
        functools.partial(_hgrn2_kernel, n_chunks=tl // HG_CHUNK, n_heads=n_heads),
        grid=(bsz, seq_len // tl),
        in_specs=[ttok] * 4 + [const2((1, width))] * 2,
        out_specs=ttok,
        out_shape=jax.ShapeDtypeStruct((bsz, seq_len, width), F32),
        scratch_shapes=[pltpu.VMEM((n_heads, HG_HEAD_DIM, HG_HEAD_DIM), F32)],
        compiler_params=_cparams("arbitrary", "arbitrary"),
        name="hgrn2",
    )(q, fz, v, gz, lb, row(hg_norm_g[0]))

    bf = FFN_BLOCK
    assert d_ff % bf == 0
    n_blk = d_ff // bf
    by_block = lambda a: a.reshape(a.shape[0], 2 * n_blk, bf).transpose(1, 0, 2)
    out = pl.pallas_call(
        functools.partial(_ffn_kernel, n_blk=n_blk),
        grid=(bsz, seq_len // tm),
        in_specs=[tok(d_model), tok(width), tok(width), resident((2 * width, d_model)), const2((1, d_model)),
                  resident((2 * n_blk, d_model, bf)), const2((2 * n_blk, CONV_WIDTH, bf)),
                  const2((2 * n_blk, 1, bf)), resident((d_ff, d_model)), const2((1, d_model))],
        out_specs=tok(d_model),
        out_shape=jax.ShapeDtypeStruct((bsz, seq_len, d_model), F32),
        scratch_shapes=[pltpu.VMEM((2 * width, d_model), BF16), pltpu.VMEM((d_ff, d_model), BF16),
                        pltpu.VMEM((2 * n_blk, FFN_ROWS, bf), F32), pltpu.VMEM((tm, d_model), F32),
                        pltpu.VMEM((tm, d_model), BF16), pltpu.VMEM((2, 2, tm + FFN_ROWS, bf), F32),
                        pltpu.VMEM((tm, d_ff), BF16)],
        compiler_params=_cparams("arbitrary", "arbitrary"),
        name="ffn",
    )(x, o_hg, y_s5, w_out[0], row(ffn_norm_g[0]),
      by_block(w_up[0]).astype(BF16), by_block(conv_w[0]), by_block(row(conv_b[0])),
      w_down[0], row(final_norm_g))
    return out
```

```python
import functools

import jax
import jax.numpy as jnp
from jax import lax
from jax.experimental import pallas as pl
from jax.experimental.pallas import tpu as pltpu

EPS = 1e-6
HG_CHUNK = 64
HG_HEAD_DIM = 128
S5_GROUP = 16
S5_STATE = 64
S5_SEG = 64
S5_SEQS = 8
S5_SCAN_BLOCKS = 2
S5_SCAN_UNROLL = True
CONV_WIDTH = 3
FFN_BLOCK = 256
FFN_ROWS = 16
SUBLANES = 8
LANES = 128
VMEM_LIMIT = 60 * 1024 * 1024

F32 = jnp.float32
BF16 = jnp.bfloat16

_NT = (((1,), (1,)), ((), ()))
_TN = (((0,), (0,)), ((), ()))


def _rms(x, g):
    ms = jnp.mean(x * x, axis=-1, keepdims=True)
    return x * lax.rsqrt(ms + EPS) * g


def _cparams(*sem):
    return pltpu.CompilerParams(dimension_semantics=sem, vmem_limit_bytes=VMEM_LIMIT)


def _param_kernel(lb_ref, are_ref, aim_ref, ldt_ref, bre_ref, bim_ref,
                  lb_out, abr_out, abi_out, bbr_out, bbi_out):
    z = lb_ref[...]
    e = jnp.exp(z - jnp.max(z, axis=0, keepdims=True))
    sm = e / jnp.sum(e, axis=0, keepdims=True)
    lb_out[...] = sm[0:1, :]
    dt = jnp.exp(ldt_ref[...])
    ar = are_ref[...]
    ai = aim_ref[...]
    mag = jnp.exp(dt * ar)
    abr = mag * jnp.cos(dt * ai)
    abi = mag * jnp.sin(dt * ai)
    nr = abr - 1.0
    den = ar * ar + ai * ai
    zr = (nr * ar + abi * ai) / den
    zi = (abi * ar - nr * ai) / den
    br = bre_ref[...]
    bi = bim_ref[...]
    abr_out[...] = abr
    abi_out[...] = abi
    bbr_out[...] = zr * br - zi * bi
    bbi_out[...] = zr * bi + zi * br


def _hgrn2_kernel(q_ref, f_ref, v_ref, g_ref, lb_ref, ng_ref, o_ref,
                  st_ref, qd_ref, kd_ref, kt_ref, vb_ref, gate_ref, dec_ref, *, n_chunks, n_heads):
    t = pl.program_id(1)
    c = HG_CHUNK
    lb = lb_ref[...]
    ng = ng_ref[...]
    tri = lax.broadcasted_iota(jnp.int32, (c, c), 0) >= lax.broadcasted_iota(jnp.int32, (c, c), 1)
    tri_b = tri.astype(BF16)

    def prep_chunk(slot, ci):
        rows = slice(ci * c, (ci + 1) * c)
        f = lb + (1.0 - lb) * jax.nn.sigmoid(f_ref[rows, :])
        k = 1.0 - f
        logf = jnp.log(f)
        hi = logf.astype(BF16)
        r1 = logf - hi.astype(F32)
        mid = r1.astype(BF16)
        lo = (r1 - mid.astype(F32)).astype(BF16)
        b = (jnp.dot(tri_b, hi, preferred_element_type=F32)
             + (jnp.dot(tri_b, mid, preferred_element_type=F32) + jnp.dot(tri_b, lo, preferred_element_type=F32)))
        kd_f = k * jnp.exp(-b)
        dec = jnp.exp(b[c - 1:c, :])
        qd_ref[slot, rows] = (q_ref[rows, :] * jnp.exp(b)).astype(BF16)
        kd_ref[slot, rows] = kd_f.astype(BF16)
        kt_ref[slot, rows] = (kd_f * dec).astype(BF16)
        vb_ref[slot, rows] = v_ref[rows, :].astype(BF16)
        gz = g_ref[rows, :]
        gate_ref[slot, rows] = ng * (gz * jax.nn.sigmoid(gz))
        dec_ref[slot, ci] = jnp.broadcast_to(dec, dec_ref.shape[2:])

    def unit(slot, hd, ci, st):
        rows = slice(ci * c, (ci + 1) * c)
        cs = slice(hd * HG_HEAD_DIM, (hd + 1) * HG_HEAD_DIM)
        qd = qd_ref[slot, rows, cs]
        vb = vb_ref[slot, rows, cs]
        att = lax.dot_general(qd, kd_ref[slot, rows, cs], _NT, preferred_element_type=F32)
        att = jnp.where(tri, att, 0.0)
        o = jnp.dot(att.astype(BF16), vb, preferred_element_type=F32)
        o = o + lax.dot_general(qd, st.astype(BF16), _NT, preferred_element_type=F32)
        ds = lax.dot_general(vb, kt_ref[slot, rows, cs], _TN, preferred_element_type=F32)
        st = st * dec_ref[slot, ci, 0:1, cs] + ds
        o = o * lax.rsqrt(jnp.mean(o * o, axis=-1, keepdims=True) + EPS)
        o_ref[rows, cs] = o * gate_ref[slot, rows, cs]
        return st

    def step(main_slot, prep_slot):
        sts = [st_ref[hd] for hd in range(n_heads)] if main_slot is not None else None
        for ci in range(n_chunks):
            prep_chunk(prep_slot, ci)
            if main_slot is not None:
                for hd in range(n_heads):
                    sts[hd] = unit(main_slot, hd, ci, sts[hd])
        if main_slot is not None:
            for hd in range(n_heads):
                st_ref[hd] = sts[hd]

    @pl.when(t == 0)
    def _():
        st_ref[...] = jnp.zeros_like(st_ref)
        step(None, 0)

    for parity in range(2):
        @pl.when((t > 0) & (t % 2 == parity))
        def _(parity=parity):
            step(1 - parity, parity)


def _inproj_s5_kernel(x_ref, gin_ref, win32_ref, wb_ref, wc_ref, are_ref, aim_ref, d_ref, wglu32_ref, bglu_ref,
                      q_ref, f_ref, v_ref, gz_ref, y_ref,
                      win_ref, wglu_ref, il_ref, bu_ref, pre_ref, pim_ref, carry_ref, *, n_blocks):
    h = S5_SEG
    s = S5_SEQS
    half = wb_ref.shape[2] // 2
    kw = wb_ref.shape[1]
    bsz, tt, d_model = x_ref.shape
    width = y_ref.shape[-1]

    @pl.when(pl.program_id(0) == 0)
    def _():
        win_ref[...] = win32_ref[...].astype(BF16)
        wglu_ref[...] = wglu32_ref[...].astype(BF16)
        carry_ref[...] = jnp.zeros_like(carry_ref)
        ar = are_ref[...]
        ai = aim_ref[...]
        pre_ref[0:s, :] = ar
        pim_ref[0:s, :] = ai

        def powers(i, c):
            pr, pi = c
            npr = pr * ar - pi * ai
            npi = pr * ai + pi * ar
            r = pl.ds(pl.multiple_of(i * s, s), s)
            pre_ref[r, :] = npr
            pim_ref[r, :] = npi
            return npr, npi

        lax.fori_loop(1, h, powers, (ar, ai))

    xb = _rms(x_ref[...].reshape(bsz * tt, d_model), gin_ref[...]).astype(BF16)
    u_nat = jnp.dot(xb, win_ref[:, 4 * width:5 * width], preferred_element_type=F32)

    n_slabs = il_ref.shape[0]
    for bi in range(bsz):
        for k in range(2):
            r0 = bi * tt + k * h
            for j in range(n_slabs):
                il_ref[j, pl.ds(2 * bi + k, h, stride=s), :] = u_nat[r0:r0 + h, j * LANES:(j + 1) * LANES]
    u = jnp.concatenate([il_ref[j] for j in range(n_slabs)], axis=1)
    ub = u.astype(BF16)
    for m in range(n_blocks):
        bu_ref[:, 2 * half * m:2 * half * (m + 1)] = jnp.dot(
            ub[:, kw * m:kw * (m + 1)], wb_ref[m], preferred_element_type=F32)

    for i, o in enumerate((q_ref, f_ref, v_ref, gz_ref)):
        o[...] = jnp.dot(xb, win_ref[:, i * width:(i + 1) * width],
                         preferred_element_type=F32).reshape(bsz, tt, width).astype(o.dtype)

    second = lax.broadcasted_iota(jnp.int32, (s, half), 0) % 2 == 1
    last = slice((h - 1) * s, h * s)

    for m0 in range(0, n_blocks, S5_SCAN_BLOCKS):
        blocks = []
        for m in range(m0, m0 + S5_SCAN_BLOCKS):
            lo = 2 * half * m
            blocks.append((slice(lo, lo + half), slice(lo + half, lo + 2 * half),
                           slice(half * m, half * (m + 1))))
        coef = [(are_ref[:, am], aim_ref[:, am]) for _, _, am in blocks]

        def scan(i, c, blocks=blocks, coef=coef):
            r = pl.ds(pl.multiple_of(i * s, s), s)
            out = []
            for (re, im, _), (ar, ai), (xr, xi) in zip(blocks, coef, c):
                nxr = ar * xr - ai * xi + bu_ref[r, re]
                nxi = ar * xi + ai * xr + bu_ref[r, im]
                bu_ref[r, re] = nxr
                bu_ref[r, im] = nxi
                out.append((nxr, nxi))
            return tuple(out)

        init = tuple((carry_ref[:, re], carry_ref[:, im]) for re, im, _ in blocks)
        ends = lax.fori_loop(0, h, scan, init, unroll=S5_SCAN_UNROLL)
        starts = [(jnp.where(second, pltpu.roll(er, 1, axis=0), 0.0),
                   jnp.where(second, pltpu.roll(ei, 1, axis=0), 0.0)) for er, ei in ends]

        def fix(i, c, blocks=blocks, starts=starts):
            r = pl.ds(pl.multiple_of(i * s, s), s)
            for (re, im, am), (cr, ci) in zip(blocks, starts):
                pr = pre_ref[r, am]
                pi = pim_ref[r, am]
                bu_ref[r, re] = bu_ref[r, re] + (pr * cr - pi * ci)
                bu_ref[r, im] = bu_ref[r, im] + (pr * ci + pi * cr)
            return c

        lax.fori_loop(0, h, fix, 0, unroll=S5_SCAN_UNROLL)
        for re, im, _ in blocks:
            carry_ref[:, re] = jnp.where(second, 0.0, pltpu.roll(bu_ref[last, re], s - 1, axis=0))
            carry_ref[:, im] = jnp.where(second, 0.0, pltpu.roll(bu_ref[last, im], s - 1, axis=0))

    ys = [jnp.dot(bu_ref[:, 2 * half * m:2 * half * (m + 1)].astype(BF16), wc_ref[m],
                  preferred_element_type=F32) for m in range(n_blocks)]
    y = jnp.concatenate(ys, axis=1) + d_ref[...] * u
    y = jax.nn.gelu(y)
    z = jnp.dot(y.astype(BF16), wglu_ref[...], preferred_element_type=F32) + bglu_ref[...]
    y = y * jax.nn.sigmoid(z)
    for j in range(n_slabs):
        il_ref[j] = y[:, j * LANES:(j + 1) * LANES]
    for bi in range(y_ref.shape[0]):
        for k in range(2):
            for j in range(n_slabs):
                y_ref[bi, k * h:(k + 1) * h, j * LANES:(j + 1) * LANES] = il_ref[
                    j, pl.ds(2 * bi + k, h, stride=s), :]


def _ffn_kernel(x_ref, ohg_ref, y_ref, wout32_ref, gf_ref, wup_ref, cw_ref, cb_ref, wdn32_ref, g_ref, o_ref,
                wout_ref, wdn_ref, tail_ref, h_ref, hn_ref, pre_ref, act_ref, *, n_blk):
    @pl.when((pl.program_id(0) == 0) & (pl.program_id(1) == 0))
    def _():
        wout_ref[...] = wout32_ref[...].astype(BF16)
        wdn_ref[...] = wdn32_ref[...].astype(BF16)

    @pl.when(pl.program_id(1) == 0)
    def _():
        tail_ref[...] = jnp.zeros_like(tail_ref)

    width = ohg_ref.shape[-1]
    mix = jnp.dot(ohg_ref[...].astype(BF16), wout_ref[0:width, :], preferred_element_type=F32)
    mix = mix + jnp.dot(y_ref[...].astype(BF16), wout_ref[width:2 * width, :], preferred_element_type=F32)
    hres = x_ref[...] + mix
    h_ref[...] = hres
    hn_ref[...] = _rms(hres, gf_ref[...]).astype(BF16)

    tm = act_ref.shape[0]
    bf = pre_ref.shape[-1]
    rows = FFN_ROWS
    top = lax.broadcasted_iota(jnp.int32, (SUBLANES, bf), 0)

    def up(j, slot):
        for part in range(2):
            pre_ref[slot, part, rows:rows + tm] = jnp.dot(hn_ref[...], wup_ref[part * n_blk + j],
                                                          preferred_element_type=F32)

    def mid(j, slot):
        taps, prev = [], []
        for part in range(2):
            blk = part * n_blk + j
            pre_ref[slot, part, 0:rows] = tail_ref[blk]
            tail_ref[blk] = pre_ref[slot, part, tm:tm + rows]
            cw = cw_ref[blk]
            taps.append((cb_ref[blk], cw[0:1, :], cw[1:2, :], cw[2:3, :]))
            p = pre_ref[slot, part, 0:rows]
            prev.append((pltpu.roll(p, 1, axis=0), pltpu.roll(p, 2, axis=0)))
        for g in range(tm // rows):
            r0 = rows * (g + 1)
            outs = []
            for part in range(2):
                cur = pre_ref[slot, part, r0:r0 + rows]
                rc = (pltpu.roll(cur, 1, axis=0), pltpu.roll(cur, 2, axis=0))
                sh = [jnp.concatenate([jnp.where(top < d + 1, prev[part][d][0:SUBLANES], rc[d][0:SUBLANES]),
                                       rc[d][SUBLANES:]], axis=0) for d in range(2)]
                prev[part] = rc
                cb, w0, w1, w2 = taps[part]
                outs.append(cb + w0 * sh[1] + w1 * sh[0] + w2 * cur)
            gate, val = outs
            act_ref[rows * g:rows * (g + 1), j * bf:(j + 1) * bf] = (
                (gate * jax.nn.sigmoid(gate)) * val).astype(BF16)

    up(0, 0)
    for j in range(n_blk):
        if j + 1 < n_blk:
            up(j + 1, (j + 1) % 2)
        mid(j, j % 2)
    out = h_ref[...] + jnp.dot(act_ref[...], wdn_ref[...], preferred_element_type=F32)
    o_ref[...] = _rms(out, g_ref[...])


def _tile_sizes(seq_len):
    tm = min(512, seq_len)
    tl = min(256, seq_len)
    assert seq_len % tm == 0 and seq_len % tl == 0 and tm % (2 * S5_SEG) == 0 and tl % HG_CHUNK == 0
    return tm, tl


def kernel(x, in_norm_g, w_in, hg_lb, hg_norm_g, s5_a_re, s5_a_im, s5_log_dt, s5_b_re, s5_b_im,
           s5_c_re, s5_c_im, s5_d, s5_w_glu, s5_b_glu, w_out, ffn_norm_g, w_up, conv_w, conv_b,
           w_down, final_norm_g):
    bsz, seq_len, d_model = x.shape
    depth = w_in.shape[0]
    assert depth == 1, "single-layer block"
    width = hg_norm_g.shape[1]
    assert w_in.shape[2] == 5 * width and s5_d.shape[1] == width
    n_heads = width // HG_HEAD_DIM
    groups, n_state, n_chan = s5_b_re.shape[1:]
    assert (n_state, n_chan) == (S5_STATE, S5_GROUP) and groups * n_chan == width
    assert 2 * bsz == S5_SEQS
    d_ff = w_down.shape[1]
    tm, tl = _tile_sizes(seq_len)
    n_tiles = seq_len // (2 * S5_SEG)
    row = lambda a: a.reshape(1, -1)

    gp = groups * n_chan
    rep = lambda a: jnp.repeat(a, n_chan, axis=0)
    ldt = jnp.broadcast_to(jnp.repeat(s5_log_dt[0], n_chan)[:, None], (gp, n_state))
    b_t = lambda a: a.transpose(0, 2, 1).reshape(gp, n_state)
    lb, abr, abi, bbr, bbi = pl.pallas_call(
        _param_kernel,
        out_shape=[jax.ShapeDtypeStruct((1, width), F32)] + [jax.ShapeDtypeStruct((gp, n_state), F32)] * 4,
        name="params",
    )(hg_lb, rep(s5_a_re[0]), rep(s5_a_im[0]), ldt, b_t(s5_b_re[0]), b_t(s5_b_im[0]))

    gpb = LANES // n_chan
    n_blocks = groups // gpb
    eye = jnp.eye(gpb, dtype=F32)
    blockdiag_in = lambda w: jnp.einsum("mgpn,gh->mgphn", w.reshape(n_blocks, gpb, n_chan, n_state),
                                        eye).reshape(n_blocks, gpb * n_chan, gpb * n_state)
    blockdiag_out = lambda w: jnp.einsum("mgpn,gh->mhngp", w.reshape(n_blocks, gpb, n_chan, n_state),
                                         eye).reshape(n_blocks, gpb * n_state, gpb * n_chan)
    wb = jnp.concatenate([blockdiag_in(bbr), blockdiag_in(bbi)], axis=2).astype(BF16)
    wc = jnp.concatenate([blockdiag_out(s5_c_re[0]), -blockdiag_out(s5_c_im[0])], axis=1).astype(BF16)
    n_lanes = groups * n_state
    a_rows = lambda a: jnp.broadcast_to(a.reshape(groups, n_chan, n_state)[:, 0, :].reshape(1, n_lanes),
                                        (S5_SEQS, n_lanes))
    a_re8, a_im8 = a_rows(abr), a_rows(abi)

    tok = lambda w: pl.BlockSpec((None, tm, w), lambda b, t: (b, t, 0))
    const2 = lambda shape: pl.BlockSpec(shape, lambda b, t: (0,) * len(shape))
    resident = lambda shape: pl.BlockSpec(shape, lambda b, t: (0,) * len(shape), pipeline_mode=pl.Buffered(1))

    m_rows = S5_SEG * S5_SEQS
    c1 = lambda shape: pl.BlockSpec(shape, lambda j: (0,) * len(shape))
    res1 = lambda shape: pl.BlockSpec(shape, lambda j: (0,) * len(shape), pipeline_mode=pl.Buffered(1))
    tile_spec = lambda w: pl.BlockSpec((bsz, 2 * S5_SEG, w), lambda j: (0, j, 0))
    act = lambda dt: jax.ShapeDtypeStruct((bsz, seq_len, width), dt)
    q, fz, v, gz, y_s5 = pl.pallas_call(
        functools.partial(_inproj_s5_kernel, n_blocks=n_blocks),
        grid=(n_tiles,),
        in_specs=[tile_spec(d_model), c1((1, d_model)), res1((d_model, 5 * width)),
                  c1(wb.shape), c1(wc.shape), c1(a_re8.shape), c1(a_im8.shape),
                  c1((1, width)), res1((width, width)), c1((1, width))],
        out_specs=[tile_spec(width)] * 5,
        out_shape=[act(F32)] * 5,
        scratch_shapes=[pltpu.VMEM((d_model, 5 * width), BF16), pltpu.VMEM((width, width), BF16),
                        pltpu.VMEM((width // LANES, m_rows, LANES), F32), pltpu.VMEM((m_rows, 2 * n_lanes), F32),
                        pltpu.VMEM((m_rows, n_lanes), F32), pltpu.VMEM((m_rows, n_lanes), F32),
                        pltpu.VMEM((S5_SEQS, 2 * n_lanes), F32)],
        compiler_params=_cparams("arbitrary"),
        name="inproj_s5",
    )(x, row(in_norm_g[0]), w_in[0], wb, wc, a_re8, a_im8, row(s5_d[0]), s5_w_glu[0], row(s5_b_glu[0]))

    n_t = seq_len // tl
    hg_in = pl.BlockSpec((None, tl, width), lambda b, t: (b, jnp.minimum(t, n_t - 1), 0))
    hg_out = pl.BlockSpec((None, tl, width), lambda b, t: (b, jnp.maximum(t - 1, 0), 0))
    staged = lambda dt: pltpu.VMEM((2, tl, width), dt)
    o_hg = pl.pallas_call(
        functools.partial(_hgrn2_kernel, n_chunks=tl // HG_CHUNK, n_heads=n_heads),
        grid=(bsz, n_t + 1),
        in_specs=[hg_in] * 4 + [const2((1, width))] * 2,
        out_specs=hg_out,
        out_shape=jax.ShapeDtypeStruct((bsz, seq_len, width), F32),
        scratch_shapes=[pltpu.VMEM((n_heads, HG_HEAD_DIM, HG_HEAD_DIM), F32),
                        staged(BF16), staged(BF16), staged(BF16), staged(BF16), staged(F32),
                        pltpu.VMEM((2, tl // HG_CHUNK, SUBLANES, width), F32)],
        compiler_params=_cparams("arbitrary", "arbitrary"),
        name="hgrn2",
    )(q, fz, v, gz, lb, row(hg_norm_g[0]))

    bf = FFN_BLOCK
    assert d_ff % bf == 0
    n_blk = d_ff // bf
    by_block = lambda a: a.reshape(a.shape[0], 2 * n_blk, bf).transpose(1, 0, 2)
    out = pl.pallas_call(
        functools.partial(_ffn_kernel, n_blk=n_blk),
        grid=(bsz, seq_len // tm),
        in_specs=[tok(d_model), tok(width), tok(width), resident((2 * width, d_model)), const2((1, d_model)),
                  resident((2 * n_blk, d_model, bf)), const2((2 * n_blk, CONV_WIDTH, bf)),
                  const2((2 * n_blk, 1, bf)), resident((d_ff, d_model)), const2((1, d_model))],
        out_specs=tok(d_model),
        out_shape=jax.ShapeDtypeStruct((bsz, seq_len, d_model), F32),
        scratch_shapes=[pltpu.VMEM((2 * width, d_model), BF16), pltpu.VMEM((d_ff, d_model), BF16),
                        pltpu.VMEM((2 * n_blk, FFN_ROWS, bf), F32), pltpu.VMEM((tm, d_model), F32),
                        pltpu.VMEM((tm, d_model), BF16), pltpu.VMEM((2, 2, tm + FFN_ROWS, bf), F32),
                        pltpu.VMEM((tm, d_ff), BF16)],
        compiler_params=_cparams("arbitrary", "arbitrary"),
        name="ffn",
    )(x, o_hg, y_s5, w_out[0], row(ffn_norm_g[0]),
      by_block(w_up[0]).astype(BF16), by_block(conv_w[0]), by_block(row(conv_b[0])),
      w_down[0], row(final_norm_g))
    return out
```

```python
import functools

import jax
import jax.numpy as jnp
from jax import lax
from jax.experimental import pallas as pl
from jax.experimental.pallas import tpu as pltpu

EPS = 1e-6
HG_CHUNK = 64
HG_HEAD_DIM = 128
S5_GROUP = 16
S5_STATE = 64
S5_SEG = 64
S5_SEQS = 8
S5_SCAN_BLOCKS = 2
S5_SCAN_UNROLL = True
CONV_WIDTH = 3
FFN_BLOCK = 256
FFN_ROWS = 16
SUBLANES = 8
LANES = 128
VMEM_LIMIT = 60 * 1024 * 1024

F32 = jnp.float32
BF16 = jnp.bfloat16

_NT = (((1,), (1,)), ((), ()))
_TN = (((0,), (0,)), ((), ()))


def _rms(x, g):
    ms = jnp.mean(x * x, axis=-1, keepdims=True)
    return x * lax.rsqrt(ms + EPS) * g


def _cparams(*sem):
    return pltpu.CompilerParams(dimension_semantics=sem, vmem_limit_bytes=VMEM_LIMIT)


def _param_kernel(lb_ref, are_ref, aim_ref, ldt_ref, bre_ref, bim_ref,
                  lb_out, abr_out, abi_out, bbr_out, bbi_out):
    z = lb_ref[...]
    e = jnp.exp(z - jnp.max(z, axis=0, keepdims=True))
    sm = e / jnp.sum(e, axis=0, keepdims=True)
    lb_out[...] = sm[0:1, :]
    dt = jnp.exp(ldt_ref[...])
    ar = are_ref[...]
    ai = aim_ref[...]
    mag = jnp.exp(dt * ar)
    abr = mag * jnp.cos(dt * ai)
    abi = mag * jnp.sin(dt * ai)
    nr = abr - 1.0
    den = ar * ar + ai * ai
    zr = (nr * ar + abi * ai) / den
    zi = (abi * ar - nr * ai) / den
    br = bre_ref[...]
    bi = bim_ref[...]
    abr_out[...] = abr
    abi_out[...] = abi
    bbr_out[...] = zr * br - zi * bi
    bbi_out[...] = zr * bi + zi * br


def _hgrn2_kernel(q_ref, f_ref, v_ref, g_ref, lb_ref, ng_ref, o_ref, st_ref, *, n_chunks, n_heads):
    @pl.when(pl.program_id(1) == 0)
    def _():
        st_ref[...] = jnp.zeros_like(st_ref)

    lb = lb_ref[...]
    ng = ng_ref[...]
    c = HG_CHUNK
    tri = lax.broadcasted_iota(jnp.int32, (c, c), 0) >= lax.broadcasted_iota(jnp.int32, (c, c), 1)
    tri_b = tri.astype(BF16)

    f = lb + (1.0 - lb) * jax.nn.sigmoid(f_ref[...])
    k = 1.0 - f
    logf = jnp.log(f)
    hi = logf.astype(BF16)
    r1 = logf - hi.astype(F32)
    mid = r1.astype(BF16)
    lo = (r1 - mid.astype(F32)).astype(BF16)
    bs = []
    for ci in range(n_chunks):
        rows = slice(ci * c, (ci + 1) * c)
        bs.append(jnp.dot(tri_b, hi[rows], preferred_element_type=F32)
                  + (jnp.dot(tri_b, mid[rows], preferred_element_type=F32)
                     + jnp.dot(tri_b, lo[rows], preferred_element_type=F32)))
    b = jnp.concatenate(bs, axis=0)
    kd_f = k * jnp.exp(-b)
    qd = (q_ref[...] * jnp.exp(b)).astype(BF16)
    kd = kd_f.astype(BF16)
    vb = v_ref[...].astype(BF16)
    gz = g_ref[...]
    gate = ng * (gz * jax.nn.sigmoid(gz))
    decs = [jnp.exp(b[(ci + 1) * c - 1:(ci + 1) * c, :]) for ci in range(n_chunks)]
    kts = [(kd_f[ci * c:(ci + 1) * c] * decs[ci]).astype(BF16) for ci in range(n_chunks)]

    for hd in range(n_heads):
        cs = slice(hd * HG_HEAD_DIM, (hd + 1) * HG_HEAD_DIM)
        st = st_ref[hd]
        for ci in range(n_chunks):
            rows = slice(ci * c, (ci + 1) * c)
            att = lax.dot_general(qd[rows, cs], kd[rows, cs], _NT, preferred_element_type=F32)
            att = jnp.where(tri, att, 0.0)
            o = jnp.dot(att.astype(BF16), vb[rows, cs], preferred_element_type=F32)
            o = o + lax.dot_general(qd[rows, cs], st.astype(BF16), _NT, preferred_element_type=F32)
            ds = lax.dot_general(vb[rows, cs], kts[ci][:, cs], _TN, preferred_element_type=F32)
            st = st * decs[ci][:, cs] + ds
            o = o * lax.rsqrt(jnp.mean(o * o, axis=-1, keepdims=True) + EPS)
            o_ref[rows, cs] = o * gate[rows, cs]
        st_ref[hd] = st


def _inproj_s5_kernel(x_ref, gin_ref, win32_ref, wb_ref, wc_ref, are_ref, aim_ref, d_ref, wglu32_ref, bglu_ref,
                      q_ref, f_ref, v_ref, gz_ref, y_ref,
                      win_ref, wglu_ref, il_ref, bu_ref, pre_ref, pim_ref, carry_ref, *, n_blocks):
    h = S5_SEG
    s = S5_SEQS
    half = wb_ref.shape[2] // 2
    kw = wb_ref.shape[1]
    bsz, tt, d_model = x_ref.shape
    width = y_ref.shape[-1]

    @pl.when(pl.program_id(0) == 0)
    def _():
        win_ref[...] = win32_ref[...].astype(BF16)
        wglu_ref[...] = wglu32_ref[...].astype(BF16)
        carry_ref[...] = jnp.zeros_like(carry_ref)
        ar = are_ref[...]
        ai = aim_ref[...]
        pre_ref[0:s, :] = ar
        pim_ref[0:s, :] = ai

        def powers(i, c):
            pr, pi = c
            npr = pr * ar - pi * ai
            npi = pr * ai + pi * ar
            r = pl.ds(pl.multiple_of(i * s, s), s)
            pre_ref[r, :] = npr
            pim_ref[r, :] = npi
            return npr, npi

        lax.fori_loop(1, h, powers, (ar, ai))

    xb = _rms(x_ref[...].reshape(bsz * tt, d_model), gin_ref[...]).astype(BF16)
    u_nat = jnp.dot(xb, win_ref[:, 4 * width:5 * width], preferred_element_type=F32)

    n_slabs = il_ref.shape[0]
    for bi in range(bsz):
        for k in range(2):
            r0 = bi * tt + k * h
            for j in range(n_slabs):
                il_ref[j, pl.ds(2 * bi + k, h, stride=s), :] = u_nat[r0:r0 + h, j * LANES:(j + 1) * LANES]
    u = jnp.concatenate([il_ref[j] for j in range(n_slabs)], axis=1)
    ub = u.astype(BF16)
    for m in range(n_blocks):
        bu_ref[:, 2 * half * m:2 * half * (m + 1)] = jnp.dot(
            ub[:, kw * m:kw * (m + 1)], wb_ref[m], preferred_element_type=F32)

    for i, o in enumerate((q_ref, f_ref, v_ref, gz_ref)):
        o[...] = jnp.dot(xb, win_ref[:, i * width:(i + 1) * width],
                         preferred_element_type=F32).reshape(bsz, tt, width).astype(o.dtype)

    second = lax.broadcasted_iota(jnp.int32, (s, half), 0) % 2 == 1
    last = slice((h - 1) * s, h * s)

    for m0 in range(0, n_blocks, S5_SCAN_BLOCKS):
        blocks = []
        for m in range(m0, m0 + S5_SCAN_BLOCKS):
            lo = 2 * half * m
            blocks.append((slice(lo, lo + half), slice(lo + half, lo + 2 * half),
                           slice(half * m, half * (m + 1))))
        coef = [(are_ref[:, am], aim_ref[:, am]) for _, _, am in blocks]

        def scan(i, c, blocks=blocks, coef=coef):
            r = pl.ds(pl.multiple_of(i * s, s), s)
            out = []
            for (re, im, _), (ar, ai), (xr, xi) in zip(blocks, coef, c):
                nxr = ar * xr - ai * xi + bu_ref[r, re]
                nxi = ar * xi + ai * xr + bu_ref[r, im]
                bu_ref[r, re] = nxr
                bu_ref[r, im] = nxi
                out.append((nxr, nxi))
            return tuple(out)

        init = tuple((carry_ref[:, re], carry_ref[:, im]) for re, im, _ in blocks)
        ends = lax.fori_loop(0, h, scan, init, unroll=S5_SCAN_UNROLL)
        starts = [(jnp.where(second, pltpu.roll(er, 1, axis=0), 0.0),
                   jnp.where(second, pltpu.roll(ei, 1, axis=0), 0.0)) for er, ei in ends]

        def fix(i, c, blocks=blocks, starts=starts):
            r = pl.ds(pl.multiple_of(i * s, s), s)
            for (re, im, am), (cr, ci) in zip(blocks, starts):
                pr = pre_ref[r, am]
                pi = pim_ref[r, am]
                bu_ref[r, re] = bu_ref[r, re] + (pr * cr - pi * ci)
                bu_ref[r, im] = bu_ref[r, im] + (pr * ci + pi * cr)
            return c

        lax.fori_loop(0, h, fix, 0, unroll=S5_SCAN_UNROLL)
        for re, im, _ in blocks:
            carry_ref[:, re] = jnp.where(second, 0.0, pltpu.roll(bu_ref[last, re], s - 1, axis=0))
            carry_ref[:, im] = jnp.where(second, 0.0, pltpu.roll(bu_ref[last, im], s - 1, axis=0))

    ys = [jnp.dot(bu_ref[:, 2 * half * m:2 * half * (m + 1)].astype(BF16), wc_ref[m],
                  preferred_element_type=F32) for m in range(n_blocks)]
    y = jnp.concatenate(ys, axis=1) + d_ref[...] * u
    y = jax.nn.gelu(y)
    z = jnp.dot(y.astype(BF16), wglu_ref[...], preferred_element_type=F32) + bglu_ref[...]
    y = y * jax.nn.sigmoid(z)
    for j in range(n_slabs):
        il_ref[j] = y[:, j * LANES:(j + 1) * LANES]
    for bi in range(y_ref.shape[0]):
        for k in range(2):
            for j in range(n_slabs):
                y_ref[bi, k * h:(k + 1) * h, j * LANES:(j + 1) * LANES] = il_ref[
                    j, pl.ds(2 * bi + k, h, stride=s), :]


def _ffn_kernel(x_ref, ohg_ref, y_ref, wout32_ref, gf_ref, wup_ref, cw_ref, cb_ref, wdn32_ref, g_ref, o_ref,
                wout_ref, wdn_ref, tail_ref, h_ref, hn_ref, pre_ref, act_ref, *, n_blk):
    @pl.when((pl.program_id(0) == 0) & (pl.program_id(1) == 0))
    def _():
        wout_ref[...] = wout32_ref[...].astype(BF16)
        wdn_ref[...] = wdn32_ref[...].astype(BF16)

    @pl.when(pl.program_id(1) == 0)
    def _():
        tail_ref[...] = jnp.zeros_like(tail_ref)

    width = ohg_ref.shape[-1]
    mix = jnp.dot(ohg_ref[...].astype(BF16), wout_ref[0:width, :], preferred_element_type=F32)
    mix = mix + jnp.dot(y_ref[...].astype(BF16), wout_ref[width:2 * width, :], preferred_element_type=F32)
    hres = x_ref[...] + mix
    h_ref[...] = hres
    hn_ref[...] = _rms(hres, gf_ref[...]).astype(BF16)

    tm = act_ref.shape[0]
    bf = pre_ref.shape[-1]
    rows = FFN_ROWS
    top = lax.broadcasted_iota(jnp.int32, (SUBLANES, bf), 0)

    cols = lambda j, part: slice((part * n_blk + j) * bf, (part * n_blk + j + 1) * bf)

    def up(j, slot):
        for part in range(2):
            pre_ref[slot, part, rows:rows + tm] = jnp.dot(hn_ref[...], wup_ref[:, cols(j, part)],
                                                          preferred_element_type=F32)

    def mid(j, slot):
        taps, prev = [], []
        for part in range(2):
            blk = cols(j, part)
            pre_ref[slot, part, 0:rows] = tail_ref[:, blk]
            tail_ref[:, blk] = pre_ref[slot, part, tm:tm + rows]
            cw = cw_ref[:, blk]
            taps.append((cb_ref[:, blk], cw[0:1, :], cw[1:2, :], cw[2:3, :]))
            p = pre_ref[slot, part, 0:rows]
            prev.append((pltpu.roll(p, 1, axis=0), pltpu.roll(p, 2, axis=0)))
        for g in range(tm // rows):
            r0 = rows * (g + 1)
            outs = []
            for part in range(2):
                cur = pre_ref[slot, part, r0:r0 + rows]
                rc = (pltpu.roll(cur, 1, axis=0), pltpu.roll(cur, 2, axis=0))
                sh = [jnp.concatenate([jnp.where(top < d + 1, prev[part][d][0:SUBLANES], rc[d][0:SUBLANES]),
                                       rc[d][SUBLANES:]], axis=0) for d in range(2)]
                prev[part] = rc
                cb, w0, w1, w2 = taps[part]
                outs.append(cb + w0 * sh[1] + w1 * sh[0] + w2 * cur)
            gate, val = outs
            act_ref[rows * g:rows * (g + 1), j * bf:(j + 1) * bf] = (
                (gate * jax.nn.sigmoid(gate)) * val).astype(BF16)

    up(0, 0)
    for j in range(n_blk):
        if j + 1 < n_blk:
            up(j + 1, (j + 1) % 2)
        mid(j, j % 2)
    out = h_ref[...] + jnp.dot(act_ref[...], wdn_ref[...], preferred_element_type=F32)
    o_ref[...] = _rms(out, g_ref[...])


def _tile_sizes(seq_len):
    tm = min(512, seq_len)
    tl = min(256, seq_len)
    assert seq_len % tm == 0 and seq_len % tl == 0 and tm % (2 * S5_SEG) == 0 and tl % HG_CHUNK == 0
    return tm, tl


def kernel(x, in_norm_g, w_in, hg_lb, hg_norm_g, s5_a_re, s5_a_im, s5_log_dt, s5_b_re, s5_b_im,
           s5_c_re, s5_c_im, s5_d, s5_w_glu, s5_b_glu, w_out, ffn_norm_g, w_up, conv_w, conv_b,
           w_down, final_norm_g):
    bsz, seq_len, d_model = x.shape
    depth = w_in.shape[0]
    assert depth == 1, "single-layer block"
    width = hg_norm_g.shape[1]
    assert w_in.shape[2] == 5 * width and s5_d.shape[1] == width
    n_heads = width // HG_HEAD_DIM
    groups, n_state, n_chan = s5_b_re.shape[1:]
    assert (n_state, n_chan) == (S5_STATE, S5_GROUP) and groups * n_chan == width
    assert 2 * bsz == S5_SEQS
    d_ff = w_down.shape[1]
    tm, tl = _tile_sizes(seq_len)
    n_tiles = seq_len // (2 * S5_SEG)
    row = lambda a: a.reshape(1, -1)

    gp = groups * n_chan
    rep = lambda a: jnp.repeat(a, n_chan, axis=0)
    ldt = jnp.broadcast_to(jnp.repeat(s5_log_dt[0], n_chan)[:, None], (gp, n_state))
    b_t = lambda a: a.transpose(0, 2, 1).reshape(gp, n_state)
    lb, abr, abi, bbr, bbi = pl.pallas_call(
        _param_kernel,
        out_shape=[jax.ShapeDtypeStruct((1, width), F32)] + [jax.ShapeDtypeStruct((gp, n_state), F32)] * 4,
        name="params",
    )(hg_lb, rep(s5_a_re[0]), rep(s5_a_im[0]), ldt, b_t(s5_b_re[0]), b_t(s5_b_im[0]))

    gpb = LANES // n_chan
    n_blocks = groups // gpb
    eye = jnp.eye(gpb, dtype=F32)
    blockdiag_in = lambda w: jnp.einsum("mgpn,gh->mgphn", w.reshape(n_blocks, gpb, n_chan, n_state),
                                        eye).reshape(n_blocks, gpb * n_chan, gpb * n_state)
    blockdiag_out = lambda w: jnp.einsum("mgpn,gh->mhngp", w.reshape(n_blocks, gpb, n_chan, n_state),
                                         eye).reshape(n_blocks, gpb * n_state, gpb * n_chan)
    wb = jnp.concatenate([blockdiag_in(bbr), blockdiag_in(bbi)], axis=2).astype(BF16)
    wc = jnp.concatenate([blockdiag_out(s5_c_re[0]), -blockdiag_out(s5_c_im[0])], axis=1).astype(BF16)
    n_lanes = groups * n_state
    a_rows = lambda a: jnp.broadcast_to(a.reshape(groups, n_chan, n_state)[:, 0, :].reshape(1, n_lanes),
                                        (S5_SEQS, n_lanes))
    a_re8, a_im8 = a_rows(abr), a_rows(abi)

    tok = lambda w: pl.BlockSpec((None, tm, w), lambda b, t: (b, t, 0))
    const2 = lambda shape: pl.BlockSpec(shape, lambda b, t: (0,) * len(shape))
    resident = lambda shape: pl.BlockSpec(shape, lambda b, t: (0,) * len(shape), pipeline_mode=pl.Buffered(1))

    m_rows = S5_SEG * S5_SEQS
    c1 = lambda shape: pl.BlockSpec(shape, lambda j: (0,) * len(shape))
    res1 = lambda shape: pl.BlockSpec(shape, lambda j: (0,) * len(shape), pipeline_mode=pl.Buffered(1))
    tile_spec = lambda w: pl.BlockSpec((bsz, 2 * S5_SEG, w), lambda j: (0, j, 0))
    act = lambda dt: jax.ShapeDtypeStruct((bsz, seq_len, width), dt)
    q, fz, v, gz, y_s5 = pl.pallas_call(
        functools.partial(_inproj_s5_kernel, n_blocks=n_blocks),
        grid=(n_tiles,),
        in_specs=[tile_spec(d_model), c1((1, d_model)), res1((d_model, 5 * width)),
                  c1(wb.shape), c1(wc.shape), c1(a_re8.shape), c1(a_im8.shape),
                  c1((1, width)), res1((width, width)), c1((1, width))],
        out_specs=[tile_spec(width)] * 5,
        out_shape=[act(F32)] * 5,
        scratch_shapes=[pltpu.VMEM((d_model, 5 * width), BF16), pltpu.VMEM((width, width), BF16),
                        pltpu.VMEM((width // LANES, m_rows, LANES), F32), pltpu.VMEM((m_rows, 2 * n_lanes), F32),
                        pltpu.VMEM((m_rows, n_lanes), F32), pltpu.VMEM((m_rows, n_lanes), F32),
                        pltpu.VMEM((S5_SEQS, 2 * n_lanes), F32)],
        compiler_params=_cparams("arbitrary"),
        name="inproj_s5",
    )(x, row(in_norm_g[0]), w_in[0], wb, wc, a_re8, a_im8, row(s5_d[0]), s5_w_glu[0], row(s5_b_glu[0]))

    ttok = pl.BlockSpec((None, tl, width), lambda b, t: (b, t, 0))
    o_hg = pl.pallas_call(
        functools.partial(_hgrn2_kernel, n_chunks=tl // HG_CHUNK, n_heads=n_heads),
        grid=(bsz, seq_len // tl),
        in_specs=[ttok] * 4 + [const2((1, width))] * 2,
        out_specs=ttok,
        out_shape=jax.ShapeDtypeStruct((bsz, seq_len, width), F32),
        scratch_shapes=[pltpu.VMEM((n_heads, HG_HEAD_DIM, HG_HEAD_DIM), F32)],
        compiler_params=_cparams("arbitrary", "arbitrary"),
        name="hgrn2",
    )(q, fz, v, gz, lb, row(hg_norm_g[0]))

    bf = FFN_BLOCK
    assert d_ff % bf == 0
    n_blk = d_ff // bf
    out = pl.pallas_call(
        functools.partial(_ffn_kernel, n_blk=n_blk),
        grid=(bsz, seq_len // tm),
        in_specs=[tok(d_model), tok(width), tok(width), resident((2 * width, d_model)), const2((1, d_model)),
                  resident((d_model, 2 * d_ff)), const2((CONV_WIDTH, 2 * d_ff)),
                  const2((1, 2 * d_ff)), resident((d_ff, d_model)), const2((1, d_model))],
        out_specs=tok(d_model),
        out_shape=jax.ShapeDtypeStruct((bsz, seq_len, d_model), F32),
        scratch_shapes=[pltpu.VMEM((2 * width, d_model), BF16), pltpu.VMEM((d_ff, d_model), BF16),
                        pltpu.VMEM((FFN_ROWS, 2 * d_ff), F32), pltpu.VMEM((tm, d_model), F32),
                        pltpu.VMEM((tm, d_model), BF16), pltpu.VMEM((2, 2, tm + FFN_ROWS, bf), F32),
                        pltpu.VMEM((tm, d_ff), BF16)],
        compiler_params=_cparams("arbitrary", "arbitrary"),
        name="ffn",
    )(x, o_hg, y_s5, w_out[0], row(ffn_norm_g[0]), w_up[0].astype(BF16), conv_w[0], row(conv_b[0]),
      w_down[0], row(final_norm_g))
    return out
```

```python
import functools

import jax
import jax.numpy as jnp
from jax import lax
from jax.experimental import pallas as pl
from jax.experimental.pallas import tpu as pltpu

EPS = 1e-6
HG_CHUNK = 64
HG_HEAD_DIM = 128
S5_GROUP = 16
S5_STATE = 64
S5_SEG = 64
S5_SEQS = 8
S5_SCAN_BLOCKS = 2
S5_SCAN_UNROLL = True
CONV_WIDTH = 3
FFN_BLOCK = 256
FFN_ROWS = 16
SUBLANES = 8
LANES = 128
VMEM_LIMIT = 60 * 1024 * 1024

F32 = jnp.float32
BF16 = jnp.bfloat16

_NT = (((1,), (1,)), ((), ()))
_TN = (((0,), (0,)), ((), ()))


def _rms(x, g):
    ms = jnp.mean(x * x, axis=-1, keepdims=True)
    return x * lax.rsqrt(ms + EPS) * g


def _cparams(*sem):
    return pltpu.CompilerParams(dimension_semantics=sem, vmem_limit_bytes=VMEM_LIMIT)


def _param_kernel(lb_ref, are_ref, aim_ref, ldt_ref, bre_ref, bim_ref,
                  lb_out, abr_out, abi_out, bbr_out, bbi_out):
    z = lb_ref[...]
    e = jnp.exp(z - jnp.max(z, axis=0, keepdims=True))
    sm = e / jnp.sum(e, axis=0, keepdims=True)
    lb_out[...] = sm[0:1, :]
    dt = jnp.exp(ldt_ref[...])
    ar = are_ref[...]
    ai = aim_ref[...]
    mag = jnp.exp(dt * ar)
    abr = mag * jnp.cos(dt * ai)
    abi = mag * jnp.sin(dt * ai)
    nr = abr - 1.0
    den = ar * ar + ai * ai
    zr = (nr * ar + abi * ai) / den
    zi = (abi * ar - nr * ai) / den
    br = bre_ref[...]
    bi = bim_ref[...]
    abr_out[...] = abr
    abi_out[...] = abi
    bbr_out[...] = zr * br - zi * bi
    bbi_out[...] = zr * bi + zi * br


def _hgrn2_rows(q, fz, v, gz, lb, ng, st_ref, o_ref):
    c = HG_CHUNK
    n_chunks = q.shape[0] // c
    n_heads = st_ref.shape[0]
    tri = lax.broadcasted_iota(jnp.int32, (c, c), 0) >= lax.broadcasted_iota(jnp.int32, (c, c), 1)
    tri_b = tri.astype(BF16)

    f = lb + (1.0 - lb) * jax.nn.sigmoid(fz)
    k = 1.0 - f
    logf = jnp.log(f)
    hi = logf.astype(BF16)
    r1 = logf - hi.astype(F32)
    mid = r1.astype(BF16)
    lo = (r1 - mid.astype(F32)).astype(BF16)
    bs = []
    for ci in range(n_chunks):
        rows = slice(ci * c, (ci + 1) * c)
        bs.append(jnp.dot(tri_b, hi[rows], preferred_element_type=F32)
                  + (jnp.dot(tri_b, mid[rows], preferred_element_type=F32)
                     + jnp.dot(tri_b, lo[rows], preferred_element_type=F32)))
    b = jnp.concatenate(bs, axis=0)
    kd_f = k * jnp.exp(-b)
    qd = (q * jnp.exp(b)).astype(BF16)
    kd = kd_f.astype(BF16)
    vb = v.astype(BF16)
    gate = ng * (gz * jax.nn.sigmoid(gz))
    decs = [jnp.exp(b[(ci + 1) * c - 1:(ci + 1) * c, :]) for ci in range(n_chunks)]
    kts = [(kd_f[ci * c:(ci + 1) * c] * decs[ci]).astype(BF16) for ci in range(n_chunks)]

    for hd in range(n_heads):
        cs = slice(hd * HG_HEAD_DIM, (hd + 1) * HG_HEAD_DIM)
        st = st_ref[hd]
        for ci in range(n_chunks):
            rows = slice(ci * c, (ci + 1) * c)
            att = lax.dot_general(qd[rows, cs], kd[rows, cs], _NT, preferred_element_type=F32)
            att = jnp.where(tri, att, 0.0)
            o = jnp.dot(att.astype(BF16), vb[rows, cs], preferred_element_type=F32)
            o = o + lax.dot_general(qd[rows, cs], st.astype(BF16), _NT, preferred_element_type=F32)
            ds = lax.dot_general(vb[rows, cs], kts[ci][:, cs], _TN, preferred_element_type=F32)
            st = st * decs[ci][:, cs] + ds
            o = o * lax.rsqrt(jnp.mean(o * o, axis=-1, keepdims=True) + EPS)
            o_ref[rows, cs] = o * gate[rows, cs]
        st_ref[hd] = st


def _mixer_kernel(x_ref, gin_ref, win32_ref, wb_ref, wc_ref, are_ref, aim_ref, d_ref, wglu32_ref, bglu_ref,
                  lb_ref, ng_ref, ohg_ref, y_ref,
                  win_ref, wglu_ref, il_ref, bu_ref, pre_ref, pim_ref, carry_ref, hst_ref, *, n_blocks):
    h = S5_SEG
    s = S5_SEQS
    half = wb_ref.shape[2] // 2
    kw = wb_ref.shape[1]
    bsz, tt, d_model = x_ref.shape
    width = y_ref.shape[-1]

    @pl.when(pl.program_id(0) == 0)
    def _():
        win_ref[...] = win32_ref[...].astype(BF16)
        wglu_ref[...] = wglu32_ref[...].astype(BF16)
        carry_ref[...] = jnp.zeros_like(carry_ref)
        hst_ref[...] = jnp.zeros_like(hst_ref)
        ar = are_ref[...]
        ai = aim_ref[...]
        pre_ref[0:s, :] = ar
        pim_ref[0:s, :] = ai

        def powers(i, c):
            pr, pi = c
            npr = pr * ar - pi * ai
            npi = pr * ai + pi * ar
            r = pl.ds(pl.multiple_of(i * s, s), s)
            pre_ref[r, :] = npr
            pim_ref[r, :] = npi
            return npr, npi

        lax.fori_loop(1, h, powers, (ar, ai))

    xb = _rms(x_ref[...].reshape(bsz * tt, d_model), gin_ref[...]).astype(BF16)
    u_nat = jnp.dot(xb, win_ref[:, 4 * width:5 * width], preferred_element_type=F32)

    n_slabs = il_ref.shape[0]
    for bi in range(bsz):
        for k in range(2):
            r0 = bi * tt + k * h
            for j in range(n_slabs):
                il_ref[j, pl.ds(2 * bi + k, h, stride=s), :] = u_nat[r0:r0 + h, j * LANES:(j + 1) * LANES]
    u = jnp.concatenate([il_ref[j] for j in range(n_slabs)], axis=1)
    ub = u.astype(BF16)
    for m in range(n_blocks):
        bu_ref[:, 2 * half * m:2 * half * (m + 1)] = jnp.dot(
            ub[:, kw * m:kw * (m + 1)], wb_ref[m], preferred_element_type=F32)

    q, fz, v, gz = (jnp.dot(xb, win_ref[:, i * width:(i + 1) * width], preferred_element_type=F32)
                    for i in range(4))
    for bi in range(bsz):
        r = slice(bi * tt, (bi + 1) * tt)
        _hgrn2_rows(q[r], fz[r], v[r], gz[r], lb_ref[...], ng_ref[...], hst_ref.at[bi], ohg_ref.at[bi])

    second = lax.broadcasted_iota(jnp.int32, (s, half), 0) % 2 == 1
    last = slice((h - 1) * s, h * s)

    for m0 in range(0, n_blocks, S5_SCAN_BLOCKS):
        blocks = []
        for m in range(m0, m0 + S5_SCAN_BLOCKS):
            lo = 2 * half * m
            blocks.append((slice(lo, lo + half), slice(lo + half, lo + 2 * half),
                           slice(half * m, half * (m + 1))))
        coef = [(are_ref[:, am], aim_ref[:, am]) for _, _, am in blocks]

        def scan(i, c, blocks=blocks, coef=coef):
            r = pl.ds(pl.multiple_of(i * s, s), s)
            out = []
            for (re, im, _), (ar, ai), (xr, xi) in zip(blocks, coef, c):
                nxr = ar * xr - ai * xi + bu_ref[r, re]
                nxi = ar * xi + ai * xr + bu_ref[r, im]
                bu_ref[r, re] = nxr
                bu_ref[r, im] = nxi
                out.append((nxr, nxi))
            return tuple(out)

        init = tuple((carry_ref[:, re], carry_ref[:, im]) for re, im, _ in blocks)
        ends = lax.fori_loop(0, h, scan, init, unroll=S5_SCAN_UNROLL)
        starts = [(jnp.where(second, pltpu.roll(er, 1, axis=0), 0.0),
                   jnp.where(second, pltpu.roll(ei, 1, axis=0), 0.0)) for er, ei in ends]

        def fix(i, c, blocks=blocks, starts=starts):
            r = pl.ds(pl.multiple_of(i * s, s), s)
            for (re, im, am), (cr, ci) in zip(blocks, starts):
                pr = pre_ref[r, am]
                pi = pim_ref[r, am]
                bu_ref[r, re] = bu_ref[r, re] + (pr * cr - pi * ci)
                bu_ref[r, im] = bu_ref[r, im] + (pr * ci + pi * cr)
            return c

        lax.fori_loop(0, h, fix, 0, unroll=S5_SCAN_UNROLL)
        for re, im, _ in blocks:
            carry_ref[:, re] = jnp.where(second, 0.0, pltpu.roll(bu_ref[last, re], s - 1, axis=0))
            carry_ref[:, im] = jnp.where(second, 0.0, pltpu.roll(bu_ref[last, im], s - 1, axis=0))

    ys = [jnp.dot(bu_ref[:, 2 * half * m:2 * half * (m + 1)].astype(BF16), wc_ref[m],
                  preferred_element_type=F32) for m in range(n_blocks)]
    y = jnp.concatenate(ys, axis=1) + d_ref[...] * u
    y = jax.nn.gelu(y)
    z = jnp.dot(y.astype(BF16), wglu_ref[...], preferred_element_type=F32) + bglu_ref[...]
    y = y * jax.nn.sigmoid(z)
    for j in range(n_slabs):
        il_ref[j] = y[:, j * LANES:(j + 1) * LANES]
    for bi in range(y_ref.shape[0]):
        for k in range(2):
            for j in range(n_slabs):
                y_ref[bi, k * h:(k + 1) * h, j * LANES:(j + 1) * LANES] = il_ref[
                    j, pl.ds(2 * bi + k, h, stride=s), :]


def _ffn_kernel(x_ref, ohg_ref, y_ref, wout32_ref, gf_ref, wup_ref, cw_ref, cb_ref, wdn32_ref, g_ref, o_ref,
                wout_ref, wdn_ref, tail_ref, h_ref, hn_ref, pre_ref, act_ref, *, n_blk):
    @pl.when((pl.program_id(0) == 0) & (pl.program_id(1) == 0))
    def _():
        wout_ref[...] = wout32_ref[...].astype(BF16)
        wdn_ref[...] = wdn32_ref[...].astype(BF16)

    @pl.when(pl.program_id(1) == 0)
    def _():
        tail_ref[...] = jnp.zeros_like(tail_ref)

    width = ohg_ref.shape[-1]
    mix = jnp.dot(ohg_ref[...].astype(BF16), wout_ref[0:width, :], preferred_element_type=F32)
    mix = mix + jnp.dot(y_ref[...].astype(BF16), wout_ref[width:2 * width, :], preferred_element_type=F32)
    hres = x_ref[...] + mix
    h_ref[...] = hres
    hn_ref[...] = _rms(hres, gf_ref[...]).astype(BF16)

    tm = act_ref.shape[0]
    bf = pre_ref.shape[-1]
    rows = FFN_ROWS
    top = lax.broadcasted_iota(jnp.int32, (SUBLANES, bf), 0)

    cols = lambda j, part: slice((part * n_blk + j) * bf, (part * n_blk + j + 1) * bf)

    def up(j, slot):
        for part in range(2):
            pre_ref[slot, part, rows:rows + tm] = jnp.dot(hn_ref[...], wup_ref[:, cols(j, part)],
                                                          preferred_element_type=F32)

    def mid(j, slot):
        taps, prev = [], []
        for part in range(2):
            blk = cols(j, part)
            pre_ref[slot, part, 0:rows] = tail_ref[:, blk]
            tail_ref[:, blk] = pre_ref[slot, part, tm:tm + rows]
            cw = cw_ref[:, blk]
            taps.append((cb_ref[:, blk], cw[0:1, :], cw[1:2, :], cw[2:3, :]))
            p = pre_ref[slot, part, 0:rows]
            prev.append((pltpu.roll(p, 1, axis=0), pltpu.roll(p, 2, axis=0)))
        for g in range(tm // rows):
            r0 = rows * (g + 1)
            outs = []
            for part in range(2):
                cur = pre_ref[slot, part, r0:r0 + rows]
                rc = (pltpu.roll(cur, 1, axis=0), pltpu.roll(cur, 2, axis=0))
                sh = [jnp.concatenate([jnp.where(top < d + 1, prev[part][d][0:SUBLANES], rc[d][0:SUBLANES]),
                                       rc[d][SUBLANES:]], axis=0) for d in range(2)]
                prev[part] = rc
                cb, w0, w1, w2 = taps[part]
                outs.append(cb + w0 * sh[1] + w1 * sh[0] + w2 * cur)
            gate, val = outs
            act_ref[rows * g:rows * (g + 1), j * bf:(j + 1) * bf] = (
                (gate * jax.nn.sigmoid(gate)) * val).astype(BF16)

    up(0, 0)
    for j in range(n_blk):
        if j + 1 < n_blk:
            up(j + 1, (j + 1) % 2)
        mid(j, j % 2)
    out = h_ref[...] + jnp.dot(act_ref[...], wdn_ref[...], preferred_element_type=F32)
    o_ref[...] = _rms(out, g_ref[...])


def _tile_sizes(seq_len):
    tm = min(512, seq_len)
    tl = min(256, seq_len)
    assert seq_len % tm == 0 and seq_len % tl == 0 and tm % (2 * S5_SEG) == 0 and tl % HG_CHUNK == 0
    return tm, tl


def kernel(x, in_norm_g, w_in, hg_lb, hg_norm_g, s5_a_re, s5_a_im, s5_log_dt, s5_b_re, s5_b_im,
           s5_c_re, s5_c_im, s5_d, s5_w_glu, s5_b_glu, w_out, ffn_norm_g, w_up, conv_w, conv_b,
           w_down, final_norm_g):
    bsz, seq_len, d_model = x.shape
    depth = w_in.shape[0]
    assert depth == 1, "single-layer block"
    width = hg_norm_g.shape[1]
    assert w_in.shape[2] == 5 * width and s5_d.shape[1] == width
    n_heads = width // HG_HEAD_DIM
    groups, n_state, n_chan = s5_b_re.shape[1:]
    assert (n_state, n_chan) == (S5_STATE, S5_GROUP) and groups * n_chan == width
    assert 2 * bsz == S5_SEQS
    d_ff = w_down.shape[1]
    tm, tl = _tile_sizes(seq_len)
    n_tiles = seq_len // (2 * S5_SEG)
    row = lambda a: a.reshape(1, -1)

    gp = groups * n_chan
    rep = lambda a: jnp.repeat(a, n_chan, axis=0)
    ldt = jnp.broadcast_to(jnp.repeat(s5_log_dt[0], n_chan)[:, None], (gp, n_state))
    b_t = lambda a: a.transpose(0, 2, 1).reshape(gp, n_state)
    lb, abr, abi, bbr, bbi = pl.pallas_call(
        _param_kernel,
        out_shape=[jax.ShapeDtypeStruct((1, width), F32)] + [jax.ShapeDtypeStruct((gp, n_state), F32)] * 4,
        name="params",
    )(hg_lb, rep(s5_a_re[0]), rep(s5_a_im[0]), ldt, b_t(s5_b_re[0]), b_t(s5_b_im[0]))

    gpb = LANES // n_chan
    n_blocks = groups // gpb
    eye = jnp.eye(gpb, dtype=F32)
    blockdiag_in = lambda w: jnp.einsum("mgpn,gh->mgphn", w.reshape(n_blocks, gpb, n_chan, n_state),
                                        eye).reshape(n_blocks, gpb * n_chan, gpb * n_state)
    blockdiag_out = lambda w: jnp.einsum("mgpn,gh->mhngp", w.reshape(n_blocks, gpb, n_chan, n_state),
                                         eye).reshape(n_blocks, gpb * n_state, gpb * n_chan)
    wb = jnp.concatenate([blockdiag_in(bbr), blockdiag_in(bbi)], axis=2).astype(BF16)
    wc = jnp.concatenate([blockdiag_out(s5_c_re[0]), -blockdiag_out(s5_c_im[0])], axis=1).astype(BF16)
    n_lanes = groups * n_state
    a_rows = lambda a: jnp.broadcast_to(a.reshape(groups, n_chan, n_state)[:, 0, :].reshape(1, n_lanes),
                                        (S5_SEQS, n_lanes))
    a_re8, a_im8 = a_rows(abr), a_rows(abi)

    tok = lambda w: pl.BlockSpec((None, tm, w), lambda b, t: (b, t, 0))
    const2 = lambda shape: pl.BlockSpec(shape, lambda b, t: (0,) * len(shape))
    resident = lambda shape: pl.BlockSpec(shape, lambda b, t: (0,) * len(shape), pipeline_mode=pl.Buffered(1))

    m_rows = S5_SEG * S5_SEQS
    c1 = lambda shape: pl.BlockSpec(shape, lambda j: (0,) * len(shape))
    res1 = lambda shape: pl.BlockSpec(shape, lambda j: (0,) * len(shape), pipeline_mode=pl.Buffered(1))
    tile_spec = lambda w: pl.BlockSpec((bsz, 2 * S5_SEG, w), lambda j: (0, j, 0))
    act = lambda dt: jax.ShapeDtypeStruct((bsz, seq_len, width), dt)
    o_hg, y_s5 = pl.pallas_call(
        functools.partial(_mixer_kernel, n_blocks=n_blocks),
        grid=(n_tiles,),
        in_specs=[tile_spec(d_model), c1((1, d_model)), res1((d_model, 5 * width)),
                  c1(wb.shape), c1(wc.shape), c1(a_re8.shape), c1(a_im8.shape),
                  c1((1, width)), res1((width, width)), c1((1, width)), c1((1, width)), c1((1, width))],
        out_specs=[tile_spec(width)] * 2,
        out_shape=[act(F32)] * 2,
        scratch_shapes=[pltpu.VMEM((d_model, 5 * width), BF16), pltpu.VMEM((width, width), BF16),
                        pltpu.VMEM((width // LANES, m_rows, LANES), F32), pltpu.VMEM((m_rows, 2 * n_lanes), F32),
                        pltpu.VMEM((m_rows, n_lanes), F32), pltpu.VMEM((m_rows, n_lanes), F32),
                        pltpu.VMEM((S5_SEQS, 2 * n_lanes), F32),
                        pltpu.VMEM((bsz, n_heads, HG_HEAD_DIM, HG_HEAD_DIM), F32)],
        compiler_params=_cparams("arbitrary"),
        name="mixer",
    )(x, row(in_norm_g[0]), w_in[0], wb, wc, a_re8, a_im8, row(s5_d[0]), s5_w_glu[0], row(s5_b_glu[0]),
      lb, row(hg_norm_g[0]))

    bf = FFN_BLOCK
    assert d_ff % bf == 0
    n_blk = d_ff // bf
    out = pl.pallas_call(
        functools.partial(_ffn_kernel, n_blk=n_blk),
        grid=(bsz, seq_len // tm),
        in_specs=[tok(d_model), tok(width), tok(width), resident((2 * width, d_model)), const2((1, d_model)),
                  resident((d_model, 2 * d_ff)), const2((CONV_WIDTH, 2 * d_ff)),
                  const2((1, 2 * d_ff)), resident((d_ff, d_model)), const2((1, d_model))],
        out_specs=tok(d_model),
        out_shape=jax.ShapeDtypeStruct((bsz, seq_len, d_model), F32),
        scratch_shapes=[pltpu.VMEM((2 * width, d_model), BF16), pltpu.VMEM((d_ff, d_model), BF16),
                        pltpu.VMEM((FFN_ROWS, 2 * d_ff), F32), pltpu.VMEM((tm, d_model), F32),
                        pltpu.VMEM((tm, d_model), BF16), pltpu.VMEM((2, 2, tm + FFN_ROWS, bf), F32),
                        pltpu.VMEM((tm, d_ff), BF16)],
        compiler_params=_cparams("arbitrary", "arbitrary"),
        name="ffn",
    )(x, o_hg, y_s5, w_out[0], row(ffn_norm_g[0]), w_up[0].astype(BF16), conv_w[0], row(conv_b[0]),
      w_down[0], row(final_norm_g))
    return out
```

```python
import functools

import jax
import jax.numpy as jnp
from jax import lax
from jax.experimental import pallas as pl
from jax.experimental.pallas import tpu as pltpu

EPS = 1e-6
HG_CHUNK = 64
HG_HEAD_DIM = 128
S5_GROUP = 16
S5_STATE = 64
S5_SEG = 64
S5_SEQS = 8
S5_SCAN_BLOCKS = 2
S5_SCAN_UNROLL = True
CONV_WIDTH = 3
FFN_BLOCK = 256
FFN_ROWS = 16
SUBLANES = 8
LANES = 128
VMEM_LIMIT = 60 * 1024 * 1024

F32 = jnp.float32
BF16 = jnp.bfloat16

_NT = (((1,), (1,)), ((), ()))
_TN = (((0,), (0,)), ((), ()))


def _rms(x, g):
    ms = jnp.mean(x * x, axis=-1, keepdims=True)
    return x * lax.rsqrt(ms + EPS) * g


def _cparams(*sem):
    return pltpu.CompilerParams(dimension_semantics=sem, vmem_limit_bytes=VMEM_LIMIT)


def _param_kernel(lb_ref, are_ref, aim_ref, ldt_ref, bre_ref, bim_ref,
                  lb_out, abr_out, abi_out, bbr_out, bbi_out):
    z = lb_ref[...]
    e = jnp.exp(z - jnp.max(z, axis=0, keepdims=True))
    sm = e / jnp.sum(e, axis=0, keepdims=True)
    lb_out[...] = sm[0:1, :]
    dt = jnp.exp(ldt_ref[...])
    ar = are_ref[...]
    ai = aim_ref[...]
    mag = jnp.exp(dt * ar)
    abr = mag * jnp.cos(dt * ai)
    abi = mag * jnp.sin(dt * ai)
    nr = abr - 1.0
    den = ar * ar + ai * ai
    zr = (nr * ar + abi * ai) / den
    zi = (abi * ar - nr * ai) / den
    br = bre_ref[...]
    bi = bim_ref[...]
    abr_out[...] = abr
    abi_out[...] = abi
    bbr_out[...] = zr * br - zi * bi
    bbi_out[...] = zr * bi + zi * br


def _hgrn2_rows(q, fz, v, gz, lb, ng, st_ref, o_ref):
    c = HG_CHUNK
    n_chunks = q.shape[0] // c
    n_heads = st_ref.shape[0]
    tri = lax.broadcasted_iota(jnp.int32, (c, c), 0) >= lax.broadcasted_iota(jnp.int32, (c, c), 1)
    tri_b = tri.astype(BF16)

    f = lb + (1.0 - lb) * jax.nn.sigmoid(fz)
    k = 1.0 - f
    logf = jnp.log(f)
    hi = logf.astype(BF16)
    r1 = logf - hi.astype(F32)
    mid = r1.astype(BF16)
    lo = (r1 - mid.astype(F32)).astype(BF16)
    bs = []
    for ci in range(n_chunks):
        rows = slice(ci * c, (ci + 1) * c)
        bs.append(jnp.dot(tri_b, hi[rows], preferred_element_type=F32)
                  + (jnp.dot(tri_b, mid[rows], preferred_element_type=F32)
                     + jnp.dot(tri_b, lo[rows], preferred_element_type=F32)))
    b = jnp.concatenate(bs, axis=0)
    kd_f = k * jnp.exp(-b)
    qd = (q * jnp.exp(b)).astype(BF16)
    kd = kd_f.astype(BF16)
    vb = v.astype(BF16)
    gate = ng * (gz * jax.nn.sigmoid(gz))
    decs = [jnp.exp(b[(ci + 1) * c - 1:(ci + 1) * c, :]) for ci in range(n_chunks)]
    kts = [(kd_f[ci * c:(ci + 1) * c] * decs[ci]).astype(BF16) for ci in range(n_chunks)]

    for hd in range(n_heads):
        cs = slice(hd * HG_HEAD_DIM, (hd + 1) * HG_HEAD_DIM)
        st = st_ref[hd]
        for ci in range(n_chunks):
            rows = slice(ci * c, (ci + 1) * c)
            att = lax.dot_general(qd[rows, cs], kd[rows, cs], _NT, preferred_element_type=F32)
            att = jnp.where(tri, att, 0.0)
            o = jnp.dot(att.astype(BF16), vb[rows, cs], preferred_element_type=F32)
            o = o + lax.dot_general(qd[rows, cs], st.astype(BF16), _NT, preferred_element_type=F32)
            ds = lax.dot_general(vb[rows, cs], kts[ci][:, cs], _TN, preferred_element_type=F32)
            st = st * decs[ci][:, cs] + ds
            o = o * lax.rsqrt(jnp.mean(o * o, axis=-1, keepdims=True) + EPS)
            o_ref[rows, cs] = (o * gate[rows, cs]).astype(o_ref.dtype)
        st_ref[hd] = st


def _mixer_kernel(x_ref, gin_ref, win32_ref, wb_ref, wc_ref, are_ref, aim_ref, d_ref, wglu32_ref, bglu_ref,
                  lb_ref, ng_ref, ohg_ref, y_ref,
                  win_ref, wglu_ref, il_ref, bu_ref, pre_ref, pim_ref, carry_ref, hst_ref, *, n_blocks):
    h = S5_SEG
    s = S5_SEQS
    half = wb_ref.shape[2] // 2
    kw = wb_ref.shape[1]
    bsz, tt, d_model = x_ref.shape
    width = y_ref.shape[-1]

    @pl.when(pl.program_id(0) == 0)
    def _():
        win_ref[...] = win32_ref[...].astype(BF16)
        wglu_ref[...] = wglu32_ref[...].astype(BF16)
        carry_ref[...] = jnp.zeros_like(carry_ref)
        hst_ref[...] = jnp.zeros_like(hst_ref)
        ar = are_ref[...]
        ai = aim_ref[...]
        pre_ref[0:s, :] = ar
        pim_ref[0:s, :] = ai

        def powers(i, c):
            pr, pi = c
            npr = pr * ar - pi * ai
            npi = pr * ai + pi * ar
            r = pl.ds(pl.multiple_of(i * s, s), s)
            pre_ref[r, :] = npr
            pim_ref[r, :] = npi
            return npr, npi

        lax.fori_loop(1, h, powers, (ar, ai))

    xb = _rms(x_ref[...].reshape(bsz * tt, d_model), gin_ref[...]).astype(BF16)
    u_nat = jnp.dot(xb, win_ref[:, 4 * width:5 * width], preferred_element_type=F32)

    n_slabs = il_ref.shape[0]
    for bi in range(bsz):
        for k in range(2):
            r0 = bi * tt + k * h
            for j in range(n_slabs):
                il_ref[j, pl.ds(2 * bi + k, h, stride=s), :] = u_nat[r0:r0 + h, j * LANES:(j + 1) * LANES]
    u = jnp.concatenate([il_ref[j] for j in range(n_slabs)], axis=1)
    ub = u.astype(BF16)
    for m in range(n_blocks):
        bu_ref[:, 2 * half * m:2 * half * (m + 1)] = jnp.dot(
            ub[:, kw * m:kw * (m + 1)], wb_ref[m], preferred_element_type=F32)

    q, fz, v, gz = (jnp.dot(xb, win_ref[:, i * width:(i + 1) * width], preferred_element_type=F32)
                    for i in range(4))
    for bi in range(bsz):
        r = slice(bi * tt, (bi + 1) * tt)
        _hgrn2_rows(q[r], fz[r], v[r], gz[r], lb_ref[...], ng_ref[...], hst_ref.at[bi], ohg_ref.at[bi])

    second = lax.broadcasted_iota(jnp.int32, (s, half), 0) % 2 == 1
    last = slice((h - 1) * s, h * s)

    for m0 in range(0, n_blocks, S5_SCAN_BLOCKS):
        blocks = []
        for m in range(m0, m0 + S5_SCAN_BLOCKS):
            lo = 2 * half * m
            blocks.append((slice(lo, lo + half), slice(lo + half, lo + 2 * half),
                           slice(half * m, half * (m + 1))))
        coef = [(are_ref[:, am], aim_ref[:, am]) for _, _, am in blocks]

        def scan(i, c, blocks=blocks, coef=coef):
            r = pl.ds(pl.multiple_of(i * s, s), s)
            out = []
            for (re, im, _), (ar, ai), (xr, xi) in zip(blocks, coef, c):
                nxr = ar * xr - ai * xi + bu_ref[r, re]
                nxi = ar * xi + ai * xr + bu_ref[r, im]
                bu_ref[r, re] = nxr
                bu_ref[r, im] = nxi
                out.append((nxr, nxi))
            return tuple(out)

        init = tuple((carry_ref[:, re], carry_ref[:, im]) for re, im, _ in blocks)
        ends = lax.fori_loop(0, h, scan, init, unroll=S5_SCAN_UNROLL)
        starts = [(jnp.where(second, pltpu.roll(er, 1, axis=0), 0.0),
                   jnp.where(second, pltpu.roll(ei, 1, axis=0), 0.0)) for er, ei in ends]

        def fix(i, c, blocks=blocks, starts=starts):
            r = pl.ds(pl.multiple_of(i * s, s), s)
            for (re, im, am), (cr, ci) in zip(blocks, starts):
                pr = pre_ref[r, am]
                pi = pim_ref[r, am]
                bu_ref[r, re] = bu_ref[r, re] + (pr * cr - pi * ci)
                bu_ref[r, im] = bu_ref[r, im] + (pr * ci + pi * cr)
            return c

        lax.fori_loop(0, h, fix, 0, unroll=S5_SCAN_UNROLL)
        for re, im, _ in blocks:
            carry_ref[:, re] = jnp.where(second, 0.0, pltpu.roll(bu_ref[last, re], s - 1, axis=0))
            carry_ref[:, im] = jnp.where(second, 0.0, pltpu.roll(bu_ref[last, im], s - 1, axis=0))

    ys = [jnp.dot(bu_ref[:, 2 * half * m:2 * half * (m + 1)].astype(BF16), wc_ref[m],
                  preferred_element_type=F32) for m in range(n_blocks)]
    y = jnp.concatenate(ys, axis=1) + d_ref[...] * u
    y = jax.nn.gelu(y)
    z = jnp.dot(y.astype(BF16), wglu_ref[...], preferred_element_type=F32) + bglu_ref[...]
    y = y * jax.nn.sigmoid(z)
    for j in range(n_slabs):
        il_ref[j] = y[:, j * LANES:(j + 1) * LANES]
    for bi in range(y_ref.shape[0]):
        for k in range(2):
            for j in range(n_slabs):
                y_ref[bi, k * h:(k + 1) * h, j * LANES:(j + 1) * LANES] = il_ref[
                    j, pl.ds(2 * bi + k, h, stride=s), :].astype(y_ref.dtype)


def _ffn_kernel(x_ref, ohg_ref, y_ref, wout32_ref, gf_ref, wup_ref, cw_ref, cb_ref, wdn32_ref, g_ref, o_ref,
                wout_ref, wdn_ref, tail_ref, h_ref, hn_ref, pre_ref, act_ref, *, n_blk):
    @pl.when((pl.program_id(0) == 0) & (pl.program_id(1) == 0))
    def _():
        wout_ref[...] = wout32_ref[...].astype(BF16)
        wdn_ref[...] = wdn32_ref[...].astype(BF16)

    @pl.when(pl.program_id(1) == 0)
    def _():
        tail_ref[...] = jnp.zeros_like(tail_ref)

    width = ohg_ref.shape[-1]
    mix = jnp.dot(ohg_ref[...].astype(BF16), wout_ref[0:width, :], preferred_element_type=F32)
    mix = mix + jnp.dot(y_ref[...].astype(BF16), wout_ref[width:2 * width, :], preferred_element_type=F32)
    hres = x_ref[...] + mix
    h_ref[...] = hres
    hn_ref[...] = _rms(hres, gf_ref[...]).astype(BF16)

    tm = act_ref.shape[0]
    bf = pre_ref.shape[-1]
    rows = FFN_ROWS
    top = lax.broadcasted_iota(jnp.int32, (SUBLANES, bf), 0)

    cols = lambda j, part: slice((part * n_blk + j) * bf, (part * n_blk + j + 1) * bf)

    def up(j, slot):
        for part in range(2):
            pre_ref[slot, part, rows:rows + tm] = jnp.dot(hn_ref[...], wup_ref[:, cols(j, part)],
                                                          preferred_element_type=F32)

    def mid(j, slot):
        taps, prev = [], []
        for part in range(2):
            blk = cols(j, part)
            pre_ref[slot, part, 0:rows] = tail_ref[:, blk]
            tail_ref[:, blk] = pre_ref[slot, part, tm:tm + rows]
            cw = cw_ref[:, blk]
            taps.append((cb_ref[:, blk], cw[0:1, :], cw[1:2, :], cw[2:3, :]))
            p = pre_ref[slot, part, 0:rows]
            prev.append((pltpu.roll(p, 1, axis=0), pltpu.roll(p, 2, axis=0)))
        for g in range(tm // rows):
            r0 = rows * (g + 1)
            outs = []
            for part in range(2):
                cur = pre_ref[slot, part, r0:r0 + rows]
                rc = (pltpu.roll(cur, 1, axis=0), pltpu.roll(cur, 2, axis=0))
                sh = [jnp.concatenate([jnp.where(top < d + 1, prev[part][d][0:SUBLANES], rc[d][0:SUBLANES]),
                                       rc[d][SUBLANES:]], axis=0) for d in range(2)]
                prev[part] = rc
                cb, w0, w1, w2 = taps[part]
                outs.append(cb + w0 * sh[1] + w1 * sh[0] + w2 * cur)
            gate, val = outs
            act_ref[rows * g:rows * (g + 1), j * bf:(j + 1) * bf] = (
                (gate * jax.nn.sigmoid(gate)) * val).astype(BF16)

    up(0, 0)
    for j in range(n_blk):
        if j + 1 < n_blk:
            up(j + 1, (j + 1) % 2)
        mid(j, j % 2)
    out = h_ref[...] + jnp.dot(act_ref[...], wdn_ref[...], preferred_element_type=F32)
    o_ref[...] = _rms(out, g_ref[...])


def _tile_sizes(seq_len):
    tm = min(512, seq_len)
    tl = min(256, seq_len)
    assert seq_len % tm == 0 and seq_len % tl == 0 and tm % (2 * S5_SEG) == 0 and tl % HG_CHUNK == 0
    return tm, tl


def kernel(x, in_norm_g, w_in, hg_lb, hg_norm_g, s5_a_re, s5_a_im, s5_log_dt, s5_b_re, s5_b_im,
           s5_c_re, s5_c_im, s5_d, s5_w_glu, s5_b_glu, w_out, ffn_norm_g, w_up, conv_w, conv_b,
           w_down, final_norm_g):
    bsz, seq_len, d_model = x.shape
    depth = w_in.shape[0]
    assert depth == 1, "single-layer block"
    width = hg_norm_g.shape[1]
    assert w_in.shape[2] == 5 * width and s5_d.shape[1] == width
    n_heads = width // HG_HEAD_DIM
    groups, n_state, n_chan = s5_b_re.shape[1:]
    assert (n_state, n_chan) == (S5_STATE, S5_GROUP) and groups * n_chan == width
    assert 2 * bsz == S5_SEQS
    d_ff = w_down.shape[1]
    tm, tl = _tile_sizes(seq_len)
    n_tiles = seq_len // (2 * S5_SEG)
    row = lambda a: a.reshape(1, -1)

    gp = groups * n_chan
    rep = lambda a: jnp.repeat(a, n_chan, axis=0)
    ldt = jnp.broadcast_to(jnp.repeat(s5_log_dt[0], n_chan)[:, None], (gp, n_state))
    b_t = lambda a: a.transpose(0, 2, 1).reshape(gp, n_state)
    lb, abr, abi, bbr, bbi = pl.pallas_call(
        _param_kernel,
        out_shape=[jax.ShapeDtypeStruct((1, width), F32)] + [jax.ShapeDtypeStruct((gp, n_state), F32)] * 4,
        name="params",
    )(hg_lb, rep(s5_a_re[0]), rep(s5_a_im[0]), ldt, b_t(s5_b_re[0]), b_t(s5_b_im[0]))

    gpb = LANES // n_chan
    n_blocks = groups // gpb
    eye = jnp.eye(gpb, dtype=F32)
    blockdiag_in = lambda w: jnp.einsum("mgpn,gh->mgphn", w.reshape(n_blocks, gpb, n_chan, n_state),
                                        eye).reshape(n_blocks, gpb * n_chan, gpb * n_state)
    blockdiag_out = lambda w: jnp.einsum("mgpn,gh->mhngp", w.reshape(n_blocks, gpb, n_chan, n_state),
                                         eye).reshape(n_blocks, gpb * n_state, gpb * n_chan)
    wb = jnp.concatenate([blockdiag_in(bbr), blockdiag_in(bbi)], axis=2).astype(BF16)
    wc = jnp.concatenate([blockdiag_out(s5_c_re[0]), -blockdiag_out(s5_c_im[0])], axis=1).astype(BF16)
    n_lanes = groups * n_state
    a_rows = lambda a: jnp.broadcast_to(a.reshape(groups, n_chan, n_state)[:, 0, :].reshape(1, n_lanes),
                                        (S5_SEQS, n_lanes))
    a_re8, a_im8 = a_rows(abr), a_rows(abi)

    tok = lambda w: pl.BlockSpec((None, tm, w), lambda b, t: (b, t, 0))
    const2 = lambda shape: pl.BlockSpec(shape, lambda b, t: (0,) * len(shape))
    resident = lambda shape: pl.BlockSpec(shape, lambda b, t: (0,) * len(shape), pipeline_mode=pl.Buffered(1))

    m_rows = S5_SEG * S5_SEQS
    c1 = lambda shape: pl.BlockSpec(shape, lambda j: (0,) * len(shape))
    res1 = lambda shape: pl.BlockSpec(shape, lambda j: (0,) * len(shape), pipeline_mode=pl.Buffered(1))
    tile_spec = lambda w: pl.BlockSpec((bsz, 2 * S5_SEG, w), lambda j: (0, j, 0))
    act = lambda dt: jax.ShapeDtypeStruct((bsz, seq_len, width), dt)
    o_hg, y_s5 = pl.pallas_call(
        functools.partial(_mixer_kernel, n_blocks=n_blocks),
        grid=(n_tiles,),
        in_specs=[tile_spec(d_model), c1((1, d_model)), res1((d_model, 5 * width)),
                  c1(wb.shape), c1(wc.shape), c1(a_re8.shape), c1(a_im8.shape),
                  c1((1, width)), res1((width, width)), c1((1, width)), c1((1, width)), c1((1, width))],
        out_specs=[tile_spec(width)] * 2,
        out_shape=[act(BF16)] * 2,
        scratch_shapes=[pltpu.VMEM((d_model, 5 * width), BF16), pltpu.VMEM((width, width), BF16),
                        pltpu.VMEM((width // LANES, m_rows, LANES), F32), pltpu.VMEM((m_rows, 2 * n_lanes), F32),
                        pltpu.VMEM((m_rows, n_lanes), F32), pltpu.VMEM((m_rows, n_lanes), F32),
                        pltpu.VMEM((S5_SEQS, 2 * n_lanes), F32),
                        pltpu.VMEM((bsz, n_heads, HG_HEAD_DIM, HG_HEAD_DIM), F32)],
        compiler_params=_cparams("arbitrary"),
        name="mixer",
    )(x, row(in_norm_g[0]), w_in[0], wb, wc, a_re8, a_im8, row(s5_d[0]), s5_w_glu[0], row(s5_b_glu[0]),
      lb, row(hg_norm_g[0]))

    bf = FFN_BLOCK
    assert d_ff % bf == 0
    n_blk = d_ff // bf
    out = pl.pallas_call(
        functools.partial(_ffn_kernel, n_blk=n_blk),
        grid=(bsz, seq_len // tm),
        in_specs=[tok(d_model), tok(width), tok(width), resident((2 * width, d_model)), const2((1, d_model)),
                  resident((d_model, 2 * d_ff)), const2((CONV_WIDTH, 2 * d_ff)),
                  const2((1, 2 * d_ff)), resident((d_ff, d_model)), const2((1, d_model))],
        out_specs=tok(d_model),
        out_shape=jax.ShapeDtypeStruct((bsz, seq_len, d_model), F32),
        scratch_shapes=[pltpu.VMEM((2 * width, d_model), BF16), pltpu.VMEM((d_ff, d_model), BF16),
                        pltpu.VMEM((FFN_ROWS, 2 * d_ff), F32), pltpu.VMEM((tm, d_model), F32),
                        pltpu.VMEM((tm, d_model), BF16), pltpu.VMEM((2, 2, tm + FFN_ROWS, bf), F32),
                        pltpu.VMEM((tm, d_ff), BF16)],
        compiler_params=_cparams("arbitrary", "arbitrary"),
        name="ffn",
    )(x, o_hg, y_s5, w_out[0], row(ffn_norm_g[0]), w_up[0].astype(BF16), conv_w[0], row(conv_b[0]),
      w_down[0], row(final_norm_g))
    return out
```

```python
import functools

import jax
import jax.numpy as jnp
from jax import lax
from jax.experimental import pallas as pl
from jax.experimental.pallas import tpu as pltpu

EPS = 1e-6
HG_CHUNK = 64
HG_HEAD_DIM = 128
S5_GROUP = 16
S5_STATE = 64
S5_SEG = 64
S5_SEQS = 8
S5_SCAN_BLOCKS = 2
S5_SCAN_UNROLL = True
CONV_WIDTH = 3
FFN_BLOCK = 256
FFN_ROWS = 16
SUBLANES = 8
LANES = 128
VMEM_LIMIT = 60 * 1024 * 1024

F32 = jnp.float32
BF16 = jnp.bfloat16

_NT = (((1,), (1,)), ((), ()))
_TN = (((0,), (0,)), ((), ()))


def _rms(x, g):
    ms = jnp.mean(x * x, axis=-1, keepdims=True)
    return x * lax.rsqrt(ms + EPS) * g


def _cparams(*sem):
    return pltpu.CompilerParams(dimension_semantics=sem, vmem_limit_bytes=VMEM_LIMIT)


def _param_kernel(lb_ref, are_ref, aim_ref, ldt_ref, bre_ref, bim_ref,
                  lb_out, abr_out, abi_out, bbr_out, bbi_out):
    z = lb_ref[...]
    e = jnp.exp(z - jnp.max(z, axis=0, keepdims=True))
    sm = e / jnp.sum(e, axis=0, keepdims=True)
    lb_out[...] = sm[0:1, :]
    dt = jnp.exp(ldt_ref[...])
    ar = are_ref[...]
    ai = aim_ref[...]
    mag = jnp.exp(dt * ar)
    abr = mag * jnp.cos(dt * ai)
    abi = mag * jnp.sin(dt * ai)
    nr = abr - 1.0
    den = ar * ar + ai * ai
    zr = (nr * ar + abi * ai) / den
    zi = (abi * ar - nr * ai) / den
    br = bre_ref[...]
    bi = bim_ref[...]
    abr_out[...] = abr
    abi_out[...] = abi
    bbr_out[...] = zr * br - zi * bi
    bbi_out[...] = zr * bi + zi * br


def _hgrn2_rows(q, fz, v, gz, lb, ng, st_ref, o_ref):
    c = HG_CHUNK
    n_chunks = q.shape[0] // c
    n_heads = st_ref.shape[0]
    tri = lax.broadcasted_iota(jnp.int32, (c, c), 0) >= lax.broadcasted_iota(jnp.int32, (c, c), 1)
    tri_b = tri.astype(BF16)

    f = lb + (1.0 - lb) * jax.nn.sigmoid(fz)
    k = 1.0 - f
    logf = jnp.log(f)
    hi = logf.astype(BF16)
    r1 = logf - hi.astype(F32)
    mid = r1.astype(BF16)
    lo = (r1 - mid.astype(F32)).astype(BF16)
    bs = []
    for ci in range(n_chunks):
        rows = slice(ci * c, (ci + 1) * c)
        bs.append(jnp.dot(tri_b, hi[rows], preferred_element_type=F32)
                  + (jnp.dot(tri_b, mid[rows], preferred_element_type=F32)
                     + jnp.dot(tri_b, lo[rows], preferred_element_type=F32)))
    b = jnp.concatenate(bs, axis=0)
    kd_f = k * jnp.exp(-b)
    qd = (q * jnp.exp(b)).astype(BF16)
    kd = kd_f.astype(BF16)
    vb = v.astype(BF16)
    gate = ng * (gz * jax.nn.sigmoid(gz))
    decs = [jnp.exp(b[(ci + 1) * c - 1:(ci + 1) * c, :]) for ci in range(n_chunks)]
    kts = [(kd_f[ci * c:(ci + 1) * c] * decs[ci]).astype(BF16) for ci in range(n_chunks)]

    for hd in range(n_heads):
        cs = slice(hd * HG_HEAD_DIM, (hd + 1) * HG_HEAD_DIM)
        st = st_ref[hd]
        for ci in range(n_chunks):
            rows = slice(ci * c, (ci + 1) * c)
            att = lax.dot_general(qd[rows, cs], kd[rows, cs], _NT, preferred_element_type=F32)
            att = jnp.where(tri, att, 0.0)
            o = jnp.dot(att.astype(BF16), vb[rows, cs], preferred_element_type=F32)
            o = o + lax.dot_general(qd[rows, cs], st.astype(BF16), _NT, preferred_element_type=F32)
            ds = lax.dot_general(vb[rows, cs], kts[ci][:, cs], _TN, preferred_element_type=F32)
            st = st * decs[ci][:, cs] + ds
            o = o * lax.rsqrt(jnp.mean(o * o, axis=-1, keepdims=True) + EPS)
            o_ref[rows, cs] = o * gate[rows, cs]
        st_ref[hd] = st


def _mixer_kernel(x_ref, gin_ref, win32_ref, wb_ref, wc_ref, are_ref, aim_ref, d_ref, wglu32_ref, bglu_ref,
                  lb_ref, ng_ref, wup32_ref, ohg_ref, y_ref, wup_ref,
                  win_ref, wglu_ref, il_ref, bu_ref, pre_ref, pim_ref, carry_ref, hst_ref, *, n_blocks):
    wup_ref[...] = wup32_ref[...].astype(BF16)
    h = S5_SEG
    s = S5_SEQS
    half = wb_ref.shape[2] // 2
    kw = wb_ref.shape[1]
    bsz, tt, d_model = x_ref.shape
    width = y_ref.shape[-1]

    @pl.when(pl.program_id(0) == 0)
    def _():
        win_ref[...] = win32_ref[...].astype(BF16)
        wglu_ref[...] = wglu32_ref[...].astype(BF16)
        carry_ref[...] = jnp.zeros_like(carry_ref)
        hst_ref[...] = jnp.zeros_like(hst_ref)
        ar = are_ref[...]
        ai = aim_ref[...]
        pre_ref[0:s, :] = ar
        pim_ref[0:s, :] = ai

        def powers(i, c):
            pr, pi = c
            npr = pr * ar - pi * ai
            npi = pr * ai + pi * ar
            r = pl.ds(pl.multiple_of(i * s, s), s)
            pre_ref[r, :] = npr
            pim_ref[r, :] = npi
            return npr, npi

        lax.fori_loop(1, h, powers, (ar, ai))

    xb = _rms(x_ref[...].reshape(bsz * tt, d_model), gin_ref[...]).astype(BF16)
    u_nat = jnp.dot(xb, win_ref[:, 4 * width:5 * width], preferred_element_type=F32)

    n_slabs = il_ref.shape[0]
    for bi in range(bsz):
        for k in range(2):
            r0 = bi * tt + k * h
            for j in range(n_slabs):
                il_ref[j, pl.ds(2 * bi + k, h, stride=s), :] = u_nat[r0:r0 + h, j * LANES:(j + 1) * LANES]
    u = jnp.concatenate([il_ref[j] for j in range(n_slabs)], axis=1)
    ub = u.astype(BF16)
    for m in range(n_blocks):
        bu_ref[:, 2 * half * m:2 * half * (m + 1)] = jnp.dot(
            ub[:, kw * m:kw * (m + 1)], wb_ref[m], preferred_element_type=F32)

    q, fz, v, gz = (jnp.dot(xb, win_ref[:, i * width:(i + 1) * width], preferred_element_type=F32)
                    for i in range(4))
    for bi in range(bsz):
        r = slice(bi * tt, (bi + 1) * tt)
        _hgrn2_rows(q[r], fz[r], v[r], gz[r], lb_ref[...], ng_ref[...], hst_ref.at[bi], ohg_ref.at[bi])

    second = lax.broadcasted_iota(jnp.int32, (s, half), 0) % 2 == 1
    last = slice((h - 1) * s, h * s)

    for m0 in range(0, n_blocks, S5_SCAN_BLOCKS):
        blocks = []
        for m in range(m0, m0 + S5_SCAN_BLOCKS):
            lo = 2 * half * m
            blocks.append((slice(lo, lo + half), slice(lo + half, lo + 2 * half),
                           slice(half * m, half * (m + 1))))
        coef = [(are_ref[:, am], aim_ref[:, am]) for _, _, am in blocks]

        def scan(i, c, blocks=blocks, coef=coef):
            r = pl.ds(pl.multiple_of(i * s, s), s)
            out = []
            for (re, im, _), (ar, ai), (xr, xi) in zip(blocks, coef, c):
                nxr = ar * xr - ai * xi + bu_ref[r, re]
                nxi = ar * xi + ai * xr + bu_ref[r, im]
                bu_ref[r, re] = nxr
                bu_ref[r, im] = nxi
                out.append((nxr, nxi))
            return tuple(out)

        init = tuple((carry_ref[:, re], carry_ref[:, im]) for re, im, _ in blocks)
        ends = lax.fori_loop(0, h, scan, init, unroll=S5_SCAN_UNROLL)
        starts = [(jnp.where(second, pltpu.roll(er, 1, axis=0), 0.0),
                   jnp.where(second, pltpu.roll(ei, 1, axis=0), 0.0)) for er, ei in ends]

        def fix(i, c, blocks=blocks, starts=starts):
            r = pl.ds(pl.multiple_of(i * s, s), s)
            for (re, im, am), (cr, ci) in zip(blocks, starts):
                pr = pre_ref[r, am]
                pi = pim_ref[r, am]
                bu_ref[r, re] = bu_ref[r, re] + (pr * cr - pi * ci)
                bu_ref[r, im] = bu_ref[r, im] + (pr * ci + pi * cr)
            return c

        lax.fori_loop(0, h, fix, 0, unroll=S5_SCAN_UNROLL)
        for re, im, _ in blocks:
            carry_ref[:, re] = jnp.where(second, 0.0, pltpu.roll(bu_ref[last, re], s - 1, axis=0))
            carry_ref[:, im] = jnp.where(second, 0.0, pltpu.roll(bu_ref[last, im], s - 1, axis=0))

    ys = [jnp.dot(bu_ref[:, 2 * half * m:2 * half * (m + 1)].astype(BF16), wc_ref[m],
                  preferred_element_type=F32) for m in range(n_blocks)]
    y = jnp.concatenate(ys, axis=1) + d_ref[...] * u
    y = jax.nn.gelu(y)
    z = jnp.dot(y.astype(BF16), wglu_ref[...], preferred_element_type=F32) + bglu_ref[...]
    y = y * jax.nn.sigmoid(z)
    for j in range(n_slabs):
        il_ref[j] = y[:, j * LANES:(j + 1) * LANES]
    for bi in range(y_ref.shape[0]):
        for k in range(2):
            for j in range(n_slabs):
                y_ref[bi, k * h:(k + 1) * h, j * LANES:(j + 1) * LANES] = il_ref[
                    j, pl.ds(2 * bi + k, h, stride=s), :]


def _ffn_kernel(x_ref, ohg_ref, y_ref, wout32_ref, gf_ref, wup_ref, cw_ref, cb_ref, wdn32_ref, g_ref, o_ref,
                wout_ref, wdn_ref, tail_ref, h_ref, hn_ref, pre_ref, act_ref, *, n_blk):
    @pl.when((pl.program_id(0) == 0) & (pl.program_id(1) == 0))
    def _():
        wout_ref[...] = wout32_ref[...].astype(BF16)
        wdn_ref[...] = wdn32_ref[...].astype(BF16)

    @pl.when(pl.program_id(1) == 0)
    def _():
        tail_ref[...] = jnp.zeros_like(tail_ref)

    width = ohg_ref.shape[-1]
    mix = jnp.dot(ohg_ref[...].astype(BF16), wout_ref[0:width, :], preferred_element_type=F32)
    mix = mix + jnp.dot(y_ref[...].astype(BF16), wout_ref[width:2 * width, :], preferred_element_type=F32)
    hres = x_ref[...] + mix
    h_ref[...] = hres
    hn_ref[...] = _rms(hres, gf_ref[...]).astype(BF16)

    tm = act_ref.shape[0]
    bf = pre_ref.shape[-1]
    rows = FFN_ROWS
    top = lax.broadcasted_iota(jnp.int32, (SUBLANES, bf), 0)

    cols = lambda j, part: slice((part * n_blk + j) * bf, (part * n_blk + j + 1) * bf)

    def up(j, slot):
        for part in range(2):
            pre_ref[slot, part, rows:rows + tm] = jnp.dot(hn_ref[...], wup_ref[:, cols(j, part)],
                                                          preferred_element_type=F32)

    def mid(j, slot):
        taps, prev = [], []
        for part in range(2):
            blk = cols(j, part)
            pre_ref[slot, part, 0:rows] = tail_ref[:, blk]
            tail_ref[:, blk] = pre_ref[slot, part, tm:tm + rows]
            cw = cw_ref[:, blk]
            taps.append((cb_ref[:, blk], cw[0:1, :], cw[1:2, :], cw[2:3, :]))
            p = pre_ref[slot, part, 0:rows]
            prev.append((pltpu.roll(p, 1, axis=0), pltpu.roll(p, 2, axis=0)))
        for g in range(tm // rows):
            r0 = rows * (g + 1)
            outs = []
            for part in range(2):
                cur = pre_ref[slot, part, r0:r0 + rows]
                rc = (pltpu.roll(cur, 1, axis=0), pltpu.roll(cur, 2, axis=0))
                sh = [jnp.concatenate([jnp.where(top < d + 1, prev[part][d][0:SUBLANES], rc[d][0:SUBLANES]),
                                       rc[d][SUBLANES:]], axis=0) for d in range(2)]
                prev[part] = rc
                cb, w0, w1, w2 = taps[part]
                outs.append(cb + w0 * sh[1] + w1 * sh[0] + w2 * cur)
            gate, val = outs
            act_ref[rows * g:rows * (g + 1), j * bf:(j + 1) * bf] = (
                (gate * jax.nn.sigmoid(gate)) * val).astype(BF16)

    up(0, 0)
    for j in range(n_blk):
        if j + 1 < n_blk:
            up(j + 1, (j + 1) % 2)
        mid(j, j % 2)
    out = h_ref[...] + jnp.dot(act_ref[...], wdn_ref[...], preferred_element_type=F32)
    o_ref[...] = _rms(out, g_ref[...])


def _tile_sizes(seq_len):
    tm = min(512, seq_len)
    tl = min(256, seq_len)
    assert seq_len % tm == 0 and seq_len % tl == 0 and tm % (2 * S5_SEG) == 0 and tl % HG_CHUNK == 0
    return tm, tl


def kernel(x, in_norm_g, w_in, hg_lb, hg_norm_g, s5_a_re, s5_a_im, s5_log_dt, s5_b_re, s5_b_im,
           s5_c_re, s5_c_im, s5_d, s5_w_glu, s5_b_glu, w_out, ffn_norm_g, w_up, conv_w, conv_b,
           w_down, final_norm_g):
    bsz, seq_len, d_model = x.shape
    depth = w_in.shape[0]
    assert depth == 1, "single-layer block"
    width = hg_norm_g.shape[1]
    assert w_in.shape[2] == 5 * width and s5_d.shape[1] == width
    n_heads = width // HG_HEAD_DIM
    groups, n_state, n_chan = s5_b_re.shape[1:]
    assert (n_state, n_chan) == (S5_STATE, S5_GROUP) and groups * n_chan == width
    assert 2 * bsz == S5_SEQS
    d_ff = w_down.shape[1]
    tm, tl = _tile_sizes(seq_len)
    n_tiles = seq_len // (2 * S5_SEG)
    row = lambda a: a.reshape(1, -1)

    gp = groups * n_chan
    rep = lambda a: jnp.repeat(a, n_chan, axis=0)
    ldt = jnp.broadcast_to(jnp.repeat(s5_log_dt[0], n_chan)[:, None], (gp, n_state))
    b_t = lambda a: a.transpose(0, 2, 1).reshape(gp, n_state)
    lb, abr, abi, bbr, bbi = pl.pallas_call(
        _param_kernel,
        out_shape=[jax.ShapeDtypeStruct((1, width), F32)] + [jax.ShapeDtypeStruct((gp, n_state), F32)] * 4,
        name="params",
    )(hg_lb, rep(s5_a_re[0]), rep(s5_a_im[0]), ldt, b_t(s5_b_re[0]), b_t(s5_b_im[0]))

    gpb = LANES // n_chan
    n_blocks = groups // gpb
    eye = jnp.eye(gpb, dtype=F32)
    blockdiag_in = lambda w: jnp.einsum("mgpn,gh->mgphn", w.reshape(n_blocks, gpb, n_chan, n_state),
                                        eye).reshape(n_blocks, gpb * n_chan, gpb * n_state)
    blockdiag_out = lambda w: jnp.einsum("mgpn,gh->mhngp", w.reshape(n_blocks, gpb, n_chan, n_state),
                                         eye).reshape(n_blocks, gpb * n_state, gpb * n_chan)
    wb = jnp.concatenate([blockdiag_in(bbr), blockdiag_in(bbi)], axis=2).astype(BF16)
    wc = jnp.concatenate([blockdiag_out(s5_c_re[0]), -blockdiag_out(s5_c_im[0])], axis=1).astype(BF16)
    n_lanes = groups * n_state
    a_rows = lambda a: jnp.broadcast_to(a.reshape(groups, n_chan, n_state)[:, 0, :].reshape(1, n_lanes),
                                        (S5_SEQS, n_lanes))
    a_re8, a_im8 = a_rows(abr), a_rows(abi)

    tok = lambda w: pl.BlockSpec((None, tm, w), lambda b, t: (b, t, 0))
    const2 = lambda shape: pl.BlockSpec(shape, lambda b, t: (0,) * len(shape))
    resident = lambda shape: pl.BlockSpec(shape, lambda b, t: (0,) * len(shape), pipeline_mode=pl.Buffered(1))

    m_rows = S5_SEG * S5_SEQS
    c1 = lambda shape: pl.BlockSpec(shape, lambda j: (0,) * len(shape))
    res1 = lambda shape: pl.BlockSpec(shape, lambda j: (0,) * len(shape), pipeline_mode=pl.Buffered(1))
    tile_spec = lambda w: pl.BlockSpec((bsz, 2 * S5_SEG, w), lambda j: (0, j, 0))
    act = lambda dt: jax.ShapeDtypeStruct((bsz, seq_len, width), dt)
    assert d_model % n_tiles == 0
    wup_slab = pl.BlockSpec((d_model // n_tiles, 2 * d_ff), lambda j: (j, 0))
    o_hg, y_s5, w_up_bf16 = pl.pallas_call(
        functools.partial(_mixer_kernel, n_blocks=n_blocks),
        grid=(n_tiles,),
        in_specs=[tile_spec(d_model), c1((1, d_model)), res1((d_model, 5 * width)),
                  c1(wb.shape), c1(wc.shape), c1(a_re8.shape), c1(a_im8.shape),
                  c1((1, width)), res1((width, width)), c1((1, width)), c1((1, width)), c1((1, width)),
                  wup_slab],
        out_specs=[tile_spec(width)] * 2 + [wup_slab],
        out_shape=[act(F32)] * 2 + [jax.ShapeDtypeStruct((d_model, 2 * d_ff), BF16)],
        scratch_shapes=[pltpu.VMEM((d_model, 5 * width), BF16), pltpu.VMEM((width, width), BF16),
                        pltpu.VMEM((width // LANES, m_rows, LANES), F32), pltpu.VMEM((m_rows, 2 * n_lanes), F32),
                        pltpu.VMEM((m_rows, n_lanes), F32), pltpu.VMEM((m_rows, n_lanes), F32),
                        pltpu.VMEM((S5_SEQS, 2 * n_lanes), F32),
                        pltpu.VMEM((bsz, n_heads, HG_HEAD_DIM, HG_HEAD_DIM), F32)],
        compiler_params=_cparams("arbitrary"),
        name="mixer",
    )(x, row(in_norm_g[0]), w_in[0], wb, wc, a_re8, a_im8, row(s5_d[0]), s5_w_glu[0], row(s5_b_glu[0]),
      lb, row(hg_norm_g[0]), w_up[0])

    bf = FFN_BLOCK
    assert d_ff % bf == 0
    n_blk = d_ff // bf
    out = pl.pallas_call(
        functools.partial(_ffn_kernel, n_blk=n_blk),
        grid=(bsz, seq_len // tm),
        in_specs=[tok(d_model), tok(width), tok(width), resident((2 * width, d_model)), const2((1, d_model)),
                  resident((d_model, 2 * d_ff)), const2((CONV_WIDTH, 2 * d_ff)),
                  const2((1, 2 * d_ff)), resident((d_ff, d_model)), const2((1, d_model))],
        out_specs=tok(d_model),
        out_shape=jax.ShapeDtypeStruct((bsz, seq_len, d_model), F32),
        scratch_shapes=[pltpu.VMEM((2 * width, d_model), BF16), pltpu.VMEM((d_ff, d_model), BF16),
                        pltpu.VMEM((FFN_ROWS, 2 * d_ff), F32), pltpu.VMEM((tm, d_model), F32),
                        pltpu.VMEM((tm, d_model), BF16), pltpu.VMEM((2, 2, tm + FFN_ROWS, bf), F32),
                        pltpu.VMEM((tm, d_ff), BF16)],
        compiler_params=_cparams("arbitrary", "arbitrary"),
        name="ffn",
    )(x, o_hg, y_s5, w_out[0], row(ffn_norm_g[0]), w_up_bf16, conv_w[0], row(conv_b[0]),
      w_down[0], row(final_norm_g))
    return out
```

```python
import functools

import jax
import jax.numpy as jnp
from jax import lax
from jax.experimental import pallas as pl
from jax.experimental.pallas import tpu as pltpu

EPS = 1e-6
HG_CHUNK = 64
HG_HEAD_DIM = 128
S5_GROUP = 16
S5_STATE = 64
S5_SEG = 64
S5_SEQS = 8
S5_SCAN_BLOCKS = 2
S5_SCAN_UNROLL = True
CONV_WIDTH = 3
FFN_BLOCK = 256
FFN_ROWS = 16
SUBLANES = 8
LANES = 128
VMEM_LIMIT = 60 * 1024 * 1024

F32 = jnp.float32
BF16 = jnp.bfloat16

_NT = (((1,), (1,)), ((), ()))
_TN = (((0,), (0,)), ((), ()))


def _rms(x, g):
    ms = jnp.mean(x * x, axis=-1, keepdims=True)
    return x * lax.rsqrt(ms + EPS) * g


def _cparams(*sem):
    return pltpu.CompilerParams(dimension_semantics=sem, vmem_limit_bytes=VMEM_LIMIT)


def _param_kernel(lb_ref, are_ref, aim_ref, ldt_ref, bre_ref, bim_ref,
                  lb_out, abr_out, abi_out, bbr_out, bbi_out):
    z = lb_ref[...]
    e = jnp.exp(z - jnp.max(z, axis=0, keepdims=True))
    sm = e / jnp.sum(e, axis=0, keepdims=True)
    lb_out[...] = sm[0:1, :]
    dt = jnp.exp(ldt_ref[...])
    ar = are_ref[...]
    ai = aim_ref[...]
    mag = jnp.exp(dt * ar)
    abr = mag * jnp.cos(dt * ai)
    abi = mag * jnp.sin(dt * ai)
    nr = abr - 1.0
    den = ar * ar + ai * ai
    zr = (nr * ar + abi * ai) / den
    zi = (abi * ar - nr * ai) / den
    br = bre_ref[...]
    bi = bim_ref[...]
    abr_out[...] = abr
    abi_out[...] = abi
    bbr_out[...] = zr * br - zi * bi
    bbi_out[...] = zr * bi + zi * br


def _hgrn2_rows(q, fz, v, gz, lb, ng, st_ref, o_ref):
    c = HG_CHUNK
    n_chunks = q.shape[0] // c
    n_heads = st_ref.shape[0]
    tri = lax.broadcasted_iota(jnp.int32, (c, c), 0) >= lax.broadcasted_iota(jnp.int32, (c, c), 1)
    tri_b = tri.astype(BF16)

    f = lb + (1.0 - lb) * jax.nn.sigmoid(fz)
    k = 1.0 - f
    logf = jnp.log(f)
    hi = logf.astype(BF16)
    r1 = logf - hi.astype(F32)
    mid = r1.astype(BF16)
    lo = (r1 - mid.astype(F32)).astype(BF16)
    bs = []
    for ci in range(n_chunks):
        rows = slice(ci * c, (ci + 1) * c)
        bs.append(jnp.dot(tri_b, hi[rows], preferred_element_type=F32)
                  + (jnp.dot(tri_b, mid[rows], preferred_element_type=F32)
                     + jnp.dot(tri_b, lo[rows], preferred_element_type=F32)))
    b = jnp.concatenate(bs, axis=0)
    kd_f = k * jnp.exp(-b)
    qd = (q * jnp.exp(b)).astype(BF16)
    kd = kd_f.astype(BF16)
    vb = v.astype(BF16)
    gate = ng * (gz * jax.nn.sigmoid(gz))
    decs = [jnp.exp(b[(ci + 1) * c - 1:(ci + 1) * c, :]) for ci in range(n_chunks)]
    kts = [(kd_f[ci * c:(ci + 1) * c] * decs[ci]).astype(BF16) for ci in range(n_chunks)]

    for hd in range(n_heads):
        cs = slice(hd * HG_HEAD_DIM, (hd + 1) * HG_HEAD_DIM)
        st = st_ref[hd]
        for ci in range(n_chunks):
            rows = slice(ci * c, (ci + 1) * c)
            att = lax.dot_general(qd[rows, cs], kd[rows, cs], _NT, preferred_element_type=F32)
            att = jnp.where(tri, att, 0.0)
            o = jnp.dot(att.astype(BF16), vb[rows, cs], preferred_element_type=F32)
            o = o + lax.dot_general(qd[rows, cs], st.astype(BF16), _NT, preferred_element_type=F32)
            ds = lax.dot_general(vb[rows, cs], kts[ci][:, cs], _TN, preferred_element_type=F32)
            st = st * decs[ci][:, cs] + ds
            o = o * lax.rsqrt(jnp.mean(o * o, axis=-1, keepdims=True) + EPS)
            o_ref[rows, cs] = o * gate[rows, cs]
        st_ref[hd] = st


def _mixer_kernel(x_ref, gin_ref, win32_ref, wb_ref, wc_ref, are_ref, aim_ref, d_ref, wglu32_ref, bglu_ref,
                  lb_ref, ng_ref, wup32_ref, wdn32_ref, wout32_ref, ohg_ref, y_ref, wup_ref, wdn_ref, wout_ref,
                  win_ref, wglu_ref, il_ref, bu_ref, pre_ref, pim_ref, carry_ref, hst_ref, *, n_blocks):
    for dst, src in ((wup_ref, wup32_ref), (wdn_ref, wdn32_ref), (wout_ref, wout32_ref)):
        dst[...] = src[...].astype(BF16)
    h = S5_SEG
    s = S5_SEQS
    half = wb_ref.shape[2] // 2
    kw = wb_ref.shape[1]
    bsz, tt, d_model = x_ref.shape
    width = y_ref.shape[-1]

    @pl.when(pl.program_id(0) == 0)
    def _():
        win_ref[...] = win32_ref[...].astype(BF16)
        wglu_ref[...] = wglu32_ref[...].astype(BF16)
        carry_ref[...] = jnp.zeros_like(carry_ref)
        hst_ref[...] = jnp.zeros_like(hst_ref)
        ar = are_ref[...]
        ai = aim_ref[...]
        pre_ref[0:s, :] = ar
        pim_ref[0:s, :] = ai

        def powers(i, c):
            pr, pi = c
            npr = pr * ar - pi * ai
            npi = pr * ai + pi * ar
            r = pl.ds(pl.multiple_of(i * s, s), s)
            pre_ref[r, :] = npr
            pim_ref[r, :] = npi
            return npr, npi

        lax.fori_loop(1, h, powers, (ar, ai))

    xb = _rms(x_ref[...].reshape(bsz * tt, d_model), gin_ref[...]).astype(BF16)
    u_nat = jnp.dot(xb, win_ref[:, 4 * width:5 * width], preferred_element_type=F32)

    n_slabs = il_ref.shape[0]
    for bi in range(bsz):
        for k in range(2):
            r0 = bi * tt + k * h
            for j in range(n_slabs):
                il_ref[j, pl.ds(2 * bi + k, h, stride=s), :] = u_nat[r0:r0 + h, j * LANES:(j + 1) * LANES]
    u = jnp.concatenate([il_ref[j] for j in range(n_slabs)], axis=1)
    ub = u.astype(BF16)
    for m in range(n_blocks):
        bu_ref[:, 2 * half * m:2 * half * (m + 1)] = jnp.dot(
            ub[:, kw * m:kw * (m + 1)], wb_ref[m], preferred_element_type=F32)

    q, fz, v, gz = (jnp.dot(xb, win_ref[:, i * width:(i + 1) * width], preferred_element_type=F32)
                    for i in range(4))
    for bi in range(bsz):
        r = slice(bi * tt, (bi + 1) * tt)
        _hgrn2_rows(q[r], fz[r], v[r], gz[r], lb_ref[...], ng_ref[...], hst_ref.at[bi], ohg_ref.at[bi])

    second = lax.broadcasted_iota(jnp.int32, (s, half), 0) % 2 == 1
    last = slice((h - 1) * s, h * s)

    for m0 in range(0, n_blocks, S5_SCAN_BLOCKS):
        blocks = []
        for m in range(m0, m0 + S5_SCAN_BLOCKS):
            lo = 2 * half * m
            blocks.append((slice(lo, lo + half), slice(lo + half, lo + 2 * half),
                           slice(half * m, half * (m + 1))))
        coef = [(are_ref[:, am], aim_ref[:, am]) for _, _, am in blocks]

        def scan(i, c, blocks=blocks, coef=coef):
            r = pl.ds(pl.multiple_of(i * s, s), s)
            out = []
            for (re, im, _), (ar, ai), (xr, xi) in zip(blocks, coef, c):
                nxr = ar * xr - ai * xi + bu_ref[r, re]
                nxi = ar * xi + ai * xr + bu_ref[r, im]
                bu_ref[r, re] = nxr
                bu_ref[r, im] = nxi
                out.append((nxr, nxi))
            return tuple(out)

        init = tuple((carry_ref[:, re], carry_ref[:, im]) for re, im, _ in blocks)
        ends = lax.fori_loop(0, h, scan, init, unroll=S5_SCAN_UNROLL)
        starts = [(jnp.where(second, pltpu.roll(er, 1, axis=0), 0.0),
                   jnp.where(second, pltpu.roll(ei, 1, axis=0), 0.0)) for er, ei in ends]

        def fix(i, c, blocks=blocks, starts=starts):
            r = pl.ds(pl.multiple_of(i * s, s), s)
            for (re, im, am), (cr, ci) in zip(blocks, starts):
                pr = pre_ref[r, am]
                pi = pim_ref[r, am]
                bu_ref[r, re] = bu_ref[r, re] + (pr * cr - pi * ci)
                bu_ref[r, im] = bu_ref[r, im] + (pr * ci + pi * cr)
            return c

        lax.fori_loop(0, h, fix, 0, unroll=S5_SCAN_UNROLL)
        for re, im, _ in blocks:
            carry_ref[:, re] = jnp.where(second, 0.0, pltpu.roll(bu_ref[last, re], s - 1, axis=0))
            carry_ref[:, im] = jnp.where(second, 0.0, pltpu.roll(bu_ref[last, im], s - 1, axis=0))

    ys = [jnp.dot(bu_ref[:, 2 * half * m:2 * half * (m + 1)].astype(BF16), wc_ref[m],
                  preferred_element_type=F32) for m in range(n_blocks)]
    y = jnp.concatenate(ys, axis=1) + d_ref[...] * u
    y = jax.nn.gelu(y)
    z = jnp.dot(y.astype(BF16), wglu_ref[...], preferred_element_type=F32) + bglu_ref[...]
    y = y * jax.nn.sigmoid(z)
    for j in range(n_slabs):
        il_ref[j] = y[:, j * LANES:(j + 1) * LANES]
    for bi in range(y_ref.shape[0]):
        for k in range(2):
            for j in range(n_slabs):
                y_ref[bi, k * h:(k + 1) * h, j * LANES:(j + 1) * LANES] = il_ref[
                    j, pl.ds(2 * bi + k, h, stride=s), :]


def _ffn_kernel(x_ref, ohg_ref, y_ref, wout_ref, gf_ref, wup_ref, cw_ref, cb_ref, wdn_ref, g_ref, o_ref,
                tail_ref, h_ref, hn_ref, pre_ref, act_ref, *, n_blk):
    @pl.when(pl.program_id(1) == 0)
    def _():
        tail_ref[...] = jnp.zeros_like(tail_ref)

    width = ohg_ref.shape[-1]
    mix = jnp.dot(ohg_ref[...].astype(BF16), wout_ref[0:width, :], preferred_element_type=F32)
    mix = mix + jnp.dot(y_ref[...].astype(BF16), wout_ref[width:2 * width, :], preferred_element_type=F32)
    hres = x_ref[...] + mix
    h_ref[...] = hres
    hn_ref[...] = _rms(hres, gf_ref[...]).astype(BF16)

    tm = act_ref.shape[0]
    bf = pre_ref.shape[-1]
    rows = FFN_ROWS
    top = lax.broadcasted_iota(jnp.int32, (SUBLANES, bf), 0)

    cols = lambda j, part: slice((part * n_blk + j) * bf, (part * n_blk + j + 1) * bf)

    def up(j, slot):
        for part in range(2):
            pre_ref[slot, part, rows:rows + tm] = jnp.dot(hn_ref[...], wup_ref[:, cols(j, part)],
                                                          preferred_element_type=F32)

    def mid(j, slot):
        taps, prev = [], []
        for part in range(2):
            blk = cols(j, part)
            pre_ref[slot, part, 0:rows] = tail_ref[:, blk]
            tail_ref[:, blk] = pre_ref[slot, part, tm:tm + rows]
            cw = cw_ref[:, blk]
            taps.append((cb_ref[:, blk], cw[0:1, :], cw[1:2, :], cw[2:3, :]))
            p = pre_ref[slot, part, 0:rows]
            prev.append((pltpu.roll(p, 1, axis=0), pltpu.roll(p, 2, axis=0)))
        for g in range(tm // rows):
            r0 = rows * (g + 1)
            outs = []
            for part in range(2):
                cur = pre_ref[slot, part, r0:r0 + rows]
                rc = (pltpu.roll(cur, 1, axis=0), pltpu.roll(cur, 2, axis=0))
                sh = [jnp.concatenate([jnp.where(top < d + 1, prev[part][d][0:SUBLANES], rc[d][0:SUBLANES]),
                                       rc[d][SUBLANES:]], axis=0) for d in range(2)]
                prev[part] = rc
                cb, w0, w1, w2 = taps[part]
                outs.append(cb + w0 * sh[1] + w1 * sh[0] + w2 * cur)
            gate, val = outs
            act_ref[rows * g:rows * (g + 1), j * bf:(j + 1) * bf] = (
                (gate * jax.nn.sigmoid(gate)) * val).astype(BF16)

    up(0, 0)
    for j in range(n_blk):
        if j + 1 < n_blk:
            up(j + 1, (j + 1) % 2)
        mid(j, j % 2)
    out = h_ref[...] + jnp.dot(act_ref[...], wdn_ref[...], preferred_element_type=F32)
    o_ref[...] = _rms(out, g_ref[...])


def _tile_sizes(seq_len):
    tm = min(512, seq_len)
    tl = min(256, seq_len)
    assert seq_len % tm == 0 and seq_len % tl == 0 and tm % (2 * S5_SEG) == 0 and tl % HG_CHUNK == 0
    return tm, tl


def kernel(x, in_norm_g, w_in, hg_lb, hg_norm_g, s5_a_re, s5_a_im, s5_log_dt, s5_b_re, s5_b_im,
           s5_c_re, s5_c_im, s5_d, s5_w_glu, s5_b_glu, w_out, ffn_norm_g, w_up, conv_w, conv_b,
           w_down, final_norm_g):
    bsz, seq_len, d_model = x.shape
    depth = w_in.shape[0]
    assert depth == 1, "single-layer block"
    width = hg_norm_g.shape[1]
    assert w_in.shape[2] == 5 * width and s5_d.shape[1] == width
    n_heads = width // HG_HEAD_DIM
    groups, n_state, n_chan = s5_b_re.shape[1:]
    assert (n_state, n_chan) == (S5_STATE, S5_GROUP) and groups * n_chan == width
    assert 2 * bsz == S5_SEQS
    d_ff = w_down.shape[1]
    tm, tl = _tile_sizes(seq_len)
    n_tiles = seq_len // (2 * S5_SEG)
    row = lambda a: a.reshape(1, -1)

    gp = groups * n_chan
    rep = lambda a: jnp.repeat(a, n_chan, axis=0)
    ldt = jnp.broadcast_to(jnp.repeat(s5_log_dt[0], n_chan)[:, None], (gp, n_state))
    b_t = lambda a: a.transpose(0, 2, 1).reshape(gp, n_state)
    lb, abr, abi, bbr, bbi = pl.pallas_call(
        _param_kernel,
        out_shape=[jax.ShapeDtypeStruct((1, width), F32)] + [jax.ShapeDtypeStruct((gp, n_state), F32)] * 4,
        name="params",
    )(hg_lb, rep(s5_a_re[0]), rep(s5_a_im[0]), ldt, b_t(s5_b_re[0]), b_t(s5_b_im[0]))

    gpb = LANES // n_chan
    n_blocks = groups // gpb
    eye = jnp.eye(gpb, dtype=F32)
    blockdiag_in = lambda w: jnp.einsum("mgpn,gh->mgphn", w.reshape(n_blocks, gpb, n_chan, n_state),
                                        eye).reshape(n_blocks, gpb * n_chan, gpb * n_state)
    blockdiag_out = lambda w: jnp.einsum("mgpn,gh->mhngp", w.reshape(n_blocks, gpb, n_chan, n_state),
                                         eye).reshape(n_blocks, gpb * n_state, gpb * n_chan)
    wb = jnp.concatenate([blockdiag_in(bbr), blockdiag_in(bbi)], axis=2).astype(BF16)
    wc = jnp.concatenate([blockdiag_out(s5_c_re[0]), -blockdiag_out(s5_c_im[0])], axis=1).astype(BF16)
    n_lanes = groups * n_state
    a_rows = lambda a: jnp.broadcast_to(a.reshape(groups, n_chan, n_state)[:, 0, :].reshape(1, n_lanes),
                                        (S5_SEQS, n_lanes))
    a_re8, a_im8 = a_rows(abr), a_rows(abi)

    tok = lambda w: pl.BlockSpec((None, tm, w), lambda b, t: (b, t, 0))
    const2 = lambda shape: pl.BlockSpec(shape, lambda b, t: (0,) * len(shape))
    resident = lambda shape: pl.BlockSpec(shape, lambda b, t: (0,) * len(shape), pipeline_mode=pl.Buffered(1))

    m_rows = S5_SEG * S5_SEQS
    c1 = lambda shape: pl.BlockSpec(shape, lambda j: (0,) * len(shape))
    res1 = lambda shape: pl.BlockSpec(shape, lambda j: (0,) * len(shape), pipeline_mode=pl.Buffered(1))
    tile_spec = lambda w: pl.BlockSpec((bsz, 2 * S5_SEG, w), lambda j: (0, j, 0))
    act = lambda dt: jax.ShapeDtypeStruct((bsz, seq_len, width), dt)

    def slab(w):
        n = n_tiles
        while w.shape[0] % n or (w.shape[0] // n) % (2 * SUBLANES):
            n //= 2
        return pl.BlockSpec((w.shape[0] // n, w.shape[1]), lambda j: (j // (n_tiles // n), 0))

    ride = (w_up[0], w_down[0], w_out[0])
    o_hg, y_s5, w_up_b, w_down_b, w_out_b = pl.pallas_call(
        functools.partial(_mixer_kernel, n_blocks=n_blocks),
        grid=(n_tiles,),
        in_specs=[tile_spec(d_model), c1((1, d_model)), res1((d_model, 5 * width)),
                  c1(wb.shape), c1(wc.shape), c1(a_re8.shape), c1(a_im8.shape),
                  c1((1, width)), res1((width, width)), c1((1, width)), c1((1, width)), c1((1, width))]
        + [slab(w) for w in ride],
        out_specs=[tile_spec(width)] * 2 + [slab(w) for w in ride],
        out_shape=[act(F32)] * 2 + [jax.ShapeDtypeStruct(w.shape, BF16) for w in ride],
        scratch_shapes=[pltpu.VMEM((d_model, 5 * width), BF16), pltpu.VMEM((width, width), BF16),
                        pltpu.VMEM((width // LANES, m_rows, LANES), F32), pltpu.VMEM((m_rows, 2 * n_lanes), F32),
                        pltpu.VMEM((m_rows, n_lanes), F32), pltpu.VMEM((m_rows, n_lanes), F32),
                        pltpu.VMEM((S5_SEQS, 2 * n_lanes), F32),
                        pltpu.VMEM((bsz, n_heads, HG_HEAD_DIM, HG_HEAD_DIM), F32)],
        compiler_params=_cparams("arbitrary"),
        name="mixer",
    )(x, row(in_norm_g[0]), w_in[0], wb, wc, a_re8, a_im8, row(s5_d[0]), s5_w_glu[0], row(s5_b_glu[0]),
      lb, row(hg_norm_g[0]), *ride)

    bf = FFN_BLOCK
    assert d_ff % bf == 0
    n_blk = d_ff // bf
    out = pl.pallas_call(
        functools.partial(_ffn_kernel, n_blk=n_blk),
        grid=(bsz, seq_len // tm),
        in_specs=[tok(d_model), tok(width), tok(width), resident((2 * width, d_model)), const2((1, d_model)),
                  resident((d_model, 2 * d_ff)), const2((CONV_WIDTH, 2 * d_ff)),
                  const2((1, 2 * d_ff)), resident((d_ff, d_model)), const2((1, d_model))],
        out_specs=tok(d_model),
        out_shape=jax.ShapeDtypeStruct((bsz, seq_len, d_model), F32),
        scratch_shapes=[pltpu.VMEM((FFN_ROWS, 2 * d_ff), F32), pltpu.VMEM((tm, d_model), F32),
                        pltpu.VMEM((tm, d_model), BF16), pltpu.VMEM((2, 2, tm + FFN_ROWS, bf), F32),
                        pltpu.VMEM((tm, d_ff), BF16)],
        compiler_params=_cparams("arbitrary", "arbitrary"),
        name="ffn",
    )(x, o_hg, y_s5, w_out_b, row(ffn_norm_g[0]), w_up_b, conv_w[0], row(conv_b[0]),
      w_down_b, row(final_norm_g))
    return out
```

```python
import functools

import jax
import jax.numpy as jnp
from jax import lax
from jax.experimental import pallas as pl
from jax.experimental.pallas import tpu as pltpu

EPS = 1e-6
HG_CHUNK = 64
HG_HEAD_DIM = 128
S5_GROUP = 16
S5_STATE = 64
S5_SEG = 64
S5_SEQS = 8
S5_SCAN_BLOCKS = 2
S5_SCAN_UNROLL = True
CONV_WIDTH = 3
FFN_BLOCK = 256
FFN_ROWS = 16
SUBLANES = 8
LANES = 128
VMEM_LIMIT = 60 * 1024 * 1024

F32 = jnp.float32
BF16 = jnp.bfloat16

_NT = (((1,), (1,)), ((), ()))
_TN = (((0,), (0,)), ((), ()))


def _rms(x, g):
    ms = jnp.mean(x * x, axis=-1, keepdims=True)
    return x * lax.rsqrt(ms + EPS) * g


def _cparams(*sem):
    return pltpu.CompilerParams(dimension_semantics=sem, vmem_limit_bytes=VMEM_LIMIT)


def _param_kernel(lb_ref, are_ref, aim_ref, ldt_ref, bre_ref, bim_ref,
                  lb_out, abr_out, abi_out, bbr_out, bbi_out):
    z = lb_ref[...]
    e = jnp.exp(z - jnp.max(z, axis=0, keepdims=True))
    sm = e / jnp.sum(e, axis=0, keepdims=True)
    lb_out[...] = sm[0:1, :]
    dt = jnp.exp(ldt_ref[...])
    ar = are_ref[...]
    ai = aim_ref[...]
    mag = jnp.exp(dt * ar)
    abr = mag * jnp.cos(dt * ai)
    abi = mag * jnp.sin(dt * ai)
    nr = abr - 1.0
    den = ar * ar + ai * ai
    zr = (nr * ar + abi * ai) / den
    zi = (abi * ar - nr * ai) / den
    br = bre_ref[...]
    bi = bim_ref[...]
    abr_out[...] = abr
    abi_out[...] = abi
    bbr_out[...] = zr * br - zi * bi
    bbi_out[...] = zr * bi + zi * br


def _hgrn2_rows(q, fz, v, gz, lb, ng, st_ref, o_ref):
    c = HG_CHUNK
    n_chunks = q.shape[0] // c
    n_heads = st_ref.shape[0]
    tri = lax.broadcasted_iota(jnp.int32, (c, c), 0) >= lax.broadcasted_iota(jnp.int32, (c, c), 1)
    tri_b = tri.astype(BF16)

    f = lb + (1.0 - lb) * jax.nn.sigmoid(fz)
    k = 1.0 - f
    logf = jnp.log(f)
    hi = logf.astype(BF16)
    r1 = logf - hi.astype(F32)
    mid = r1.astype(BF16)
    lo = (r1 - mid.astype(F32)).astype(BF16)
    bs = []
    for ci in range(n_chunks):
        rows = slice(ci * c, (ci + 1) * c)
        bs.append(jnp.dot(tri_b, hi[rows], preferred_element_type=F32)
                  + (jnp.dot(tri_b, mid[rows], preferred_element_type=F32)
                     + jnp.dot(tri_b, lo[rows], preferred_element_type=F32)))
    b = jnp.concatenate(bs, axis=0)
    kd_f = k * jnp.exp(-b)
    qd = (q * jnp.exp(b)).astype(BF16)
    kd = kd_f.astype(BF16)
    vb = v.astype(BF16)
    gate = ng * (gz * jax.nn.sigmoid(gz))
    decs = [jnp.exp(b[(ci + 1) * c - 1:(ci + 1) * c, :]) for ci in range(n_chunks)]
    kts = [(kd_f[ci * c:(ci + 1) * c] * decs[ci]).astype(BF16) for ci in range(n_chunks)]

    for hd in range(n_heads):
        cs = slice(hd * HG_HEAD_DIM, (hd + 1) * HG_HEAD_DIM)
        st = st_ref[hd]
        for ci in range(n_chunks):
            rows = slice(ci * c, (ci + 1) * c)
            att = lax.dot_general(qd[rows, cs], kd[rows, cs], _NT, preferred_element_type=F32)
            att = jnp.where(tri, att, 0.0)
            lhs = jnp.concatenate([qd[rows, cs], att.astype(BF16)], axis=1)
            rhs = jnp.concatenate([st.astype(BF16), vb[rows, cs]], axis=0)
            o = jnp.dot(lhs, rhs, preferred_element_type=F32)
            ds = lax.dot_general(kts[ci][:, cs], vb[rows, cs], _TN, preferred_element_type=F32)
            dcol = jnp.transpose(jnp.broadcast_to(decs[ci][:, cs], (HG_HEAD_DIM, HG_HEAD_DIM)))
            st = st * dcol + ds
            o = o * lax.rsqrt(jnp.mean(o * o, axis=-1, keepdims=True) + EPS)
            o_ref[rows, cs] = o * gate[rows, cs]
        st_ref[hd] = st


def _mixer_kernel(x_ref, gin_ref, win32_ref, wb_ref, wc_ref, are_ref, aim_ref, d_ref, wglu32_ref, bglu_ref,
                  lb_ref, ng_ref, wup32_ref, ohg_ref, y_ref, wup_ref,
                  win_ref, wglu_ref, il_ref, bu_ref, pre_ref, pim_ref, carry_ref, hst_ref, *, n_blocks):
    wup_ref[...] = wup32_ref[...].astype(BF16)
    h = S5_SEG
    s = S5_SEQS
    half = wb_ref.shape[2] // 2
    kw = wb_ref.shape[1]
    bsz, tt, d_model = x_ref.shape
    width = y_ref.shape[-1]

    @pl.when(pl.program_id(0) == 0)
    def _():
        win_ref[...] = win32_ref[...].astype(BF16)
        wglu_ref[...] = wglu32_ref[...].astype(BF16)
        carry_ref[...] = jnp.zeros_like(carry_ref)
        hst_ref[...] = jnp.zeros_like(hst_ref)
        ar = are_ref[...]
        ai = aim_ref[...]
        pre_ref[0:s, :] = ar
        pim_ref[0:s, :] = ai

        def powers(i, c):
            pr, pi = c
            npr = pr * ar - pi * ai
            npi = pr * ai + pi * ar
            r = pl.ds(pl.multiple_of(i * s, s), s)
            pre_ref[r, :] = npr
            pim_ref[r, :] = npi
            return npr, npi

        lax.fori_loop(1, h, powers, (ar, ai))

    xb = _rms(x_ref[...].reshape(bsz * tt, d_model), gin_ref[...]).astype(BF16)
    u_nat = jnp.dot(xb, win_ref[:, 4 * width:5 * width], preferred_element_type=F32)

    n_slabs = il_ref.shape[0]
    for bi in range(bsz):
        for k in range(2):
            r0 = bi * tt + k * h
            for j in range(n_slabs):
                il_ref[j, pl.ds(2 * bi + k, h, stride=s), :] = u_nat[r0:r0 + h, j * LANES:(j + 1) * LANES]
    u = jnp.concatenate([il_ref[j] for j in range(n_slabs)], axis=1)
    ub = u.astype(BF16)
    for m in range(n_blocks):
        bu_ref[:, 2 * half * m:2 * half * (m + 1)] = jnp.dot(
            ub[:, kw * m:kw * (m + 1)], wb_ref[m], preferred_element_type=F32)

    q, fz, v, gz = (jnp.dot(xb, win_ref[:, i * width:(i + 1) * width], preferred_element_type=F32)
                    for i in range(4))
    for bi in range(bsz):
        r = slice(bi * tt, (bi + 1) * tt)
        _hgrn2_rows(q[r], fz[r], v[r], gz[r], lb_ref[...], ng_ref[...], hst_ref.at[bi], ohg_ref.at[bi])

    second = lax.broadcasted_iota(jnp.int32, (s, half), 0) % 2 == 1
    last = slice((h - 1) * s, h * s)

    for m0 in range(0, n_blocks, S5_SCAN_BLOCKS):
        blocks = []
        for m in range(m0, m0 + S5_SCAN_BLOCKS):
            lo = 2 * half * m
            blocks.append((slice(lo, lo + half), slice(lo + half, lo + 2 * half),
                           slice(half * m, half * (m + 1))))
        coef = [(are_ref[:, am], aim_ref[:, am]) for _, _, am in blocks]

        def scan(i, c, blocks=blocks, coef=coef):
            r = pl.ds(pl.multiple_of(i * s, s), s)
            out = []
            for (re, im, _), (ar, ai), (xr, xi) in zip(blocks, coef, c):
                nxr = ar * xr - ai * xi + bu_ref[r, re]
                nxi = ar * xi + ai * xr + bu_ref[r, im]
                bu_ref[r, re] = nxr
                bu_ref[r, im] = nxi
                out.append((nxr, nxi))
            return tuple(out)

        init = tuple((carry_ref[:, re], carry_ref[:, im]) for re, im, _ in blocks)
        ends = lax.fori_loop(0, h, scan, init, unroll=S5_SCAN_UNROLL)
        starts = [(jnp.where(second, pltpu.roll(er, 1, axis=0), 0.0),
                   jnp.where(second, pltpu.roll(ei, 1, axis=0), 0.0)) for er, ei in ends]

        def fix(i, c, blocks=blocks, starts=starts):
            r = pl.ds(pl.multiple_of(i * s, s), s)
            for (re, im, am), (cr, ci) in zip(blocks, starts):
                pr = pre_ref[r, am]
                pi = pim_ref[r, am]
                bu_ref[r, re] = bu_ref[r, re] + (pr * cr - pi * ci)
                bu_ref[r, im] = bu_ref[r, im] + (pr * ci + pi * cr)
            return c

        lax.fori_loop(0, h, fix, 0, unroll=S5_SCAN_UNROLL)
        for re, im, _ in blocks:
            carry_ref[:, re] = jnp.where(second, 0.0, pltpu.roll(bu_ref[last, re], s - 1, axis=0))
            carry_ref[:, im] = jnp.where(second, 0.0, pltpu.roll(bu_ref[last, im], s - 1, axis=0))

    ys = [jnp.dot(bu_ref[:, 2 * half * m:2 * half * (m + 1)].astype(BF16), wc_ref[m],
                  preferred_element_type=F32) for m in range(n_blocks)]
    y = jnp.concatenate(ys, axis=1) + d_ref[...] * u
    y = jax.nn.gelu(y)
    z = jnp.dot(y.astype(BF16), wglu_ref[...], preferred_element_type=F32) + bglu_ref[...]
    y = y * jax.nn.sigmoid(z)
    for j in range(n_slabs):
        il_ref[j] = y[:, j * LANES:(j + 1) * LANES]
    for bi in range(y_ref.shape[0]):
        for k in range(2):
            for j in range(n_slabs):
                y_ref[bi, k * h:(k + 1) * h, j * LANES:(j + 1) * LANES] = il_ref[
                    j, pl.ds(2 * bi + k, h, stride=s), :]


def _ffn_kernel(x_ref, ohg_ref, y_ref, wout32_ref, gf_ref, wup_ref, cw_ref, cb_ref, wdn32_ref, g_ref, o_ref,
                wout_ref, wdn_ref, tail_ref, h_ref, hn_ref, pre_ref, act_ref, *, n_blk):
    @pl.when((pl.program_id(0) == 0) & (pl.program_id(1) == 0))
    def _():
        wout_ref[...] = wout32_ref[...].astype(BF16)
        wdn_ref[...] = wdn32_ref[...].astype(BF16)

    @pl.when(pl.program_id(1) == 0)
    def _():
        tail_ref[...] = jnp.zeros_like(tail_ref)

    width = ohg_ref.shape[-1]
    mix = jnp.dot(ohg_ref[...].astype(BF16), wout_ref[0:width, :], preferred_element_type=F32)
    mix = mix + jnp.dot(y_ref[...].astype(BF16), wout_ref[width:2 * width, :], preferred_element_type=F32)
    hres = x_ref[...] + mix
    h_ref[...] = hres
    hn_ref[...] = _rms(hres, gf_ref[...]).astype(BF16)

    tm = act_ref.shape[0]
    bf = pre_ref.shape[-1]
    rows = FFN_ROWS
    top = lax.broadcasted_iota(jnp.int32, (SUBLANES, bf), 0)

    cols = lambda j, part: slice((part * n_blk + j) * bf, (part * n_blk + j + 1) * bf)

    def up(j, slot):
        for part in range(2):
            pre_ref[slot, part, rows:rows + tm] = jnp.dot(hn_ref[...], wup_ref[:, cols(j, part)],
                                                          preferred_element_type=F32)

    def mid(j, slot):
        taps, prev = [], []
        for part in range(2):
            blk = cols(j, part)
            pre_ref[slot, part, 0:rows] = tail_ref[:, blk]
            tail_ref[:, blk] = pre_ref[slot, part, tm:tm + rows]
            cw = cw_ref[:, blk]
            taps.append((cb_ref[:, blk], cw[0:1, :], cw[1:2, :], cw[2:3, :]))
            p = pre_ref[slot, part, 0:rows]
            prev.append((pltpu.roll(p, 1, axis=0), pltpu.roll(p, 2, axis=0)))
        for g in range(tm // rows):
            r0 = rows * (g + 1)
            outs = []
            for part in range(2):
                cur = pre_ref[slot, part, r0:r0 + rows]
                rc = (pltpu.roll(cur, 1, axis=0), pltpu.roll(cur, 2, axis=0))
                sh = [jnp.concatenate([jnp.where(top < d + 1, prev[part][d][0:SUBLANES], rc[d][0:SUBLANES]),
                                       rc[d][SUBLANES:]], axis=0) for d in range(2)]
                prev[part] = rc
                cb, w0, w1, w2 = taps[part]
                outs.append(cb + w0 * sh[1] + w1 * sh[0] + w2 * cur)
            gate, val = outs
            act_ref[rows * g:rows * (g + 1), j * bf:(j + 1) * bf] = (
                (gate * jax.nn.sigmoid(gate)) * val).astype(BF16)

    up(0, 0)
    for j in range(n_blk):
        if j + 1 < n_blk:
            up(j + 1, (j + 1) % 2)
        mid(j, j % 2)
    out = h_ref[...] + jnp.dot(act_ref[...], wdn_ref[...], preferred_element_type=F32)
    o_ref[...] = _rms(out, g_ref[...])


def _tile_sizes(seq_len):
    tm = min(512, seq_len)
    tl = min(256, seq_len)
    assert seq_len % tm == 0 and seq_len % tl == 0 and tm % (2 * S5_SEG) == 0 and tl % HG_CHUNK == 0
    return tm, tl


def kernel(x, in_norm_g, w_in, hg_lb, hg_norm_g, s5_a_re, s5_a_im, s5_log_dt, s5_b_re, s5_b_im,
           s5_c_re, s5_c_im, s5_d, s5_w_glu, s5_b_glu, w_out, ffn_norm_g, w_up, conv_w, conv_b,
           w_down, final_norm_g):
    bsz, seq_len, d_model = x.shape
    depth = w_in.shape[0]
    assert depth == 1, "single-layer block"
    width = hg_norm_g.shape[1]
    assert w_in.shape[2] == 5 * width and s5_d.shape[1] == width
    n_heads = width // HG_HEAD_DIM
    groups, n_state, n_chan = s5_b_re.shape[1:]
    assert (n_state, n_chan) == (S5_STATE, S5_GROUP) and groups * n_chan == width
    assert 2 * bsz == S5_SEQS
    d_ff = w_down.shape[1]
    tm, tl = _tile_sizes(seq_len)
    n_tiles = seq_len // (2 * S5_SEG)
    row = lambda a: a.reshape(1, -1)

    gp = groups * n_chan
    rep = lambda a: jnp.repeat(a, n_chan, axis=0)
    ldt = jnp.broadcast_to(jnp.repeat(s5_log_dt[0], n_chan)[:, None], (gp, n_state))
    b_t = lambda a: a.transpose(0, 2, 1).reshape(gp, n_state)
    lb, abr, abi, bbr, bbi = pl.pallas_call(
        _param_kernel,
        out_shape=[jax.ShapeDtypeStruct((1, width), F32)] + [jax.ShapeDtypeStruct((gp, n_state), F32)] * 4,
        name="params",
    )(hg_lb, rep(s5_a_re[0]), rep(s5_a_im[0]), ldt, b_t(s5_b_re[0]), b_t(s5_b_im[0]))

    gpb = LANES // n_chan
    n_blocks = groups // gpb
    eye = jnp.eye(gpb, dtype=F32)
    blockdiag_in = lambda w: jnp.einsum("mgpn,gh->mgphn", w.reshape(n_blocks, gpb, n_chan, n_state),
                                        eye).reshape(n_blocks, gpb * n_chan, gpb * n_state)
    blockdiag_out = lambda w: jnp.einsum("mgpn,gh->mhngp", w.reshape(n_blocks, gpb, n_chan, n_state),
                                         eye).reshape(n_blocks, gpb * n_state, gpb * n_chan)
    wb = jnp.concatenate([blockdiag_in(bbr), blockdiag_in(bbi)], axis=2).astype(BF16)
    wc = jnp.concatenate([blockdiag_out(s5_c_re[0]), -blockdiag_out(s5_c_im[0])], axis=1).astype(BF16)
    n_lanes = groups * n_state
    a_rows = lambda a: jnp.broadcast_to(a.reshape(groups, n_chan, n_state)[:, 0, :].reshape(1, n_lanes),
                                        (S5_SEQS, n_lanes))
    a_re8, a_im8 = a_rows(abr), a_rows(abi)

    tok = lambda w: pl.BlockSpec((None, tm, w), lambda b, t: (b, t, 0))
    const2 = lambda shape: pl.BlockSpec(shape, lambda b, t: (0,) * len(shape))
    resident = lambda shape: pl.BlockSpec(shape, lambda b, t: (0,) * len(shape), pipeline_mode=pl.Buffered(1))

    m_rows = S5_SEG * S5_SEQS
    c1 = lambda shape: pl.BlockSpec(shape, lambda j: (0,) * len(shape))
    res1 = lambda shape: pl.BlockSpec(shape, lambda j: (0,) * len(shape), pipeline_mode=pl.Buffered(1))
    tile_spec = lambda w: pl.BlockSpec((bsz, 2 * S5_SEG, w), lambda j: (0, j, 0))
    act = lambda dt: jax.ShapeDtypeStruct((bsz, seq_len, width), dt)
    assert d_model % n_tiles == 0
    wup_slab = pl.BlockSpec((d_model // n_tiles, 2 * d_ff), lambda j: (j, 0))
    o_hg, y_s5, w_up_bf16 = pl.pallas_call(
        functools.partial(_mixer_kernel, n_blocks=n_blocks),
        grid=(n_tiles,),
        in_specs=[tile_spec(d_model), c1((1, d_model)), res1((d_model, 5 * width)),
                  c1(wb.shape), c1(wc.shape), c1(a_re8.shape), c1(a_im8.shape),
                  c1((1, width)), res1((width, width)), c1((1, width)), c1((1, width)), c1((1, width)),
                  wup_slab],
        out_specs=[tile_spec(width)] * 2 + [wup_slab],
        out_shape=[act(F32)] * 2 + [jax.ShapeDtypeStruct((d_model, 2 * d_ff), BF16)],
        scratch_shapes=[pltpu.VMEM((d_model, 5 * width), BF16), pltpu.VMEM((width, width), BF16),
                        pltpu.VMEM((width // LANES, m_rows, LANES), F32), pltpu.VMEM((m_rows, 2 * n_lanes), F32),
                        pltpu.VMEM((m_rows, n_lanes), F32), pltpu.VMEM((m_rows, n_lanes), F32),
                        pltpu.VMEM((S5_SEQS, 2 * n_lanes), F32),
                        pltpu.VMEM((bsz, n_heads, HG_HEAD_DIM, HG_HEAD_DIM), F32)],
        compiler_params=_cparams("arbitrary"),
        name="mixer",
    )(x, row(in_norm_g[0]), w_in[0], wb, wc, a_re8, a_im8, row(s5_d[0]), s5_w_glu[0], row(s5_b_glu[0]),
      lb, row(hg_norm_g[0]), w_up[0])

    bf = FFN_BLOCK
    assert d_ff % bf == 0
    n_blk = d_ff // bf
    out = pl.pallas_call(
        functools.partial(_ffn_kernel, n_blk=n_blk),
        grid=(bsz, seq_len // tm),
        in_specs=[tok(d_model), tok(width), tok(width), resident((2 * width, d_model)), const2((1, d_model)),
                  resident((d_model, 2 * d_ff)), const2((CONV_WIDTH, 2 * d_ff)),
                  const2((1, 2 * d_ff)), resident((d_ff, d_model)), const2((1, d_model))],
        out_specs=tok(d_model),
        out_shape=jax.ShapeDtypeStruct((bsz, seq_len, d_model), F32),
        scratch_shapes=[pltpu.VMEM((2 * width, d_model), BF16), pltpu.VMEM((d_ff, d_model), BF16),
                        pltpu.VMEM((FFN_ROWS, 2 * d_ff), F32), pltpu.VMEM((tm, d_model), F32),
                        pltpu.VMEM((tm, d_model), BF16), pltpu.VMEM((2, 2, tm + FFN_ROWS, bf), F32),
                        pltpu.VMEM((tm, d_ff), BF16)],
        compiler_params=_cparams("arbitrary", "arbitrary"),
        name="ffn",
    )(x, o_hg, y_s5, w_out[0], row(ffn_norm_g[0]), w_up_bf16, conv_w[0], row(conv_b[0]),
      w_down[0], row(final_norm_g))
    return out
```

```python
import functools

import jax
import jax.numpy as jnp
from jax import lax
from jax.experimental import pallas as pl
from jax.experimental.pallas import tpu as pltpu

EPS = 1e-6
HG_CHUNK = 64
HG_HEAD_DIM = 128
S5_GROUP = 16
S5_STATE = 64
S5_SEG = 64
S5_SEQS = 8
S5_SCAN_BLOCKS = 2
S5_SCAN_UNROLL = True
CONV_WIDTH = 3
FFN_BLOCK = 256
FFN_ROWS = 16
SUBLANES = 8
LANES = 128
VMEM_LIMIT = 60 * 1024 * 1024

F32 = jnp.float32
BF16 = jnp.bfloat16

_NT = (((1,), (1,)), ((), ()))
_TN = (((0,), (0,)), ((), ()))


def _rms(x, g):
    ms = jnp.mean(x * x, axis=-1, keepdims=True)
    return x * lax.rsqrt(ms + EPS) * g


def _cparams(*sem):
    return pltpu.CompilerParams(dimension_semantics=sem, vmem_limit_bytes=VMEM_LIMIT)


def _param_kernel(lb_ref, are_ref, aim_ref, ldt_ref, bre_ref, bim_ref,
                  lb_out, abr_out, abi_out, bbr_out, bbi_out):
    z = lb_ref[...]
    e = jnp.exp(z - jnp.max(z, axis=0, keepdims=True))
    sm = e / jnp.sum(e, axis=0, keepdims=True)
    lb_out[...] = sm[0:1, :]
    dt = jnp.exp(ldt_ref[...])
    ar = are_ref[...]
    ai = aim_ref[...]
    mag = jnp.exp(dt * ar)
    abr = mag * jnp.cos(dt * ai)
    abi = mag * jnp.sin(dt * ai)
    nr = abr - 1.0
    den = ar * ar + ai * ai
    zr = (nr * ar + abi * ai) / den
    zi = (abi * ar - nr * ai) / den
    br = bre_ref[...]
    bi = bim_ref[...]
    abr_out[...] = abr
    abi_out[...] = abi
    bbr_out[...] = zr * br - zi * bi
    bbi_out[...] = zr * bi + zi * br


def _cumsum_rows(x):
    n, w = x.shape
    top = lax.broadcasted_iota(jnp.int32, (SUBLANES, w), 0)
    sh = 1
    while sh < n:
        if sh < SUBLANES:
            r = pltpu.roll(x, sh, axis=0)
            r = jnp.concatenate([jnp.where(top >= sh, r[0:SUBLANES], 0.0), r[SUBLANES:]], axis=0)
        else:
            r = jnp.concatenate([jnp.zeros((sh, w), x.dtype), x[:n - sh]], axis=0)
        x = x + r
        sh *= 2
    return x


def _hgrn2_rows(q, fz, v, gz, lb, ng, st_ref, o_ref):
    c = HG_CHUNK
    n_chunks = q.shape[0] // c
    n_heads = st_ref.shape[0]
    tri = lax.broadcasted_iota(jnp.int32, (c, c), 0) >= lax.broadcasted_iota(jnp.int32, (c, c), 1)

    f = lb + (1.0 - lb) * jax.nn.sigmoid(fz)
    k = 1.0 - f
    logf = jnp.log(f)
    b = jnp.concatenate([_cumsum_rows(logf[ci * c:(ci + 1) * c]) for ci in range(n_chunks)], axis=0)
    kd_f = k * jnp.exp(-b)
    qd = (q * jnp.exp(b)).astype(BF16)
    kd = kd_f.astype(BF16)
    vb = v.astype(BF16)
    gate = ng * (gz * jax.nn.sigmoid(gz))
    decs = [jnp.exp(b[(ci + 1) * c - 1:(ci + 1) * c, :]) for ci in range(n_chunks)]
    kts = [(kd_f[ci * c:(ci + 1) * c] * decs[ci]).astype(BF16) for ci in range(n_chunks)]

    for hd in range(n_heads):
        cs = slice(hd * HG_HEAD_DIM, (hd + 1) * HG_HEAD_DIM)
        st = st_ref[hd]
        for ci in range(n_chunks):
            rows = slice(ci * c, (ci + 1) * c)
            att = lax.dot_general(qd[rows, cs], kd[rows, cs], _NT, preferred_element_type=F32)
            att = jnp.where(tri, att, 0.0)
            lhs = jnp.concatenate([qd[rows, cs], att.astype(BF16)], axis=1)
            rhs = jnp.concatenate([st.astype(BF16), vb[rows, cs]], axis=0)
            o = jnp.dot(lhs, rhs, preferred_element_type=F32)
            ds = lax.dot_general(kts[ci][:, cs], vb[rows, cs], _TN, preferred_element_type=F32)
            dcol = jnp.transpose(jnp.broadcast_to(decs[ci][:, cs], (HG_HEAD_DIM, HG_HEAD_DIM)))
            st = st * dcol + ds
            o = o * lax.rsqrt(jnp.mean(o * o, axis=-1, keepdims=True) + EPS)
            o_ref[rows, cs] = o * gate[rows, cs]
        st_ref[hd] = st


def _mixer_kernel(x_ref, gin_ref, win32_ref, wb_ref, wc_ref, are_ref, aim_ref, d_ref, wglu32_ref, bglu_ref,
                  lb_ref, ng_ref, wup32_ref, ohg_ref, y_ref, wup_ref,
                  win_ref, wglu_ref, il_ref, bu_ref, pre_ref, pim_ref, carry_ref, hst_ref, *, n_blocks):
    wup_ref[...] = wup32_ref[...].astype(BF16)
    h = S5_SEG
    s = S5_SEQS
    half = wb_ref.shape[2] // 2
    kw = wb_ref.shape[1]
    bsz, tt, d_model = x_ref.shape
    width = y_ref.shape[-1]

    @pl.when(pl.program_id(0) == 0)
    def _():
        win_ref[...] = win32_ref[...].astype(BF16)
        wglu_ref[...] = wglu32_ref[...].astype(BF16)
        carry_ref[...] = jnp.zeros_like(carry_ref)
        hst_ref[...] = jnp.zeros_like(hst_ref)
        ar = are_ref[...]
        ai = aim_ref[...]
        pre_ref[0:s, :] = ar
        pim_ref[0:s, :] = ai

        def powers(i, c):
            pr, pi = c
            npr = pr * ar - pi * ai
            npi = pr * ai + pi * ar
            r = pl.ds(pl.multiple_of(i * s, s), s)
            pre_ref[r, :] = npr
            pim_ref[r, :] = npi
            return npr, npi

        lax.fori_loop(1, h, powers, (ar, ai))

    xb = _rms(x_ref[...].reshape(bsz * tt, d_model), gin_ref[...]).astype(BF16)
    u_nat = jnp.dot(xb, win_ref[:, 4 * width:5 * width], preferred_element_type=F32)

    n_slabs = il_ref.shape[0]
    for bi in range(bsz):
        for k in range(2):
            r0 = bi * tt + k * h
            for j in range(n_slabs):
                il_ref[j, pl.ds(2 * bi + k, h, stride=s), :] = u_nat[r0:r0 + h, j * LANES:(j + 1) * LANES]
    u = jnp.concatenate([il_ref[j] for j in range(n_slabs)], axis=1)
    ub = u.astype(BF16)
    for m in range(n_blocks):
        bu_ref[:, 2 * half * m:2 * half * (m + 1)] = jnp.dot(
            ub[:, kw * m:kw * (m + 1)], wb_ref[m], preferred_element_type=F32)

    q, fz, v, gz = (jnp.dot(xb, win_ref[:, i * width:(i + 1) * width], preferred_element_type=F32)
                    for i in range(4))
    for bi in range(bsz):
        r = slice(bi * tt, (bi + 1) * tt)
        _hgrn2_rows(q[r], fz[r], v[r], gz[r], lb_ref[...], ng_ref[...], hst_ref.at[bi], ohg_ref.at[bi])

    second = lax.broadcasted_iota(jnp.int32, (s, half), 0) % 2 == 1
    last = slice((h - 1) * s, h * s)

    for m0 in range(0, n_blocks, S5_SCAN_BLOCKS):
        blocks = []
        for m in range(m0, m0 + S5_SCAN_BLOCKS):
            lo = 2 * half * m
            blocks.append((slice(lo, lo + half), slice(lo + half, lo + 2 * half),
                           slice(half * m, half * (m + 1))))
        coef = [(are_ref[:, am], aim_ref[:, am]) for _, _, am in blocks]

        def scan(i, c, blocks=blocks, coef=coef):
            r = pl.ds(pl.multiple_of(i * s, s), s)
            out = []
            for (re, im, _), (ar, ai), (xr, xi) in zip(blocks, coef, c):
                nxr = ar * xr - ai * xi + bu_ref[r, re]
                nxi = ar * xi + ai * xr + bu_ref[r, im]
                bu_ref[r, re] = nxr
                bu_ref[r, im] = nxi
                out.append((nxr, nxi))
            return tuple(out)

        init = tuple((carry_ref[:, re], carry_ref[:, im]) for re, im, _ in blocks)
        ends = lax.fori_loop(0, h, scan, init, unroll=S5_SCAN_UNROLL)
        starts = [(jnp.where(second, pltpu.roll(er, 1, axis=0), 0.0),
                   jnp.where(second, pltpu.roll(ei, 1, axis=0), 0.0)) for er, ei in ends]

        def fix(i, c, blocks=blocks, starts=starts):
            r = pl.ds(pl.multiple_of(i * s, s), s)
            for (re, im, am), (cr, ci) in zip(blocks, starts):
                pr = pre_ref[r, am]
                pi = pim_ref[r, am]
                bu_ref[r, re] = bu_ref[r, re] + (pr * cr - pi * ci)
                bu_ref[r, im] = bu_ref[r, im] + (pr * ci + pi * cr)
            return c

        lax.fori_loop(0, h, fix, 0, unroll=S5_SCAN_UNROLL)
        for re, im, _ in blocks:
            carry_ref[:, re] = jnp.where(second, 0.0, pltpu.roll(bu_ref[last, re], s - 1, axis=0))
            carry_ref[:, im] = jnp.where(second, 0.0, pltpu.roll(bu_ref[last, im], s - 1, axis=0))

    ys = [jnp.dot(bu_ref[:, 2 * half * m:2 * half * (m + 1)].astype(BF16), wc_ref[m],
                  preferred_element_type=F32) for m in range(n_blocks)]
    y = jnp.concatenate(ys, axis=1) + d_ref[...] * u
    y = jax.nn.gelu(y)
    z = jnp.dot(y.astype(BF16), wglu_ref[...], preferred_element_type=F32) + bglu_ref[...]
    y = y * jax.nn.sigmoid(z)
    for j in range(n_slabs):
        il_ref[j] = y[:, j * LANES:(j + 1) * LANES]
    for bi in range(y_ref.shape[0]):
        for k in range(2):
            for j in range(n_slabs):
                y_ref[bi, k * h:(k + 1) * h, j * LANES:(j + 1) * LANES] = il_ref[
                    j, pl.ds(2 * bi + k, h, stride=s), :]


def _ffn_kernel(x_ref, ohg_ref, y_ref, wout32_ref, gf_ref, wup_ref, cw_ref, cb_ref, wdn32_ref, g_ref, o_ref,
                wout_ref, wdn_ref, tail_ref, h_ref, hn_ref, pre_ref, act_ref, *, n_blk):
    @pl.when((pl.program_id(0) == 0) & (pl.program_id(1) == 0))
    def _():
        wout_ref[...] = wout32_ref[...].astype(BF16)
        wdn_ref[...] = wdn32_ref[...].astype(BF16)

    @pl.when(pl.program_id(1) == 0)
    def _():
        tail_ref[...] = jnp.zeros_like(tail_ref)

    width = ohg_ref.shape[-1]
    mix = jnp.dot(ohg_ref[...].astype(BF16), wout_ref[0:width, :], preferred_element_type=F32)
    mix = mix + jnp.dot(y_ref[...].astype(BF16), wout_ref[width:2 * width, :], preferred_element_type=F32)
    hres = x_ref[...] + mix
    h_ref[...] = hres
    hn_ref[...] = _rms(hres, gf_ref[...]).astype(BF16)

    tm = act_ref.shape[0]
    bf = pre_ref.shape[-1]
    rows = FFN_ROWS
    top = lax.broadcasted_iota(jnp.int32, (SUBLANES, bf), 0)

    cols = lambda j, part: slice((part * n_blk + j) * bf, (part * n_blk + j + 1) * bf)

    def up(j, slot):
        for part in range(2):
            pre_ref[slot, part, rows:rows + tm] = jnp.dot(hn_ref[...], wup_ref[:, cols(j, part)],
                                                          preferred_element_type=F32)

    def mid(j, slot):
        taps, prev = [], []
        for part in range(2):
            blk = cols(j, part)
            pre_ref[slot, part, 0:rows] = tail_ref[:, blk]
            tail_ref[:, blk] = pre_ref[slot, part, tm:tm + rows]
            cw = cw_ref[:, blk]
            taps.append((cb_ref[:, blk], cw[0:1, :], cw[1:2, :], cw[2:3, :]))
            p = pre_ref[slot, part, 0:rows]
            prev.append((pltpu.roll(p, 1, axis=0), pltpu.roll(p, 2, axis=0)))
        for g in range(tm // rows):
            r0 = rows * (g + 1)
            outs = []
            for part in range(2):
                cur = pre_ref[slot, part, r0:r0 + rows]
                rc = (pltpu.roll(cur, 1, axis=0), pltpu.roll(cur, 2, axis=0))
                sh = [jnp.concatenate([jnp.where(top < d + 1, prev[part][d][0:SUBLANES], rc[d][0:SUBLANES]),
                                       rc[d][SUBLANES:]], axis=0) for d in range(2)]
                prev[part] = rc
                cb, w0, w1, w2 = taps[part]
                outs.append(cb + w0 * sh[1] + w1 * sh[0] + w2 * cur)
            gate, val = outs
            act_ref[rows * g:rows * (g + 1), j * bf:(j + 1) * bf] = (
                (gate * jax.nn.sigmoid(gate)) * val).astype(BF16)

    up(0, 0)
    for j in range(n_blk):
        if j + 1 < n_blk:
            up(j + 1, (j + 1) % 2)
        mid(j, j % 2)
    out = h_ref[...] + jnp.dot(act_ref[...], wdn_ref[...], preferred_element_type=F32)
    o_ref[...] = _rms(out, g_ref[...])


def _tile_sizes(seq_len):
    tm = min(512, seq_len)
    tl = min(256, seq_len)
    assert seq_len % tm == 0 and seq_len % tl == 0 and tm % (2 * S5_SEG) == 0 and tl % HG_CHUNK == 0
    return tm, tl


def kernel(x, in_norm_g, w_in, hg_lb, hg_norm_g, s5_a_re, s5_a_im, s5_log_dt, s5_b_re, s5_b_im,
           s5_c_re, s5_c_im, s5_d, s5_w_glu, s5_b_glu, w_out, ffn_norm_g, w_up, conv_w, conv_b,
           w_down, final_norm_g):
    bsz, seq_len, d_model = x.shape
    depth = w_in.shape[0]
    assert depth == 1, "single-layer block"
    width = hg_norm_g.shape[1]
    assert w_in.shape[2] == 5 * width and s5_d.shape[1] == width
    n_heads = width // HG_HEAD_DIM
    groups, n_state, n_chan = s5_b_re.shape[1:]
    assert (n_state, n_chan) == (S5_STATE, S5_GROUP) and groups * n_chan == width
    assert 2 * bsz == S5_SEQS
    d_ff = w_down.shape[1]
    tm, tl = _tile_sizes(seq_len)
    n_tiles = seq_len // (2 * S5_SEG)
    row = lambda a: a.reshape(1, -1)

    gp = groups * n_chan
    rep = lambda a: jnp.repeat(a, n_chan, axis=0)
    ldt = jnp.broadcast_to(jnp.repeat(s5_log_dt[0], n_chan)[:, None], (gp, n_state))
    b_t = lambda a: a.transpose(0, 2, 1).reshape(gp, n_state)
    lb, abr, abi, bbr, bbi = pl.pallas_call(
        _param_kernel,
        out_shape=[jax.ShapeDtypeStruct((1, width), F32)] + [jax.ShapeDtypeStruct((gp, n_state), F32)] * 4,
        name="params",
    )(hg_lb, rep(s5_a_re[0]), rep(s5_a_im[0]), ldt, b_t(s5_b_re[0]), b_t(s5_b_im[0]))

    gpb = LANES // n_chan
    n_blocks = groups // gpb
    eye = jnp.eye(gpb, dtype=F32)
    blockdiag_in = lambda w: jnp.einsum("mgpn,gh->mgphn", w.reshape(n_blocks, gpb, n_chan, n_state),
                                        eye).reshape(n_blocks, gpb * n_chan, gpb * n_state)
    blockdiag_out = lambda w: jnp.einsum("mgpn,gh->mhngp", w.reshape(n_blocks, gpb, n_chan, n_state),
                                         eye).reshape(n_blocks, gpb * n_state, gpb * n_chan)
    wb = jnp.concatenate([blockdiag_in(bbr), blockdiag_in(bbi)], axis=2).astype(BF16)
    wc = jnp.concatenate([blockdiag_out(s5_c_re[0]), -blockdiag_out(s5_c_im[0])], axis=1).astype(BF16)
    n_lanes = groups * n_state
    a_rows = lambda a: jnp.broadcast_to(a.reshape(groups, n_chan, n_state)[:, 0, :].reshape(1, n_lanes),
                                        (S5_SEQS, n_lanes))
    a_re8, a_im8 = a_rows(abr), a_rows(abi)

    tok = lambda w: pl.BlockSpec((None, tm, w), lambda b, t: (b, t, 0))
    const2 = lambda shape: pl.BlockSpec(shape, lambda b, t: (0,) * len(shape))
    resident = lambda shape: pl.BlockSpec(shape, lambda b, t: (0,) * len(shape), pipeline_mode=pl.Buffered(1))

    m_rows = S5_SEG * S5_SEQS
    c1 = lambda shape: pl.BlockSpec(shape, lambda j: (0,) * len(shape))
    res1 = lambda shape: pl.BlockSpec(shape, lambda j: (0,) * len(shape), pipeline_mode=pl.Buffered(1))
    tile_spec = lambda w: pl.BlockSpec((bsz, 2 * S5_SEG, w), lambda j: (0, j, 0))
    act = lambda dt: jax.ShapeDtypeStruct((bsz, seq_len, width), dt)
    assert d_model % n_tiles == 0
    wup_slab = pl.BlockSpec((d_model // n_tiles, 2 * d_ff), lambda j: (j, 0))
    o_hg, y_s5, w_up_bf16 = pl.pallas_call(
        functools.partial(_mixer_kernel, n_blocks=n_blocks),
        grid=(n_tiles,),
        in_specs=[tile_spec(d_model), c1((1, d_model)), res1((d_model, 5 * width)),
                  c1(wb.shape), c1(wc.shape), c1(a_re8.shape), c1(a_im8.shape),
                  c1((1, width)), res1((width, width)), c1((1, width)), c1((1, width)), c1((1, width)),
                  wup_slab],
        out_specs=[tile_spec(width)] * 2 + [wup_slab],
        out_shape=[act(F32)] * 2 + [jax.ShapeDtypeStruct((d_model, 2 * d_ff), BF16)],
        scratch_shapes=[pltpu.VMEM((d_model, 5 * width), BF16), pltpu.VMEM((width, width), BF16),
                        pltpu.VMEM((width // LANES, m_rows, LANES), F32), pltpu.VMEM((m_rows, 2 * n_lanes), F32),
                        pltpu.VMEM((m_rows, n_lanes), F32), pltpu.VMEM((m_rows, n_lanes), F32),
                        pltpu.VMEM((S5_SEQS, 2 * n_lanes), F32),
                        pltpu.VMEM((bsz, n_heads, HG_HEAD_DIM, HG_HEAD_DIM), F32)],
        compiler_params=_cparams("arbitrary"),
        name="mixer",
    )(x, row(in_norm_g[0]), w_in[0], wb, wc, a_re8, a_im8, row(s5_d[0]), s5_w_glu[0], row(s5_b_glu[0]),
      lb, row(hg_norm_g[0]), w_up[0])

    bf = FFN_BLOCK
    assert d_ff % bf == 0
    n_blk = d_ff // bf
    out = pl.pallas_call(
        functools.partial(_ffn_kernel, n_blk=n_blk),
        grid=(bsz, seq_len // tm),
        in_specs=[tok(d_model), tok(width), tok(width), resident((2 * width, d_model)), const2((1, d_model)),
                  resident((d_model, 2 * d_ff)), const2((CONV_WIDTH, 2 * d_ff)),
                  const2((1, 2 * d_ff)), resident((d_ff, d_model)), const2((1, d_model))],
        out_specs=tok(d_model),
        out_shape=jax.ShapeDtypeStruct((bsz, seq_len, d_model), F32),
        scratch_shapes=[pltpu.VMEM((2 * width, d_model), BF16), pltpu.VMEM((d_ff, d_model), BF16),
                        pltpu.VMEM((FFN_ROWS, 2 * d_ff), F32), pltpu.VMEM((tm, d_model), F32),
                        pltpu.VMEM((tm, d_model), BF16), pltpu.VMEM((2, 2, tm + FFN_ROWS, bf), F32),
                        pltpu.VMEM((tm, d_ff), BF16)],
        compiler_params=_cparams("arbitrary", "arbitrary"),
        name="ffn",
    )(x, o_hg, y_s5, w_out[0], row(ffn_norm_g[0]), w_up_bf16, conv_w[0], row(conv_b[0]),
      w_down[0], row(final_norm_g))
    return out
```

```python
import functools

import jax
import jax.numpy as jnp
from jax import lax
from jax.experimental import pallas as pl
from jax.experimental.pallas import tpu as pltpu

EPS = 1e-6
HG_CHUNK = 64
HG_HEAD_DIM = 128
S5_GROUP = 16
S5_STATE = 64
S5_SEG = 64
S5_SEQS = 8
S5_SCAN_BLOCKS = 2
S5_SCAN_UNROLL = True
CONV_WIDTH = 3
FFN_BLOCK = 256
FFN_ROWS = 16
SUBLANES = 8
LANES = 128
VMEM_LIMIT = 60 * 1024 * 1024

F32 = jnp.float32
BF16 = jnp.bfloat16

_NT = (((1,), (1,)), ((), ()))
_TN = (((0,), (0,)), ((), ()))


def _rms(x, g):
    ms = jnp.mean(x * x, axis=-1, keepdims=True)
    return x * lax.rsqrt(ms + EPS) * g


def _cparams(*sem):
    return pltpu.CompilerParams(dimension_semantics=sem, vmem_limit_bytes=VMEM_LIMIT)


def _param_kernel(lb_ref, are_ref, aim_ref, ldt_ref, bre_ref, bim_ref,
                  lb_out, abr_out, abi_out, bbr_out, bbi_out):
    z = lb_ref[...]
    e = jnp.exp(z - jnp.max(z, axis=0, keepdims=True))
    sm = e / jnp.sum(e, axis=0, keepdims=True)
    lb_out[...] = sm[0:1, :]
    dt = jnp.exp(ldt_ref[...])
    ar = are_ref[...]
    ai = aim_ref[...]
    mag = jnp.exp(dt * ar)
    abr = mag * jnp.cos(dt * ai)
    abi = mag * jnp.sin(dt * ai)
    nr = abr - 1.0
    den = ar * ar + ai * ai
    zr = (nr * ar + abi * ai) / den
    zi = (abi * ar - nr * ai) / den
    br = bre_ref[...]
    bi = bim_ref[...]
    abr_out[...] = abr
    abi_out[...] = abi
    bbr_out[...] = zr * br - zi * bi
    bbi_out[...] = zr * bi + zi * br


def _cumsum_rows(x):
    n, w = x.shape
    top = lax.broadcasted_iota(jnp.int32, (SUBLANES, w), 0)
    sh = 1
    while sh < n:
        if sh < SUBLANES:
            r = pltpu.roll(x, sh, axis=0)
            r = jnp.concatenate([jnp.where(top >= sh, r[0:SUBLANES], 0.0), r[SUBLANES:]], axis=0)
        else:
            r = jnp.concatenate([jnp.zeros((sh, w), x.dtype), x[:n - sh]], axis=0)
        x = x + r
        sh *= 2
    return x


def _hgrn2_rows(q, fz, v, gz, lb, ng, st_ref, o_ref):
    c = HG_CHUNK
    n_chunks = q.shape[0] // c
    n_heads = st_ref.shape[0]
    tri = lax.broadcasted_iota(jnp.int32, (c, c), 0) >= lax.broadcasted_iota(jnp.int32, (c, c), 1)

    f = lb + (1.0 - lb) * jax.nn.sigmoid(fz)
    k = 1.0 - f
    logf = jnp.log(f)
    b = jnp.concatenate([_cumsum_rows(logf[ci * c:(ci + 1) * c]) for ci in range(n_chunks)], axis=0)
    kd_f = k * jnp.exp(-b)
    qd = (q * jnp.exp(b)).astype(BF16)
    kd = kd_f.astype(BF16)
    vb = v.astype(BF16)
    gate = ng * (gz * jax.nn.sigmoid(gz))
    decs = [jnp.exp(b[(ci + 1) * c - 1:(ci + 1) * c, :]) for ci in range(n_chunks)]
    kts = [(kd_f[ci * c:(ci + 1) * c] * decs[ci]).astype(BF16) for ci in range(n_chunks)]

    for hd in range(n_heads):
        cs = slice(hd * HG_HEAD_DIM, (hd + 1) * HG_HEAD_DIM)
        st = st_ref[hd]
        for ci in range(n_chunks):
            rows = slice(ci * c, (ci + 1) * c)
            att = lax.dot_general(qd[rows, cs], kd[rows, cs], _NT, preferred_element_type=F32)
            att = jnp.where(tri, att, 0.0)
            lhs = jnp.concatenate([qd[rows, cs], att.astype(BF16)], axis=1)
            rhs = jnp.concatenate([st.astype(BF16), vb[rows, cs]], axis=0)
            o = jnp.dot(lhs, rhs, preferred_element_type=F32)
            ds = lax.dot_general(kts[ci][:, cs], vb[rows, cs], _TN, preferred_element_type=F32)
            dcol = jnp.transpose(jnp.broadcast_to(decs[ci][:, cs], (HG_HEAD_DIM, HG_HEAD_DIM)))
            st = st * dcol + ds
            o = o * lax.rsqrt(jnp.mean(o * o, axis=-1, keepdims=True) + EPS)
            o_ref[rows, cs] = o * gate[rows, cs]
        st_ref[hd] = st


def _mixer_kernel(x_ref, gin_ref, win32_ref, wb_ref, wc_ref, are_ref, aim_ref, d_ref, wglu32_ref, bglu_ref,
                  lb_ref, ng_ref, wup32_ref, ohg_ref, y_ref, wup_ref,
                  win_ref, wglu_ref, il_ref, bu_ref, pre_ref, pim_ref, carry_ref, hst_ref, *, n_blocks):
    wup_ref[...] = wup32_ref[...].astype(BF16)
    h = S5_SEG
    s = S5_SEQS
    half = wb_ref.shape[2] // 2
    kw = wb_ref.shape[1]
    bsz, tt, d_model = x_ref.shape
    width = y_ref.shape[-1]

    @pl.when(pl.program_id(0) == 0)
    def _():
        win_ref[...] = win32_ref[...].astype(BF16)
        wglu_ref[...] = wglu32_ref[...].astype(BF16)
        carry_ref[...] = jnp.zeros_like(carry_ref)
        hst_ref[...] = jnp.zeros_like(hst_ref)
        ar = are_ref[...]
        ai = aim_ref[...]
        pre_ref[0:s, :] = ar
        pim_ref[0:s, :] = ai

        def powers(i, c):
            pr, pi = c
            npr = pr * ar - pi * ai
            npi = pr * ai + pi * ar
            r = pl.ds(pl.multiple_of(i * s, s), s)
            pre_ref[r, :] = npr
            pim_ref[r, :] = npi
            return npr, npi

        lax.fori_loop(1, h, powers, (ar, ai))

    xb = _rms(x_ref[...].reshape(bsz * tt, d_model), gin_ref[...]).astype(BF16)
    u_nat = jnp.dot(xb, win_ref[:, 4 * width:5 * width], preferred_element_type=F32)

    n_slabs = il_ref.shape[0]
    for bi in range(bsz):
        for k in range(2):
            r0 = bi * tt + k * h
            for j in range(n_slabs):
                il_ref[j, pl.ds(2 * bi + k, h, stride=s), :] = u_nat[r0:r0 + h, j * LANES:(j + 1) * LANES]
    u = jnp.concatenate([il_ref[j] for j in range(n_slabs)], axis=1)
    ub = u.astype(BF16)
    for m in range(n_blocks):
        bu_ref[:, 2 * half * m:2 * half * (m + 1)] = jnp.dot(
            ub[:, kw * m:kw * (m + 1)], wb_ref[m], preferred_element_type=F32)

    q, fz, v, gz = (jnp.dot(xb, win_ref[:, i * width:(i + 1) * width], preferred_element_type=F32)
                    for i in range(4))
    for bi in range(bsz):
        r = slice(bi * tt, (bi + 1) * tt)
        _hgrn2_rows(q[r], fz[r], v[r], gz[r], lb_ref[...], ng_ref[...], hst_ref.at[bi], ohg_ref.at[bi])

    second = lax.broadcasted_iota(jnp.int32, (s, half), 0) % 2 == 1
    last = slice((h - 1) * s, h * s)

    for m0 in range(0, n_blocks, S5_SCAN_BLOCKS):
        blocks = []
        for m in range(m0, m0 + S5_SCAN_BLOCKS):
            lo = 2 * half * m
            blocks.append((slice(lo, lo + half), slice(lo + half, lo + 2 * half),
                           slice(half * m, half * (m + 1))))
        coef = [(are_ref[:, am], aim_ref[:, am]) for _, _, am in blocks]

        def scan(i, c, blocks=blocks, coef=coef):
            r = pl.ds(pl.multiple_of(i * s, s), s)
            out = []
            for (re, im, _), (ar, ai), (xr, xi) in zip(blocks, coef, c):
                nxr = ar * xr - ai * xi + bu_ref[r, re]
                nxi = ar * xi + ai * xr + bu_ref[r, im]
                bu_ref[r, re] = nxr
                bu_ref[r, im] = nxi
                out.append((nxr, nxi))
            return tuple(out)

        init = tuple((carry_ref[:, re], carry_ref[:, im]) for re, im, _ in blocks)
        ends = lax.fori_loop(0, h, scan, init, unroll=S5_SCAN_UNROLL)
        starts = [(jnp.where(second, pltpu.roll(er, 1, axis=0), 0.0),
                   jnp.where(second, pltpu.roll(ei, 1, axis=0), 0.0)) for er, ei in ends]

        def fix(i, c, blocks=blocks, starts=starts):
            r = pl.ds(pl.multiple_of(i * s, s), s)
            for (re, im, am), (cr, ci) in zip(blocks, starts):
                pr = pre_ref[r, am]
                pi = pim_ref[r, am]
                bu_ref[r, re] = bu_ref[r, re] + (pr * cr - pi * ci)
                bu_ref[r, im] = bu_ref[r, im] + (pr * ci + pi * cr)
            return c

        lax.fori_loop(0, h, fix, 0, unroll=S5_SCAN_UNROLL)
        for re, im, _ in blocks:
            carry_ref[:, re] = jnp.where(second, 0.0, pltpu.roll(bu_ref[last, re], s - 1, axis=0))
            carry_ref[:, im] = jnp.where(second, 0.0, pltpu.roll(bu_ref[last, im], s - 1, axis=0))

    ys = [jnp.dot(bu_ref[:, 2 * half * m:2 * half * (m + 1)].astype(BF16), wc_ref[m],
                  preferred_element_type=F32) for m in range(n_blocks)]
    y = jnp.concatenate(ys, axis=1) + d_ref[...] * u
    y = jax.nn.gelu(y)
    z = jnp.dot(y.astype(BF16), wglu_ref[...], preferred_element_type=F32) + bglu_ref[...]
    y = y * jax.nn.sigmoid(z)
    for j in range(n_slabs):
        il_ref[j] = y[:, j * LANES:(j + 1) * LANES]
    for bi in range(y_ref.shape[0]):
        for k in range(2):
            for j in range(n_slabs):
                y_ref[bi, k * h:(k + 1) * h, j * LANES:(j + 1) * LANES] = il_ref[
                    j, pl.ds(2 * bi + k, h, stride=s), :]


def _ffn_kernel(x_ref, ohg_ref, y_ref, wout32_ref, gf_ref, wup_ref, cw_ref, cb_ref, wdn32_ref, g_ref, o_ref,
                wout_ref, wdn_ref, tail_ref, h_ref, hn_ref, pre_ref, act_ref, *, n_blk):
    @pl.when((pl.program_id(0) == 0) & (pl.program_id(1) == 0))
    def _():
        wout_ref[...] = wout32_ref[...].astype(BF16)
        wdn_ref[...] = wdn32_ref[...].astype(BF16)

    @pl.when(pl.program_id(1) == 0)
    def _():
        tail_ref[...] = jnp.zeros_like(tail_ref)

    mixed = jnp.concatenate([ohg_ref[...].astype(BF16), y_ref[...].astype(BF16)], axis=1)
    hres = x_ref[...] + jnp.dot(mixed, wout_ref[...], preferred_element_type=F32)
    h_ref[...] = hres
    hn_ref[...] = _rms(hres, gf_ref[...]).astype(BF16)

    tm = act_ref.shape[0]
    bf = pre_ref.shape[-1]
    rows = FFN_ROWS
    top = lax.broadcasted_iota(jnp.int32, (SUBLANES, bf), 0)

    cols = lambda j, part: slice((part * n_blk + j) * bf, (part * n_blk + j + 1) * bf)

    def up(j, slot):
        for part in range(2):
            pre_ref[slot, part, rows:rows + tm] = jnp.dot(hn_ref[...], wup_ref[:, cols(j, part)],
                                                          preferred_element_type=F32)

    def mid(j, slot):
        taps, prev = [], []
        for part in range(2):
            blk = cols(j, part)
            pre_ref[slot, part, 0:rows] = tail_ref[:, blk]
            tail_ref[:, blk] = pre_ref[slot, part, tm:tm + rows]
            cw = cw_ref[:, blk]
            taps.append((cb_ref[:, blk], cw[0:1, :], cw[1:2, :], cw[2:3, :]))
            p = pre_ref[slot, part, 0:rows]
            prev.append((pltpu.roll(p, 1, axis=0), pltpu.roll(p, 2, axis=0)))
        for g in range(tm // rows):
            r0 = rows * (g + 1)
            outs = []
            for part in range(2):
                cur = pre_ref[slot, part, r0:r0 + rows]
                rc = (pltpu.roll(cur, 1, axis=0), pltpu.roll(cur, 2, axis=0))
                sh = [jnp.concatenate([jnp.where(top < d + 1, prev[part][d][0:SUBLANES], rc[d][0:SUBLANES]),
                                       rc[d][SUBLANES:]], axis=0) for d in range(2)]
                prev[part] = rc
                cb, w0, w1, w2 = taps[part]
                outs.append(cb + w0 * sh[1] + w1 * sh[0] + w2 * cur)
            gate, val = outs
            act_ref[rows * g:rows * (g + 1), j * bf:(j + 1) * bf] = (
                (gate * jax.nn.sigmoid(gate)) * val).astype(BF16)

    up(0, 0)
    for j in range(n_blk):
        if j + 1 < n_blk:
            up(j + 1, (j + 1) % 2)
        mid(j, j % 2)
    out = h_ref[...] + jnp.dot(act_ref[...], wdn_ref[...], preferred_element_type=F32)
    o_ref[...] = _rms(out, g_ref[...])


def _tile_sizes(seq_len):
    tm = min(512, seq_len)
    tl = min(256, seq_len)
    assert seq_len % tm == 0 and seq_len % tl == 0 and tm % (2 * S5_SEG) == 0 and tl % HG_CHUNK == 0
    return tm, tl


def kernel(x, in_norm_g, w_in, hg_lb, hg_norm_g, s5_a_re, s5_a_im, s5_log_dt, s5_b_re, s5_b_im,
           s5_c_re, s5_c_im, s5_d, s5_w_glu, s5_b_glu, w_out, ffn_norm_g, w_up, conv_w, conv_b,
           w_down, final_norm_g):
    bsz, seq_len, d_model = x.shape
    depth = w_in.shape[0]
    assert depth == 1, "single-layer block"
    width = hg_norm_g.shape[1]
    assert w_in.shape[2] == 5 * width and s5_d.shape[1] == width
    n_heads = width // HG_HEAD_DIM
    groups, n_state, n_chan = s5_b_re.shape[1:]
    assert (n_state, n_chan) == (S5_STATE, S5_GROUP) and groups * n_chan == width
    assert 2 * bsz == S5_SEQS
    d_ff = w_down.shape[1]
    tm, tl = _tile_sizes(seq_len)
    n_tiles = seq_len // (2 * S5_SEG)
    row = lambda a: a.reshape(1, -1)

    gp = groups * n_chan
    rep = lambda a: jnp.repeat(a, n_chan, axis=0)
    ldt = jnp.broadcast_to(jnp.repeat(s5_log_dt[0], n_chan)[:, None], (gp, n_state))
    b_t = lambda a: a.transpose(0, 2, 1).reshape(gp, n_state)
    lb, abr, abi, bbr, bbi = pl.pallas_call(
        _param_kernel,
        out_shape=[jax.ShapeDtypeStruct((1, width), F32)] + [jax.ShapeDtypeStruct((gp, n_state), F32)] * 4,
        name="params",
    )(hg_lb, rep(s5_a_re[0]), rep(s5_a_im[0]), ldt, b_t(s5_b_re[0]), b_t(s5_b_im[0]))

    gpb = LANES // n_chan
    n_blocks = groups // gpb
    eye = jnp.eye(gpb, dtype=F32)
    blockdiag_in = lambda w: jnp.einsum("mgpn,gh->mgphn", w.reshape(n_blocks, gpb, n_chan, n_state),
                                        eye).reshape(n_blocks, gpb * n_chan, gpb * n_state)
    blockdiag_out = lambda w: jnp.einsum("mgpn,gh->mhngp", w.reshape(n_blocks, gpb, n_chan, n_state),
                                         eye).reshape(n_blocks, gpb * n_state, gpb * n_chan)
    wb = jnp.concatenate([blockdiag_in(bbr), blockdiag_in(bbi)], axis=2).astype(BF16)
    wc = jnp.concatenate([blockdiag_out(s5_c_re[0]), -blockdiag_out(s5_c_im[0])], axis=1).astype(BF16)
    n_lanes = groups * n_state
    a_rows = lambda a: jnp.broadcast_to(a.reshape(groups, n_chan, n_state)[:, 0, :].reshape(1, n_lanes),
                                        (S5_SEQS, n_lanes))
    a_re8, a_im8 = a_rows(abr), a_rows(abi)

    tok = lambda w: pl.BlockSpec((None, tm, w), lambda b, t: (b, t, 0))
    const2 = lambda shape: pl.BlockSpec(shape, lambda b, t: (0,) * len(shape))
    resident = lambda shape: pl.BlockSpec(shape, lambda b, t: (0,) * len(shape), pipeline_mode=pl.Buffered(1))

    m_rows = S5_SEG * S5_SEQS
    c1 = lambda shape: pl.BlockSpec(shape, lambda j: (0,) * len(shape))
    res1 = lambda shape: pl.BlockSpec(shape, lambda j: (0,) * len(shape), pipeline_mode=pl.Buffered(1))
    tile_spec = lambda w: pl.BlockSpec((bsz, 2 * S5_SEG, w), lambda j: (0, j, 0))
    act = lambda dt: jax.ShapeDtypeStruct((bsz, seq_len, width), dt)
    assert d_model % n_tiles == 0
    wup_slab = pl.BlockSpec((d_model // n_tiles, 2 * d_ff), lambda j: (j, 0))
    o_hg, y_s5, w_up_bf16 = pl.pallas_call(
        functools.partial(_mixer_kernel, n_blocks=n_blocks),
        grid=(n_tiles,),
        in_specs=[tile_spec(d_model), c1((1, d_model)), res1((d_model, 5 * width)),
                  c1(wb.shape), c1(wc.shape), c1(a_re8.shape), c1(a_im8.shape),
                  c1((1, width)), res1((width, width)), c1((1, width)), c1((1, width)), c1((1, width)),
                  wup_slab],
        out_specs=[tile_spec(width)] * 2 + [wup_slab],
        out_shape=[act(F32)] * 2 + [jax.ShapeDtypeStruct((d_model, 2 * d_ff), BF16)],
        scratch_shapes=[pltpu.VMEM((d_model, 5 * width), BF16), pltpu.VMEM((width, width), BF16),
                        pltpu.VMEM((width // LANES, m_rows, LANES), F32), pltpu.VMEM((m_rows, 2 * n_lanes), F32),
                        pltpu.VMEM((m_rows, n_lanes), F32), pltpu.VMEM((m_rows, n_lanes), F32),
                        pltpu.VMEM((S5_SEQS, 2 * n_lanes), F32),
                        pltpu.VMEM((bsz, n_heads, HG_HEAD_DIM, HG_HEAD_DIM), F32)],
        compiler_params=_cparams("arbitrary"),
        name="mixer",
    )(x, row(in_norm_g[0]), w_in[0], wb, wc, a_re8, a_im8, row(s5_d[0]), s5_w_glu[0], row(s5_b_glu[0]),
      lb, row(hg_norm_g[0]), w_up[0])

    bf = FFN_BLOCK
    assert d_ff % bf == 0
    n_blk = d_ff // bf
    out = pl.pallas_call(
        functools.partial(_ffn_kernel, n_blk=n_blk),
        grid=(bsz, seq_len // tm),
        in_specs=[tok(d_model), tok(width), tok(width), resident((2 * width, d_model)), const2((1, d_model)),
                  resident((d_model, 2 * d_ff)), const2((CONV_WIDTH, 2 * d_ff)),
                  const2((1, 2 * d_ff)), resident((d_ff, d_model)), const2((1, d_model))],
        out_specs=tok(d_model),
        out_shape=jax.ShapeDtypeStruct((bsz, seq_len, d_model), F32),
        scratch_shapes=[pltpu.VMEM((2 * width, d_model), BF16), pltpu.VMEM((d_ff, d_model), BF16),
                        pltpu.VMEM((FFN_ROWS, 2 * d_ff), F32), pltpu.VMEM((tm, d_model), F32),
                        pltpu.VMEM((tm, d_model), BF16), pltpu.VMEM((2, 2, tm + FFN_ROWS, bf), F32),
                        pltpu.VMEM((tm, d_ff), BF16)],
        compiler_params=_cparams("arbitrary", "arbitrary"),
        name="ffn",
    )(x, o_hg, y_s5, w_out[0], row(ffn_norm_g[0]), w_up_bf16, conv_w[0], row(conv_b[0]),
      w_down[0], row(final_norm_g))
    return out
```

```python
import functools

import jax
import jax.numpy as jnp
from jax import lax
from jax.experimental import pallas as pl
from jax.experimental.pallas import tpu as pltpu

EPS = 1e-6
HG_CHUNK = 64
HG_HEAD_DIM = 128
S5_GROUP = 16
S5_STATE = 64
S5_SEG = 64
S5_SEQS = 8
S5_SCAN_BLOCKS = 2
S5_SCAN_UNROLL = True
CONV_WIDTH = 3
FFN_BLOCK = 256
FFN_ROWS = 16
SUBLANES = 8
LANES = 128
VMEM_LIMIT = 60 * 1024 * 1024

F32 = jnp.float32
BF16 = jnp.bfloat16

_NT = (((1,), (1,)), ((), ()))
_TN = (((0,), (0,)), ((), ()))


def _rms(x, g):
    ms = jnp.mean(x * x, axis=-1, keepdims=True)
    return x * lax.rsqrt(ms + EPS) * g


def _cparams(*sem):
    return pltpu.CompilerParams(dimension_semantics=sem, vmem_limit_bytes=VMEM_LIMIT)


def _param_kernel(lb_ref, are_ref, aim_ref, ldt_ref, bre_ref, bim_ref,
                  lb_out, abr_out, abi_out, bbr_out, bbi_out):
    z = lb_ref[...]
    e = jnp.exp(z - jnp.max(z, axis=0, keepdims=True))
    sm = e / jnp.sum(e, axis=0, keepdims=True)
    lb_out[...] = sm[0:1, :]
    dt = jnp.exp(ldt_ref[...])
    ar = are_ref[...]
    ai = aim_ref[...]
    mag = jnp.exp(dt * ar)
    abr = mag * jnp.cos(dt * ai)
    abi = mag * jnp.sin(dt * ai)
    nr = abr - 1.0
    den = ar * ar + ai * ai
    zr = (nr * ar + abi * ai) / den
    zi = (abi * ar - nr * ai) / den
    br = bre_ref[...]
    bi = bim_ref[...]
    abr_out[...] = abr
    abi_out[...] = abi
    bbr_out[...] = zr * br - zi * bi
    bbi_out[...] = zr * bi + zi * br


def _cumsum_rows(x):
    n, w = x.shape
    top = lax.broadcasted_iota(jnp.int32, (SUBLANES, w), 0)
    sh = 1
    while sh < n:
        if sh < SUBLANES:
            r = pltpu.roll(x, sh, axis=0)
            r = jnp.concatenate([jnp.where(top >= sh, r[0:SUBLANES], 0.0), r[SUBLANES:]], axis=0)
        else:
            r = jnp.concatenate([jnp.zeros((sh, w), x.dtype), x[:n - sh]], axis=0)
        x = x + r
        sh *= 2
    return x


def _hgrn2_rows(q, fz, v, gz, lb, ng, st_ref, o_ref):
    c = HG_CHUNK
    n_chunks = q.shape[0] // c
    n_heads = st_ref.shape[0]
    tri = lax.broadcasted_iota(jnp.int32, (c, c), 0) >= lax.broadcasted_iota(jnp.int32, (c, c), 1)

    f = lb + (1.0 - lb) * jax.nn.sigmoid(fz)
    k = 1.0 - f
    logf = jnp.log(f)
    b = jnp.concatenate([_cumsum_rows(logf[ci * c:(ci + 1) * c]) for ci in range(n_chunks)], axis=0)
    kd_f = k * jnp.exp(-b)
    qd = (q * jnp.exp(b)).astype(BF16)
    kd = kd_f.astype(BF16)
    vb = v.astype(BF16)
    gate = ng * (gz * jax.nn.sigmoid(gz))
    decs = [jnp.exp(b[(ci + 1) * c - 1:(ci + 1) * c, :]) for ci in range(n_chunks)]
    kts = [(kd_f[ci * c:(ci + 1) * c] * decs[ci]).astype(BF16) for ci in range(n_chunks)]

    for hd in range(n_heads):
        cs = slice(hd * HG_HEAD_DIM, (hd + 1) * HG_HEAD_DIM)
        st = st_ref[hd]
        for ci in range(n_chunks):
            rows = slice(ci * c, (ci + 1) * c)
            att = lax.dot_general(qd[rows, cs], kd[rows, cs], _NT, preferred_element_type=F32)
            att = jnp.where(tri, att, 0.0)
            lhs = jnp.concatenate([qd[rows, cs], att.astype(BF16)], axis=1)
            rhs = jnp.concatenate([st.astype(BF16), vb[rows, cs]], axis=0)
            o = jnp.dot(lhs, rhs, preferred_element_type=F32)
            ds = lax.dot_general(kts[ci][:, cs], vb[rows, cs], _TN, preferred_element_type=F32)
            dcol = jnp.transpose(jnp.broadcast_to(decs[ci][:, cs], (HG_HEAD_DIM, HG_HEAD_DIM)))
            st = st * dcol + ds
            o = o * lax.rsqrt(jnp.mean(o * o, axis=-1, keepdims=True) + EPS)
            o_ref[rows, cs] = o * gate[rows, cs]
        st_ref[hd] = st


def _mixer_kernel(x_ref, gin_ref, win32_ref, wb_ref, wc_ref, are_ref, aim_ref, d_ref, wglu32_ref, bglu_ref,
                  lb_ref, ng_ref, wup32_ref, ohg_ref, y_ref, wup_ref,
                  win_ref, wglu_ref, il_ref, bu_ref, pre_ref, pim_ref, carry_ref, hst_ref, *, n_blocks):
    wup_ref[...] = wup32_ref[...].astype(BF16)
    h = S5_SEG
    s = S5_SEQS
    half = wb_ref.shape[2] // 2
    kw = wb_ref.shape[1]
    bsz, tt, d_model = x_ref.shape
    width = y_ref.shape[-1]

    @pl.when(pl.program_id(0) == 0)
    def _():
        win_ref[...] = win32_ref[...].astype(BF16)
        wglu_ref[...] = wglu32_ref[...].astype(BF16)
        carry_ref[...] = jnp.zeros_like(carry_ref)
        hst_ref[...] = jnp.zeros_like(hst_ref)
        ar = are_ref[...]
        ai = aim_ref[...]
        pre_ref[0:s, :] = ar
        pim_ref[0:s, :] = ai

        def powers(i, c):
            pr, pi = c
            npr = pr * ar - pi * ai
            npi = pr * ai + pi * ar
            r = pl.ds(pl.multiple_of(i * s, s), s)
            pre_ref[r, :] = npr
            pim_ref[r, :] = npi
            return npr, npi

        lax.fori_loop(1, h, powers, (ar, ai))

    xb = _rms(x_ref[...].reshape(bsz * tt, d_model), gin_ref[...]).astype(BF16)
    u_nat = jnp.dot(xb, win_ref[:, 4 * width:5 * width], preferred_element_type=F32)

    n_slabs = il_ref.shape[0]
    for bi in range(bsz):
        for k in range(2):
            r0 = bi * tt + k * h
            for j in range(n_slabs):
                il_ref[j, pl.ds(2 * bi + k, h, stride=s), :] = u_nat[r0:r0 + h, j * LANES:(j + 1) * LANES]
    u = jnp.concatenate([il_ref[j] for j in range(n_slabs)], axis=1)
    ub = u.astype(BF16)
    for m in range(n_blocks):
        bu_ref[:, 2 * half * m:2 * half * (m + 1)] = jnp.dot(
            ub[:, kw * m:kw * (m + 1)], wb_ref[m], preferred_element_type=F32)

    q, fz, v, gz = (jnp.dot(xb, win_ref[:, i * width:(i + 1) * width], preferred_element_type=F32)
                    for i in range(4))
    for bi in range(bsz):
        r = slice(bi * tt, (bi + 1) * tt)
        _hgrn2_rows(q[r], fz[r], v[r], gz[r], lb_ref[...], ng_ref[...], hst_ref.at[bi], ohg_ref.at[bi])

    second = lax.broadcasted_iota(jnp.int32, (s, half), 0) % 2 == 1
    last = slice((h - 1) * s, h * s)

    for m0 in range(0, n_blocks, S5_SCAN_BLOCKS):
        blocks = []
        for m in range(m0, m0 + S5_SCAN_BLOCKS):
            lo = 2 * half * m
            blocks.append((slice(lo, lo + half), slice(lo + half, lo + 2 * half),
                           slice(half * m, half * (m + 1))))
        coef = [(are_ref[:, am], aim_ref[:, am]) for _, _, am in blocks]

        def scan(i, c, blocks=blocks, coef=coef):
            r = pl.ds(pl.multiple_of(i * s, s), s)
            out = []
            for (re, im, _), (ar, ai), (xr, xi) in zip(blocks, coef, c):
                nxr = ar * xr - ai * xi + bu_ref[r, re]
                nxi = ar * xi + ai * xr + bu_ref[r, im]
                bu_ref[r, re] = nxr
                bu_ref[r, im] = nxi
                out.append((nxr, nxi))
            return tuple(out)

        init = tuple((carry_ref[:, re], carry_ref[:, im]) for re, im, _ in blocks)
        ends = lax.fori_loop(0, h, scan, init, unroll=S5_SCAN_UNROLL)
        starts = [(jnp.where(second, pltpu.roll(er, 1, axis=0), 0.0),
                   jnp.where(second, pltpu.roll(ei, 1, axis=0), 0.0)) for er, ei in ends]

        def fix(i, c, blocks=blocks, starts=starts):
            r = pl.ds(pl.multiple_of(i * s, s), s)
            for (re, im, am), (cr, ci) in zip(blocks, starts):
                pr = pre_ref[r, am]
                pi = pim_ref[r, am]
                bu_ref[r, re] = bu_ref[r, re] + (pr * cr - pi * ci)
                bu_ref[r, im] = bu_ref[r, im] + (pr * ci + pi * cr)
            return c

        lax.fori_loop(0, h, fix, 0, unroll=S5_SCAN_UNROLL)
        for re, im, _ in blocks:
            carry_ref[:, re] = jnp.where(second, 0.0, pltpu.roll(bu_ref[last, re], s - 1, axis=0))
            carry_ref[:, im] = jnp.where(second, 0.0, pltpu.roll(bu_ref[last, im], s - 1, axis=0))

    ys = [jnp.dot(bu_ref[:, 2 * half * m:2 * half * (m + 1)].astype(BF16), wc_ref[m],
                  preferred_element_type=F32) for m in range(n_blocks)]
    y = jnp.concatenate(ys, axis=1) + d_ref[...] * u
    y = jax.nn.gelu(y)
    z = jnp.dot(y.astype(BF16), wglu_ref[...], preferred_element_type=F32) + bglu_ref[...]
    y = y * jax.nn.sigmoid(z)
    for j in range(n_slabs):
        il_ref[j] = y[:, j * LANES:(j + 1) * LANES]
    for bi in range(y_ref.shape[0]):
        for k in range(2):
            for j in range(n_slabs):
                y_ref[bi, k * h:(k + 1) * h, j * LANES:(j + 1) * LANES] = il_ref[
                    j, pl.ds(2 * bi + k, h, stride=s), :]


def _ffn_kernel(x_ref, ohg_ref, y_ref, wout32_ref, gf_ref, wup_ref, cw_ref, cb_ref, wdn32_ref, g_ref, o_ref,
                wout_ref, wdn_ref, tail_ref, h_ref, hn_ref, pre_ref, act_ref, *, n_blk):
    @pl.when((pl.program_id(0) == 0) & (pl.program_id(1) == 0))
    def _():
        wout_ref[...] = wout32_ref[...].astype(BF16)
        wdn_ref[...] = wdn32_ref[...].astype(BF16)

    @pl.when(pl.program_id(1) == 0)
    def _():
        tail_ref[...] = jnp.zeros_like(tail_ref)

    mixed = jnp.concatenate([ohg_ref[...].astype(BF16), y_ref[...].astype(BF16)], axis=1)
    hres = x_ref[...] + jnp.dot(mixed, wout_ref[...], preferred_element_type=F32)
    h_ref[...] = hres
    hn_ref[...] = _rms(hres, gf_ref[...]).astype(BF16)

    tm = act_ref.shape[0]
    bf = pre_ref.shape[-1]
    rows = FFN_ROWS
    top = lax.broadcasted_iota(jnp.int32, (SUBLANES, bf), 0)

    cols = lambda j, part: slice((part * n_blk + j) * bf, (part * n_blk + j + 1) * bf)

    def up(j, slot):
        for part in range(2):
            pre_ref[slot, part, rows:rows + tm] = jnp.dot(hn_ref[...], wup_ref[:, cols(j, part)],
                                                          preferred_element_type=F32)

    def mid(j, slot):
        taps, prev = [], []
        for part in range(2):
            blk = cols(j, part)
            pre_ref[slot, part, 0:rows] = tail_ref[:, blk]
            tail_ref[:, blk] = pre_ref[slot, part, tm:tm + rows]
            cw = cw_ref[:, blk]
            taps.append((cb_ref[:, blk], cw[0:1, :], cw[1:2, :], cw[2:3, :]))
            p = pre_ref[slot, part, 0:rows]
            prev.append((pltpu.roll(p, 1, axis=0), pltpu.roll(p, 2, axis=0)))
        for g in range(tm // rows):
            r0 = rows * (g + 1)
            outs = []
            for part in range(2):
                cur = pre_ref[slot, part, r0:r0 + rows]
                rc = (pltpu.roll(cur, 1, axis=0), pltpu.roll(cur, 2, axis=0))
                sh = [jnp.concatenate([jnp.where(top < d + 1, prev[part][d][0:SUBLANES], rc[d][0:SUBLANES]),
                                       rc[d][SUBLANES:]], axis=0) for d in range(2)]
                prev[part] = rc
                cb, w0, w1, w2 = taps[part]
                outs.append(cb + w0 * sh[1] + w1 * sh[0] + w2 * cur)
            gate, val = outs
            act_ref[rows * g:rows * (g + 1), j * bf:(j + 1) * bf] = (
                (gate * jax.nn.sigmoid(gate)) * val).astype(BF16)

    up(0, 0)
    for j in range(n_blk):
        if j + 1 < n_blk:
            up(j + 1, (j + 1) % 2)
        mid(j, j % 2)
    out = h_ref[...] + jnp.dot(act_ref[...], wdn_ref[...], preferred_element_type=F32)
    o_ref[...] = _rms(out, g_ref[...])


def _tile_sizes(seq_len):
    tm = min(512, seq_len)
    tt = 2 * S5_SEG
    assert seq_len % tm == 0 and seq_len % tt == 0 and tt % HG_CHUNK == 0 and tm % FFN_ROWS == 0
    return tm, tt


def kernel(x, in_norm_g, w_in, hg_lb, hg_norm_g, s5_a_re, s5_a_im, s5_log_dt, s5_b_re, s5_b_im,
           s5_c_re, s5_c_im, s5_d, s5_w_glu, s5_b_glu, w_out, ffn_norm_g, w_up, conv_w, conv_b,
           w_down, final_norm_g):
    bsz, seq_len, d_model = x.shape
    depth = w_in.shape[0]
    assert depth == 1, "single-layer block"
    width = hg_norm_g.shape[1]
    assert w_in.shape[2] == 5 * width and s5_d.shape[1] == width
    n_heads = width // HG_HEAD_DIM
    groups, n_state, n_chan = s5_b_re.shape[1:]
    assert (n_state, n_chan) == (S5_STATE, S5_GROUP) and groups * n_chan == width
    assert 2 * bsz == S5_SEQS
    d_ff = w_down.shape[1]
    tm, tt = _tile_sizes(seq_len)
    n_tiles = seq_len // tt
    row = lambda a: a.reshape(1, -1)

    gp = groups * n_chan
    rep = lambda a: jnp.repeat(a, n_chan, axis=0)
    ldt = jnp.broadcast_to(jnp.repeat(s5_log_dt[0], n_chan)[:, None], (gp, n_state))
    b_t = lambda a: a.transpose(0, 2, 1).reshape(gp, n_state)
    lb, abr, abi, bbr, bbi = pl.pallas_call(
        _param_kernel,
        out_shape=[jax.ShapeDtypeStruct((1, width), F32)] + [jax.ShapeDtypeStruct((gp, n_state), F32)] * 4,
        name="params",
    )(hg_lb, rep(s5_a_re[0]), rep(s5_a_im[0]), ldt, b_t(s5_b_re[0]), b_t(s5_b_im[0]))

    gpb = LANES // n_chan
    n_blocks = groups // gpb
    eye = jnp.eye(gpb, dtype=F32)
    blockdiag_in = lambda w: jnp.einsum("mgpn,gh->mgphn", w.reshape(n_blocks, gpb, n_chan, n_state),
                                        eye).reshape(n_blocks, gpb * n_chan, gpb * n_state)
    blockdiag_out = lambda w: jnp.einsum("mgpn,gh->mhngp", w.reshape(n_blocks, gpb, n_chan, n_state),
                                         eye).reshape(n_blocks, gpb * n_state, gpb * n_chan)
    wb = jnp.concatenate([blockdiag_in(bbr), blockdiag_in(bbi)], axis=2).astype(BF16)
    wc = jnp.concatenate([blockdiag_out(s5_c_re[0]), -blockdiag_out(s5_c_im[0])], axis=1).astype(BF16)
    n_lanes = groups * n_state
    a_rows = lambda a: jnp.broadcast_to(a.reshape(groups, n_chan, n_state)[:, 0, :].reshape(1, n_lanes),
                                        (S5_SEQS, n_lanes))
    a_re8, a_im8 = a_rows(abr), a_rows(abi)

    tok = lambda w: pl.BlockSpec((None, tm, w), lambda b, t: (b, t, 0))
    const2 = lambda shape: pl.BlockSpec(shape, lambda b, t: (0,) * len(shape))
    resident = lambda shape: pl.BlockSpec(shape, lambda b, t: (0,) * len(shape), pipeline_mode=pl.Buffered(1))

    m_rows = S5_SEG * S5_SEQS
    c1 = lambda shape: pl.BlockSpec(shape, lambda j: (0,) * len(shape))
    res1 = lambda shape: pl.BlockSpec(shape, lambda j: (0,) * len(shape), pipeline_mode=pl.Buffered(1))
    tile_spec = lambda w: pl.BlockSpec((bsz, tt, w), lambda j: (0, j, 0))
    act = lambda dt: jax.ShapeDtypeStruct((bsz, seq_len, width), dt)
    assert d_model % n_tiles == 0
    wup_slab = pl.BlockSpec((d_model // n_tiles, 2 * d_ff), lambda j: (j, 0))
    o_hg, y_s5, w_up_bf16 = pl.pallas_call(
        functools.partial(_mixer_kernel, n_blocks=n_blocks),
        grid=(n_tiles,),
        in_specs=[tile_spec(d_model), c1((1, d_model)), res1((d_model, 5 * width)),
                  c1(wb.shape), c1(wc.shape), c1(a_re8.shape), c1(a_im8.shape),
                  c1((1, width)), res1((width, width)), c1((1, width)), c1((1, width)), c1((1, width)),
                  wup_slab],
        out_specs=[tile_spec(width)] * 2 + [wup_slab],
        out_shape=[act(F32)] * 2 + [jax.ShapeDtypeStruct((d_model, 2 * d_ff), BF16)],
        scratch_shapes=[pltpu.VMEM((d_model, 5 * width), BF16), pltpu.VMEM((width, width), BF16),
                        pltpu.VMEM((width // LANES, m_rows, LANES), F32), pltpu.VMEM((m_rows, 2 * n_lanes), F32),
                        pltpu.VMEM((m_rows, n_lanes), F32), pltpu.VMEM((m_rows, n_lanes), F32),
                        pltpu.VMEM((S5_SEQS, 2 * n_lanes), F32),
                        pltpu.VMEM((bsz, n_heads, HG_HEAD_DIM, HG_HEAD_DIM), F32)],
        compiler_params=_cparams("arbitrary"),
        name="mixer",
    )(x, row(in_norm_g[0]), w_in[0], wb, wc, a_re8, a_im8, row(s5_d[0]), s5_w_glu[0], row(s5_b_glu[0]),
      lb, row(hg_norm_g[0]), w_up[0])

    bf = FFN_BLOCK
    assert d_ff % bf == 0
    n_blk = d_ff // bf
    out = pl.pallas_call(
        functools.partial(_ffn_kernel, n_blk=n_blk),
        grid=(bsz, seq_len // tm),
        in_specs=[tok(d_model), tok(width), tok(width), resident((2 * width, d_model)), const2((1, d_model)),
                  resident((d_model, 2 * d_ff)), const2((CONV_WIDTH, 2 * d_ff)),
                  const2((1, 2 * d_ff)), resident((d_ff, d_model)), const2((1, d_model))],
        out_specs=tok(d_model),
        out_shape=jax.ShapeDtypeStruct((bsz, seq_len, d_model), F32),
        scratch_shapes=[pltpu.VMEM((2 * width, d_model), BF16), pltpu.VMEM((d_ff, d_model), BF16),
                        pltpu.VMEM((FFN_ROWS, 2 * d_ff), F32), pltpu.VMEM((tm, d_model), F32),
                        pltpu.VMEM((tm, d_model), BF16), pltpu.VMEM((2, 2, tm + FFN_ROWS, bf), F32),
                        pltpu.VMEM((tm, d_ff), BF16)],
        compiler_params=_cparams("arbitrary", "arbitrary"),
        name="ffn",
    )(x, o_hg, y_s5, w_out[0], row(ffn_norm_g[0]), w_up_bf16, conv_w[0], row(conv_b[0]),
      w_down[0], row(final_norm_g))
    return out
```

```python
import functools

import jax
import jax.numpy as jnp
from jax import lax
from jax.experimental import pallas as pl
from jax.experimental.pallas import tpu as pltpu

EPS = 1e-6
HG_CHUNK = 64
HG_HEAD_DIM = 128
S5_GROUP = 16
S5_STATE = 64
S5_SEG = 64
S5_SEQS = 8
S5_SCAN_BLOCKS = 2
S5_SCAN_UNROLL = True
CONV_WIDTH = 3
FFN_BLOCK = 256
FFN_ROWS = 16
SUBLANES = 8
LANES = 128
VMEM_LIMIT = 60 * 1024 * 1024

F32 = jnp.float32
BF16 = jnp.bfloat16

_NT = (((1,), (1,)), ((), ()))
_TN = (((0,), (0,)), ((), ()))


def _rms(x, g):
    ms = jnp.mean(x * x, axis=-1, keepdims=True)
    return x * lax.rsqrt(ms + EPS) * g


def _cparams(*sem):
    return pltpu.CompilerParams(dimension_semantics=sem, vmem_limit_bytes=VMEM_LIMIT)


def _param_kernel(lb_ref, are_ref, aim_ref, ldt_ref, bre_ref, bim_ref,
                  lb_out, abr_out, abi_out, bbr_out, bbi_out):
    z = lb_ref[...]
    e = jnp.exp(z - jnp.max(z, axis=0, keepdims=True))
    sm = e / jnp.sum(e, axis=0, keepdims=True)
    lb_out[...] = sm[0:1, :]
    dt = jnp.exp(ldt_ref[...])
    ar = are_ref[...]
    ai = aim_ref[...]
    mag = jnp.exp(dt * ar)
    abr = mag * jnp.cos(dt * ai)
    abi = mag * jnp.sin(dt * ai)
    nr = abr - 1.0
    den = ar * ar + ai * ai
    zr = (nr * ar + abi * ai) / den
    zi = (abi * ar - nr * ai) / den
    br = bre_ref[...]
    bi = bim_ref[...]
    abr_out[...] = abr
    abi_out[...] = abi
    bbr_out[...] = zr * br - zi * bi
    bbi_out[...] = zr * bi + zi * br


def _cumsum_rows(x):
    n, w = x.shape
    top = lax.broadcasted_iota(jnp.int32, (SUBLANES, w), 0)
    sh = 1
    while sh < n:
        if sh < SUBLANES:
            r = pltpu.roll(x, sh, axis=0)
            r = jnp.concatenate([jnp.where(top >= sh, r[0:SUBLANES], 0.0), r[SUBLANES:]], axis=0)
        else:
            r = jnp.concatenate([jnp.zeros((sh, w), x.dtype), x[:n - sh]], axis=0)
        x = x + r
        sh *= 2
    return x


def _hgrn2_rows(q, fz, v, gz, lb, ng, st_ref, o_ref):
    c = HG_CHUNK
    n_chunks = q.shape[0] // c
    n_heads = st_ref.shape[0]
    tri = lax.broadcasted_iota(jnp.int32, (c, c), 0) >= lax.broadcasted_iota(jnp.int32, (c, c), 1)

    f = lb + (1.0 - lb) * jax.nn.sigmoid(fz)
    k = 1.0 - f
    logf = jnp.log(f)
    b = jnp.concatenate([_cumsum_rows(logf[ci * c:(ci + 1) * c]) for ci in range(n_chunks)], axis=0)
    kd_f = k * jnp.exp(-b)
    qd = (q * jnp.exp(b)).astype(BF16)
    kd = kd_f.astype(BF16)
    vb = v.astype(BF16)
    gate = ng * (gz * jax.nn.sigmoid(gz))
    decs = [jnp.exp(b[(ci + 1) * c - 1:(ci + 1) * c, :]) for ci in range(n_chunks)]
    kts = [(kd_f[ci * c:(ci + 1) * c] * decs[ci]).astype(BF16) for ci in range(n_chunks)]

    for hd in range(n_heads):
        cs = slice(hd * HG_HEAD_DIM, (hd + 1) * HG_HEAD_DIM)
        st = st_ref[hd]
        for ci in range(n_chunks):
            rows = slice(ci * c, (ci + 1) * c)
            att = lax.dot_general(qd[rows, cs], kd[rows, cs], _NT, preferred_element_type=F32)
            att = jnp.where(tri, att, 0.0)
            lhs = jnp.concatenate([qd[rows, cs], att.astype(BF16)], axis=1)
            rhs = jnp.concatenate([st.astype(BF16), vb[rows, cs]], axis=0)
            o = jnp.dot(lhs, rhs, preferred_element_type=F32)
            ds = lax.dot_general(kts[ci][:, cs], vb[rows, cs], _TN, preferred_element_type=F32)
            dcol = jnp.transpose(jnp.broadcast_to(decs[ci][:, cs], (HG_HEAD_DIM, HG_HEAD_DIM)))
            st = st * dcol + ds
            o = o * lax.rsqrt(jnp.mean(o * o, axis=-1, keepdims=True) + EPS)
            o_ref[rows, cs] = o * gate[rows, cs]
        st_ref[hd] = st


def _mixer_kernel(x_ref, gin_ref, win32_ref, wb_ref, wc_ref, are_ref, aim_ref, d_ref, wglu32_ref, bglu_ref,
                  lb_ref, ng_ref, wup32_ref, ohg_ref, y_ref, wup_ref,
                  win_ref, wglu_ref, il_ref, bu_ref, pre_ref, pim_ref, carry_ref, hst_ref, *, n_blocks):
    wup_ref[...] = wup32_ref[...].astype(BF16)
    h = S5_SEG
    s = S5_SEQS
    half = wb_ref.shape[2] // 2
    kw = wb_ref.shape[1]
    bsz, tt, d_model = x_ref.shape
    width = y_ref.shape[-1]

    @pl.when(pl.program_id(0) == 0)
    def _():
        win_ref[...] = win32_ref[...].astype(BF16)
        wglu_ref[...] = wglu32_ref[...].astype(BF16)
        carry_ref[...] = jnp.zeros_like(carry_ref)
        hst_ref[...] = jnp.zeros_like(hst_ref)
        ar = are_ref[...]
        ai = aim_ref[...]
        pre_ref[0:s, :] = ar
        pim_ref[0:s, :] = ai

        def powers(i, c):
            pr, pi = c
            npr = pr * ar - pi * ai
            npi = pr * ai + pi * ar
            r = pl.ds(pl.multiple_of(i * s, s), s)
            pre_ref[r, :] = npr
            pim_ref[r, :] = npi
            return npr, npi

        lax.fori_loop(1, h, powers, (ar, ai))

    xb = _rms(x_ref[...].reshape(bsz * tt, d_model), gin_ref[...]).astype(BF16)
    u_nat = jnp.dot(xb, win_ref[:, 4 * width:5 * width], preferred_element_type=F32)

    n_slabs = il_ref.shape[0]
    for bi in range(bsz):
        for k in range(2):
            r0 = bi * tt + k * h
            for j in range(n_slabs):
                il_ref[j, pl.ds(2 * bi + k, h, stride=s), :] = u_nat[r0:r0 + h, j * LANES:(j + 1) * LANES]
    u = jnp.concatenate([il_ref[j] for j in range(n_slabs)], axis=1)
    ub = u.astype(BF16)
    for m in range(n_blocks):
        bu_ref[:, 2 * half * m:2 * half * (m + 1)] = jnp.dot(
            ub[:, kw * m:kw * (m + 1)], wb_ref[m], preferred_element_type=F32)

    q, fz, v, gz = (jnp.dot(xb, win_ref[:, i * width:(i + 1) * width], preferred_element_type=F32)
                    for i in range(4))
    for bi in range(bsz):
        r = slice(bi * tt, (bi + 1) * tt)
        _hgrn2_rows(q[r], fz[r], v[r], gz[r], lb_ref[...], ng_ref[...], hst_ref.at[bi], ohg_ref.at[bi])

    second = lax.broadcasted_iota(jnp.int32, (s, half), 0) % 2 == 1
    last = slice((h - 1) * s, h * s)

    for m0 in range(0, n_blocks, S5_SCAN_BLOCKS):
        blocks = []
        for m in range(m0, m0 + S5_SCAN_BLOCKS):
            lo = 2 * half * m
            blocks.append((slice(lo, lo + half), slice(lo + half, lo + 2 * half),
                           slice(half * m, half * (m + 1))))
        coef = [(are_ref[:, am], aim_ref[:, am]) for _, _, am in blocks]

        def scan(i, c, blocks=blocks, coef=coef):
            r = pl.ds(pl.multiple_of(i * s, s), s)
            out = []
            for (re, im, _), (ar, ai), (xr, xi) in zip(blocks, coef, c):
                nxr = ar * xr - ai * xi + bu_ref[r, re]
                nxi = ar * xi + ai * xr + bu_ref[r, im]
                bu_ref[r, re] = nxr
                bu_ref[r, im] = nxi
                out.append((nxr, nxi))
            return tuple(out)

        init = tuple((carry_ref[:, re], carry_ref[:, im]) for re, im, _ in blocks)
        ends = lax.fori_loop(0, h, scan, init, unroll=S5_SCAN_UNROLL)
        starts = [(jnp.where(second, pltpu.roll(er, 1, axis=0), 0.0),
                   jnp.where(second, pltpu.roll(ei, 1, axis=0), 0.0)) for er, ei in ends]

        def fix(i, c, blocks=blocks, starts=starts):
            r = pl.ds(pl.multiple_of(i * s, s), s)
            for (re, im, am), (cr, ci) in zip(blocks, starts):
                pr = pre_ref[r, am]
                pi = pim_ref[r, am]
                bu_ref[r, re] = bu_ref[r, re] + (pr * cr - pi * ci)
                bu_ref[r, im] = bu_ref[r, im] + (pr * ci + pi * cr)
            return c

        lax.fori_loop(0, h, fix, 0, unroll=S5_SCAN_UNROLL)
        for re, im, _ in blocks:
            carry_ref[:, re] = jnp.where(second, 0.0, pltpu.roll(bu_ref[last, re], s - 1, axis=0))
            carry_ref[:, im] = jnp.where(second, 0.0, pltpu.roll(bu_ref[last, im], s - 1, axis=0))

    ys = [jnp.dot(bu_ref[:, 2 * half * m:2 * half * (m + 1)].astype(BF16), wc_ref[m],
                  preferred_element_type=F32) for m in range(n_blocks)]
    y = jnp.concatenate(ys, axis=1) + d_ref[...] * u
    y = jax.nn.gelu(y)
    z = jnp.dot(y.astype(BF16), wglu_ref[...], preferred_element_type=F32) + bglu_ref[...]
    y = y * jax.nn.sigmoid(z)
    for j in range(n_slabs):
        il_ref[j] = y[:, j * LANES:(j + 1) * LANES]
    for bi in range(y_ref.shape[0]):
        for k in range(2):
            for j in range(n_slabs):
                y_ref[bi, k * h:(k + 1) * h, j * LANES:(j + 1) * LANES] = il_ref[
                    j, pl.ds(2 * bi + k, h, stride=s), :]


def _ffn_kernel(x_ref, ohg_ref, y_ref, wout32_ref, gf_ref, wup_ref, cw_ref, cb_ref, wdn32_ref, g_ref, o_ref,
                wout_ref, wdn_ref, tail_ref, h_ref, hn_ref, pre_ref, act_ref, *, n_blk):
    @pl.when((pl.program_id(0) == 0) & (pl.program_id(1) == 0))
    def _():
        wout_ref[...] = wout32_ref[...].astype(BF16)
        wdn_ref[...] = wdn32_ref[...].astype(BF16)

    @pl.when(pl.program_id(1) == 0)
    def _():
        tail_ref[...] = jnp.zeros_like(tail_ref)

    mixed = jnp.concatenate([ohg_ref[...].astype(BF16), y_ref[...].astype(BF16)], axis=1)
    hres = x_ref[...] + jnp.dot(mixed, wout_ref[...], preferred_element_type=F32)
    h_ref[...] = hres
    hn_ref[...] = _rms(hres, gf_ref[...]).astype(BF16)

    tm = act_ref.shape[0]
    bf = pre_ref.shape[-1]
    rows = FFN_ROWS
    top = lax.broadcasted_iota(jnp.int32, (SUBLANES, bf), 0)

    cols = lambda j, part: slice((part * n_blk + j) * bf, (part * n_blk + j + 1) * bf)

    def up(j, slot):
        for part in range(2):
            pre_ref[slot, part, rows:rows + tm] = jnp.dot(hn_ref[...], wup_ref[:, cols(j, part)],
                                                          preferred_element_type=F32)

    def mid(j, slot):
        taps, prev = [], []
        for part in range(2):
            blk = cols(j, part)
            pre_ref[slot, part, 0:rows] = tail_ref[:, blk]
            tail_ref[:, blk] = pre_ref[slot, part, tm:tm + rows]
            cw = cw_ref[:, blk]
            taps.append((cb_ref[:, blk], cw[0:1, :], cw[1:2, :], cw[2:3, :]))
            p = pre_ref[slot, part, 0:rows]
            prev.append((pltpu.roll(p, 1, axis=0), pltpu.roll(p, 2, axis=0)))
        for g in range(tm // rows):
            r0 = rows * (g + 1)
            outs = []
            for part in range(2):
                cur = pre_ref[slot, part, r0:r0 + rows]
                rc = (pltpu.roll(cur, 1, axis=0), pltpu.roll(cur, 2, axis=0))
                sh = [jnp.concatenate([jnp.where(top < d + 1, prev[part][d][0:SUBLANES], rc[d][0:SUBLANES]),
                                       rc[d][SUBLANES:]], axis=0) for d in range(2)]
                prev[part] = rc
                cb, w0, w1, w2 = taps[part]
                outs.append(cb + w0 * sh[1] + w1 * sh[0] + w2 * cur)
            gate, val = outs
            act_ref[rows * g:rows * (g + 1), j * bf:(j + 1) * bf] = (
                (gate * jax.nn.sigmoid(gate)) * val).astype(BF16)

    up(0, 0)
    for j in range(n_blk):
        if j + 1 < n_blk:
            up(j + 1, (j + 1) % 2)
        mid(j, j % 2)
    out = h_ref[...] + jnp.dot(act_ref[...], wdn_ref[...], preferred_element_type=F32)
    o_ref[...] = _rms(out, g_ref[...])


def _tile_sizes(seq_len):
    tm = min(512, seq_len)
    tt = 2 * S5_SEG
    assert seq_len % tm == 0 and seq_len % tt == 0 and tt % HG_CHUNK == 0 and tm % FFN_ROWS == 0
    return tm, tt


def kernel(x, in_norm_g, w_in, hg_lb, hg_norm_g, s5_a_re, s5_a_im, s5_log_dt, s5_b_re, s5_b_im,
           s5_c_re, s5_c_im, s5_d, s5_w_glu, s5_b_glu, w_out, ffn_norm_g, w_up, conv_w, conv_b,
           w_down, final_norm_g):
    bsz, seq_len, d_model = x.shape
    depth = w_in.shape[0]
    assert depth == 1, "single-layer block"
    width = hg_norm_g.shape[1]
    assert w_in.shape[2] == 5 * width and s5_d.shape[1] == width
    n_heads = width // HG_HEAD_DIM
    groups, n_state, n_chan = s5_b_re.shape[1:]
    assert (n_state, n_chan) == (S5_STATE, S5_GROUP) and groups * n_chan == width
    assert 2 * bsz == S5_SEQS
    d_ff = w_down.shape[1]
    tm, tt = _tile_sizes(seq_len)
    n_tiles = seq_len // tt
    row = lambda a: a.reshape(1, -1)

    gp = groups * n_chan
    rep = lambda a: jnp.repeat(a, n_chan, axis=0)
    ldt = jnp.broadcast_to(jnp.repeat(s5_log_dt[0], n_chan)[:, None], (gp, n_state))
    b_t = lambda a: a.transpose(0, 2, 1).reshape(gp, n_state)
    lb, abr, abi, bbr, bbi = pl.pallas_call(
        _param_kernel,
        out_shape=[jax.ShapeDtypeStruct((1, width), F32)] + [jax.ShapeDtypeStruct((gp, n_state), F32)] * 4,
        name="params",
    )(hg_lb, rep(s5_a_re[0]), rep(s5_a_im[0]), ldt, b_t(s5_b_re[0]), b_t(s5_b_im[0]))

    gpb = LANES // n_chan
    n_blocks = groups // gpb
    same = jnp.eye(gpb, dtype=bool)[None, :, None, :, None]

    def blockdiag_in(w):
        t = w.reshape(n_blocks, gpb, n_chan, 1, n_state)
        t = jnp.where(same, jnp.broadcast_to(t, (n_blocks, gpb, n_chan, gpb, n_state)), 0.0)
        return t.reshape(n_blocks, gpb * n_chan, gpb * n_state)

    def blockdiag_out(w):
        t = w.reshape(n_blocks, gpb, n_chan, n_state).transpose(0, 3, 1, 2)[:, None]
        t = jnp.where(same, jnp.broadcast_to(t, (n_blocks, gpb, n_state, gpb, n_chan)), 0.0)
        return t.reshape(n_blocks, gpb * n_state, gpb * n_chan)
    wb = jnp.concatenate([blockdiag_in(bbr), blockdiag_in(bbi)], axis=2).astype(BF16)
    wc = jnp.concatenate([blockdiag_out(s5_c_re[0]), -blockdiag_out(s5_c_im[0])], axis=1).astype(BF16)
    n_lanes = groups * n_state
    a_rows = lambda a: jnp.broadcast_to(a.reshape(groups, n_chan, n_state)[:, 0, :].reshape(1, n_lanes),
                                        (S5_SEQS, n_lanes))
    a_re8, a_im8 = a_rows(abr), a_rows(abi)

    tok = lambda w: pl.BlockSpec((None, tm, w), lambda b, t: (b, t, 0))
    const2 = lambda shape: pl.BlockSpec(shape, lambda b, t: (0,) * len(shape))
    resident = lambda shape: pl.BlockSpec(shape, lambda b, t: (0,) * len(shape), pipeline_mode=pl.Buffered(1))

    m_rows = S5_SEG * S5_SEQS
    c1 = lambda shape: pl.BlockSpec(shape, lambda j: (0,) * len(shape))
    res1 = lambda shape: pl.BlockSpec(shape, lambda j: (0,) * len(shape), pipeline_mode=pl.Buffered(1))
    tile_spec = lambda w: pl.BlockSpec((bsz, tt, w), lambda j: (0, j, 0))
    act = lambda dt: jax.ShapeDtypeStruct((bsz, seq_len, width), dt)
    assert d_model % n_tiles == 0
    wup_slab = pl.BlockSpec((d_model // n_tiles, 2 * d_ff), lambda j: (j, 0))
    o_hg, y_s5, w_up_bf16 = pl.pallas_call(
        functools.partial(_mixer_kernel, n_blocks=n_blocks),
        grid=(n_tiles,),
        in_specs=[tile_spec(d_model), c1((1, d_model)), res1((d_model, 5 * width)),
                  c1(wb.shape), c1(wc.shape), c1(a_re8.shape), c1(a_im8.shape),
                  c1((1, width)), res1((width, width)), c1((1, width)), c1((1, width)), c1((1, width)),
                  wup_slab],
        out_specs=[tile_spec(width)] * 2 + [wup_slab],
        out_shape=[act(F32)] * 2 + [jax.ShapeDtypeStruct((d_model, 2 * d_ff), BF16)],
        scratch_shapes=[pltpu.VMEM((d_model, 5 * width), BF16), pltpu.VMEM((width, width), BF16),
                        pltpu.VMEM((width // LANES, m_rows, LANES), F32), pltpu.VMEM((m_rows, 2 * n_lanes), F32),
                        pltpu.VMEM((m_rows, n_lanes), F32), pltpu.VMEM((m_rows, n_lanes), F32),
                        pltpu.VMEM((S5_SEQS, 2 * n_lanes), F32),
                        pltpu.VMEM((bsz, n_heads, HG_HEAD_DIM, HG_HEAD_DIM), F32)],
        compiler_params=_cparams("arbitrary"),
        name="mixer",
    )(x, row(in_norm_g[0]), w_in[0], wb, wc, a_re8, a_im8, row(s5_d[0]), s5_w_glu[0], row(s5_b_glu[0]),
      lb, row(hg_norm_g[0]), w_up[0])

    bf = FFN_BLOCK
    assert d_ff % bf == 0
    n_blk = d_ff // bf
    out = pl.pallas_call(
        functools.partial(_ffn_kernel, n_blk=n_blk),
        grid=(bsz, seq_len // tm),
        in_specs=[tok(d_model), tok(width), tok(width), resident((2 * width, d_model)), const2((1, d_model)),
                  resident((d_model, 2 * d_ff)), const2((CONV_WIDTH, 2 * d_ff)),
                  const2((1, 2 * d_ff)), resident((d_ff, d_model)), const2((1, d_model))],
        out_specs=tok(d_model),
        out_shape=jax.ShapeDtypeStruct((bsz, seq_len, d_model), F32),
        scratch_shapes=[pltpu.VMEM((2 * width, d_model), BF16), pltpu.VMEM((d_ff, d_model), BF16),
                        pltpu.VMEM((FFN_ROWS, 2 * d_ff), F32), pltpu.VMEM((tm, d_model), F32),
                        pltpu.VMEM((tm, d_model), BF16), pltpu.VMEM((2, 2, tm + FFN_ROWS, bf), F32),
                        pltpu.VMEM((tm, d_ff), BF16)],
        compiler_params=_cparams("arbitrary", "arbitrary"),
        name="ffn",
    )(x, o_hg, y_s5, w_out[0], row(ffn_norm_g[0]), w_up_bf16, conv_w[0], row(conv_b[0]),
      w_down[0], row(final_norm_g))
    return out
```

```python
import functools

import jax
import jax.numpy as jnp
from jax import lax
from jax.experimental import pallas as pl
from jax.experimental.pallas import tpu as pltpu

EPS = 1e-6
HG_CHUNK = 64
HG_HEAD_DIM = 128
S5_GROUP = 16
S5_STATE = 64
S5_SEG = 64
S5_SEQS = 8
S5_SCAN_BLOCKS = 2
S5_SCAN_UNROLL = True
CONV_WIDTH = 3
FFN_BLOCK = 256
FFN_ROWS = 16
SUBLANES = 8
LANES = 128
VMEM_LIMIT = 60 * 1024 * 1024

F32 = jnp.float32
BF16 = jnp.bfloat16

_NT = (((1,), (1,)), ((), ()))
_TN = (((0,), (0,)), ((), ()))


def _rms(x, g):
    ms = jnp.mean(x * x, axis=-1, keepdims=True)
    return x * lax.rsqrt(ms + EPS) * g


def _cparams(*sem):
    return pltpu.CompilerParams(dimension_semantics=sem, vmem_limit_bytes=VMEM_LIMIT)


def _param_kernel(lb_ref, are_ref, aim_ref, ldt_ref, bre_ref, bim_ref,
                  lb_out, abr_out, abi_out, bbr_out, bbi_out):
    z = lb_ref[...]
    e = jnp.exp(z - jnp.max(z, axis=0, keepdims=True))
    sm = e / jnp.sum(e, axis=0, keepdims=True)
    lb_out[...] = sm[0:1, :]
    dt = jnp.exp(ldt_ref[...])
    ar = are_ref[...]
    ai = aim_ref[...]
    mag = jnp.exp(dt * ar)
    abr = mag * jnp.cos(dt * ai)
    abi = mag * jnp.sin(dt * ai)
    nr = abr - 1.0
    den = ar * ar + ai * ai
    zr = (nr * ar + abi * ai) / den
    zi = (abi * ar - nr * ai) / den
    br = bre_ref[...]
    bi = bim_ref[...]
    abr_out[...] = abr
    abi_out[...] = abi
    bbr_out[...] = zr * br - zi * bi
    bbi_out[...] = zr * bi + zi * br


def _cumsum_rows(x):
    n, w = x.shape
    top = lax.broadcasted_iota(jnp.int32, (SUBLANES, w), 0)
    sh = 1
    while sh < n:
        if sh < SUBLANES:
            r = pltpu.roll(x, sh, axis=0)
            r = jnp.concatenate([jnp.where(top >= sh, r[0:SUBLANES], 0.0), r[SUBLANES:]], axis=0)
        else:
            r = jnp.concatenate([jnp.zeros((sh, w), x.dtype), x[:n - sh]], axis=0)
        x = x + r
        sh *= 2
    return x


def _hgrn2_rows(q, fz, v, gz, lb, ng, st_ref, o_ref):
    c = HG_CHUNK
    n_chunks = q.shape[0] // c
    n_heads = st_ref.shape[0]
    tri = lax.broadcasted_iota(jnp.int32, (c, c), 0) >= lax.broadcasted_iota(jnp.int32, (c, c), 1)

    f = lb + (1.0 - lb) * jax.nn.sigmoid(fz)
    k = 1.0 - f
    logf = jnp.log(f)
    b = jnp.concatenate([_cumsum_rows(logf[ci * c:(ci + 1) * c]) for ci in range(n_chunks)], axis=0)
    kd_f = k * jnp.exp(-b)
    qd = (q * jnp.exp(b)).astype(BF16)
    kd = kd_f.astype(BF16)
    vb = v.astype(BF16)
    gate = ng * (gz * jax.nn.sigmoid(gz))
    decs = [jnp.exp(b[(ci + 1) * c - 1:(ci + 1) * c, :]) for ci in range(n_chunks)]
    kts = [(kd_f[ci * c:(ci + 1) * c] * decs[ci]).astype(BF16) for ci in range(n_chunks)]

    for hd in range(n_heads):
        cs = slice(hd * HG_HEAD_DIM, (hd + 1) * HG_HEAD_DIM)
        st = st_ref[hd]
        for ci in range(n_chunks):
            rows = slice(ci * c, (ci + 1) * c)
            att = lax.dot_general(qd[rows, cs], kd[rows, cs], _NT, preferred_element_type=F32)
            att = jnp.where(tri, att, 0.0)
            lhs = jnp.concatenate([qd[rows, cs], att.astype(BF16)], axis=1)
            rhs = jnp.concatenate([st.astype(BF16), vb[rows, cs]], axis=0)
            o = jnp.dot(lhs, rhs, preferred_element_type=F32)
            ds = lax.dot_general(kts[ci][:, cs], vb[rows, cs], _TN, preferred_element_type=F32)
            dcol = jnp.transpose(jnp.broadcast_to(decs[ci][:, cs], (HG_HEAD_DIM, HG_HEAD_DIM)))
            st = st * dcol + ds
            o = o * lax.rsqrt(jnp.mean(o * o, axis=-1, keepdims=True) + EPS)
            o_ref[rows, cs] = o * gate[rows, cs]
        st_ref[hd] = st


def _mixer_kernel(x_ref, gin_ref, win32_ref, bbr_ref, bbi_ref, cre_ref, cim_ref, are_ref, aim_ref, d_ref,
                  wglu32_ref, bglu_ref, lb_ref, ng_ref, wup32_ref, ohg_ref, y_ref, wup_ref,
                  win_ref, wglu_ref, wb_ref, wc_ref, il_ref, bu_ref, pre_ref, pim_ref, carry_ref, hst_ref, *,
                  n_blocks):
    wup_ref[...] = wup32_ref[...].astype(BF16)
    h = S5_SEG
    s = S5_SEQS
    half = wb_ref.shape[2] // 2
    kw = wb_ref.shape[1]
    bsz, tt, d_model = x_ref.shape
    width = y_ref.shape[-1]

    @pl.when(pl.program_id(0) == 0)
    def _():
        win_ref[...] = win32_ref[...].astype(BF16)
        wglu_ref[...] = wglu32_ref[...].astype(BF16)
        carry_ref[...] = jnp.zeros_like(carry_ref)
        hst_ref[...] = jnp.zeros_like(hst_ref)
        same = (lax.broadcasted_iota(jnp.int32, (kw, half), 0) // S5_GROUP
                == lax.broadcasted_iota(jnp.int32, (kw, half), 1) // S5_STATE)

        def blockdiag(ref, m):
            t = ref[kw * m:kw * (m + 1), :]
            return jnp.where(same, jnp.concatenate([t] * (half // S5_STATE), axis=1), 0.0)

        for m in range(n_blocks):
            wb_ref[m, :, 0:half] = blockdiag(bbr_ref, m).astype(BF16)
            wb_ref[m, :, half:2 * half] = blockdiag(bbi_ref, m).astype(BF16)
            wc_ref[m, 0:half, :] = jnp.transpose(blockdiag(cre_ref, m)).astype(BF16)
            wc_ref[m, half:2 * half, :] = (-jnp.transpose(blockdiag(cim_ref, m))).astype(BF16)
        ar = are_ref[...]
        ai = aim_ref[...]
        pre_ref[0:s, :] = ar
        pim_ref[0:s, :] = ai

        def powers(i, c):
            pr, pi = c
            npr = pr * ar - pi * ai
            npi = pr * ai + pi * ar
            r = pl.ds(pl.multiple_of(i * s, s), s)
            pre_ref[r, :] = npr
            pim_ref[r, :] = npi
            return npr, npi

        lax.fori_loop(1, h, powers, (ar, ai))

    xb = _rms(x_ref[...].reshape(bsz * tt, d_model), gin_ref[...]).astype(BF16)
    u_nat = jnp.dot(xb, win_ref[:, 4 * width:5 * width], preferred_element_type=F32)

    n_slabs = il_ref.shape[0]
    for bi in range(bsz):
        for k in range(2):
            r0 = bi * tt + k * h
            for j in range(n_slabs):
                il_ref[j, pl.ds(2 * bi + k, h, stride=s), :] = u_nat[r0:r0 + h, j * LANES:(j + 1) * LANES]
    u = jnp.concatenate([il_ref[j] for j in range(n_slabs)], axis=1)
    ub = u.astype(BF16)
    for m in range(n_blocks):
        bu_ref[:, 2 * half * m:2 * half * (m + 1)] = jnp.dot(
            ub[:, kw * m:kw * (m + 1)], wb_ref[m], preferred_element_type=F32)

    q, fz, v, gz = (jnp.dot(xb, win_ref[:, i * width:(i + 1) * width], preferred_element_type=F32)
                    for i in range(4))
    for bi in range(bsz):
        r = slice(bi * tt, (bi + 1) * tt)
        _hgrn2_rows(q[r], fz[r], v[r], gz[r], lb_ref[...], ng_ref[...], hst_ref.at[bi], ohg_ref.at[bi])

    second = lax.broadcasted_iota(jnp.int32, (s, half), 0) % 2 == 1
    last = slice((h - 1) * s, h * s)

    for m0 in range(0, n_blocks, S5_SCAN_BLOCKS):
        blocks = []
        for m in range(m0, m0 + S5_SCAN_BLOCKS):
            lo = 2 * half * m
            blocks.append((slice(lo, lo + half), slice(lo + half, lo + 2 * half),
                           slice(half * m, half * (m + 1))))
        coef = [(are_ref[:, am], aim_ref[:, am]) for _, _, am in blocks]

        def scan(i, c, blocks=blocks, coef=coef):
            r = pl.ds(pl.multiple_of(i * s, s), s)
            out = []
            for (re, im, _), (ar, ai), (xr, xi) in zip(blocks, coef, c):
                nxr = ar * xr - ai * xi + bu_ref[r, re]
                nxi = ar * xi + ai * xr + bu_ref[r, im]
                bu_ref[r, re] = nxr
                bu_ref[r, im] = nxi
                out.append((nxr, nxi))
            return tuple(out)

        init = tuple((carry_ref[:, re], carry_ref[:, im]) for re, im, _ in blocks)
        ends = lax.fori_loop(0, h, scan, init, unroll=S5_SCAN_UNROLL)
        starts = [(jnp.where(second, pltpu.roll(er, 1, axis=0), 0.0),
                   jnp.where(second, pltpu.roll(ei, 1, axis=0), 0.0)) for er, ei in ends]

        def fix(i, c, blocks=blocks, starts=starts):
            r = pl.ds(pl.multiple_of(i * s, s), s)
            for (re, im, am), (cr, ci) in zip(blocks, starts):
                pr = pre_ref[r, am]
                pi = pim_ref[r, am]
                bu_ref[r, re] = bu_ref[r, re] + (pr * cr - pi * ci)
                bu_ref[r, im] = bu_ref[r, im] + (pr * ci + pi * cr)
            return c

        lax.fori_loop(0, h, fix, 0, unroll=S5_SCAN_UNROLL)
        for re, im, _ in blocks:
            carry_ref[:, re] = jnp.where(second, 0.0, pltpu.roll(bu_ref[last, re], s - 1, axis=0))
            carry_ref[:, im] = jnp.where(second, 0.0, pltpu.roll(bu_ref[last, im], s - 1, axis=0))

    ys = [jnp.dot(bu_ref[:, 2 * half * m:2 * half * (m + 1)].astype(BF16), wc_ref[m],
                  preferred_element_type=F32) for m in range(n_blocks)]
    y = jnp.concatenate(ys, axis=1) + d_ref[...] * u
    y = jax.nn.gelu(y)
    z = jnp.dot(y.astype(BF16), wglu_ref[...], preferred_element_type=F32) + bglu_ref[...]
    y = y * jax.nn.sigmoid(z)
    for j in range(n_slabs):
        il_ref[j] = y[:, j * LANES:(j + 1) * LANES]
    for bi in range(y_ref.shape[0]):
        for k in range(2):
            for j in range(n_slabs):
                y_ref[bi, k * h:(k + 1) * h, j * LANES:(j + 1) * LANES] = il_ref[
                    j, pl.ds(2 * bi + k, h, stride=s), :]


def _ffn_kernel(x_ref, ohg_ref, y_ref, wout32_ref, gf_ref, wup_ref, cw_ref, cb_ref, wdn32_ref, g_ref, o_ref,
                wout_ref, wdn_ref, tail_ref, h_ref, hn_ref, pre_ref, act_ref, *, n_blk):
    @pl.when((pl.program_id(0) == 0) & (pl.program_id(1) == 0))
    def _():
        wout_ref[...] = wout32_ref[...].astype(BF16)
        wdn_ref[...] = wdn32_ref[...].astype(BF16)

    @pl.when(pl.program_id(1) == 0)
    def _():
        tail_ref[...] = jnp.zeros_like(tail_ref)

    mixed = jnp.concatenate([ohg_ref[...].astype(BF16), y_ref[...].astype(BF16)], axis=1)
    hres = x_ref[...] + jnp.dot(mixed, wout_ref[...], preferred_element_type=F32)
    h_ref[...] = hres
    hn_ref[...] = _rms(hres, gf_ref[...]).astype(BF16)

    tm = act_ref.shape[0]
    bf = pre_ref.shape[-1]
    rows = FFN_ROWS
    top = lax.broadcasted_iota(jnp.int32, (SUBLANES, bf), 0)

    cols = lambda j, part: slice((part * n_blk + j) * bf, (part * n_blk + j + 1) * bf)

    def up(j, slot):
        for part in range(2):
            pre_ref[slot, part, rows:rows + tm] = jnp.dot(hn_ref[...], wup_ref[:, cols(j, part)],
                                                          preferred_element_type=F32)

    def mid(j, slot):
        taps, prev = [], []
        for part in range(2):
            blk = cols(j, part)
            pre_ref[slot, part, 0:rows] = tail_ref[:, blk]
            tail_ref[:, blk] = pre_ref[slot, part, tm:tm + rows]
            cw = cw_ref[:, blk]
            taps.append((cb_ref[:, blk], cw[0:1, :], cw[1:2, :], cw[2:3, :]))
            p = pre_ref[slot, part, 0:rows]
            prev.append((pltpu.roll(p, 1, axis=0), pltpu.roll(p, 2, axis=0)))
        for g in range(tm // rows):
            r0 = rows * (g + 1)
            outs = []
            for part in range(2):
                cur = pre_ref[slot, part, r0:r0 + rows]
                rc = (pltpu.roll(cur, 1, axis=0), pltpu.roll(cur, 2, axis=0))
                sh = [jnp.concatenate([jnp.where(top < d + 1, prev[part][d][0:SUBLANES], rc[d][0:SUBLANES]),
                                       rc[d][SUBLANES:]], axis=0) for d in range(2)]
                prev[part] = rc
                cb, w0, w1, w2 = taps[part]
                outs.append(cb + w0 * sh[1] + w1 * sh[0] + w2 * cur)
            gate, val = outs
            act_ref[rows * g:rows * (g + 1), j * bf:(j + 1) * bf] = (
                (gate * jax.nn.sigmoid(gate)) * val).astype(BF16)

    up(0, 0)
    for j in range(n_blk):
        if j + 1 < n_blk:
            up(j + 1, (j + 1) % 2)
        mid(j, j % 2)
    out = h_ref[...] + jnp.dot(act_ref[...], wdn_ref[...], preferred_element_type=F32)
    o_ref[...] = _rms(out, g_ref[...])


def _tile_sizes(seq_len):
    tm = min(512, seq_len)
    tt = 2 * S5_SEG
    assert seq_len % tm == 0 and seq_len % tt == 0 and tt % HG_CHUNK == 0 and tm % FFN_ROWS == 0
    return tm, tt


def kernel(x, in_norm_g, w_in, hg_lb, hg_norm_g, s5_a_re, s5_a_im, s5_log_dt, s5_b_re, s5_b_im,
           s5_c_re, s5_c_im, s5_d, s5_w_glu, s5_b_glu, w_out, ffn_norm_g, w_up, conv_w, conv_b,
           w_down, final_norm_g):
    bsz, seq_len, d_model = x.shape
    depth = w_in.shape[0]
    assert depth == 1, "single-layer block"
    width = hg_norm_g.shape[1]
    assert w_in.shape[2] == 5 * width and s5_d.shape[1] == width
    n_heads = width // HG_HEAD_DIM
    groups, n_state, n_chan = s5_b_re.shape[1:]
    assert (n_state, n_chan) == (S5_STATE, S5_GROUP) and groups * n_chan == width
    assert 2 * bsz == S5_SEQS
    d_ff = w_down.shape[1]
    tm, tt = _tile_sizes(seq_len)
    n_tiles = seq_len // tt
    row = lambda a: a.reshape(1, -1)

    gp = groups * n_chan
    rep = lambda a: jnp.repeat(a, n_chan, axis=0)
    ldt = jnp.broadcast_to(jnp.repeat(s5_log_dt[0], n_chan)[:, None], (gp, n_state))
    b_t = lambda a: a.transpose(0, 2, 1).reshape(gp, n_state)
    lb, abr, abi, bbr, bbi = pl.pallas_call(
        _param_kernel,
        out_shape=[jax.ShapeDtypeStruct((1, width), F32)] + [jax.ShapeDtypeStruct((gp, n_state), F32)] * 4,
        name="params",
    )(hg_lb, rep(s5_a_re[0]), rep(s5_a_im[0]), ldt, b_t(s5_b_re[0]), b_t(s5_b_im[0]))

    gpb = LANES // n_chan
    n_blocks = groups // gpb
    c_re, c_im = s5_c_re[0].reshape(gp, n_state), s5_c_im[0].reshape(gp, n_state)
    n_lanes = groups * n_state
    a_rows = lambda a: jnp.broadcast_to(a.reshape(groups, n_chan, n_state)[:, 0, :].reshape(1, n_lanes),
                                        (S5_SEQS, n_lanes))
    a_re8, a_im8 = a_rows(abr), a_rows(abi)

    tok = lambda w: pl.BlockSpec((None, tm, w), lambda b, t: (b, t, 0))
    const2 = lambda shape: pl.BlockSpec(shape, lambda b, t: (0,) * len(shape))
    resident = lambda shape: pl.BlockSpec(shape, lambda b, t: (0,) * len(shape), pipeline_mode=pl.Buffered(1))

    m_rows = S5_SEG * S5_SEQS
    c1 = lambda shape: pl.BlockSpec(shape, lambda j: (0,) * len(shape))
    res1 = lambda shape: pl.BlockSpec(shape, lambda j: (0,) * len(shape), pipeline_mode=pl.Buffered(1))
    tile_spec = lambda w: pl.BlockSpec((bsz, tt, w), lambda j: (0, j, 0))
    act = lambda dt: jax.ShapeDtypeStruct((bsz, seq_len, width), dt)
    assert d_model % n_tiles == 0
    wup_slab = pl.BlockSpec((d_model // n_tiles, 2 * d_ff), lambda j: (j, 0))
    o_hg, y_s5, w_up_bf16 = pl.pallas_call(
        functools.partial(_mixer_kernel, n_blocks=n_blocks),
        grid=(n_tiles,),
        in_specs=[tile_spec(d_model), c1((1, d_model)), res1((d_model, 5 * width)),
                  c1((gp, n_state)), c1((gp, n_state)), c1((gp, n_state)), c1((gp, n_state)),
                  c1(a_re8.shape), c1(a_im8.shape),
                  c1((1, width)), res1((width, width)), c1((1, width)), c1((1, width)), c1((1, width)),
                  wup_slab],
        out_specs=[tile_spec(width)] * 2 + [wup_slab],
        out_shape=[act(F32)] * 2 + [jax.ShapeDtypeStruct((d_model, 2 * d_ff), BF16)],
        scratch_shapes=[pltpu.VMEM((d_model, 5 * width), BF16), pltpu.VMEM((width, width), BF16),
                        pltpu.VMEM((n_blocks, LANES, 2 * gpb * n_state), BF16),
                        pltpu.VMEM((n_blocks, 2 * gpb * n_state, LANES), BF16),
                        pltpu.VMEM((width // LANES, m_rows, LANES), F32), pltpu.VMEM((m_rows, 2 * n_lanes), F32),
                        pltpu.VMEM((m_rows, n_lanes), F32), pltpu.VMEM((m_rows, n_lanes), F32),
                        pltpu.VMEM((S5_SEQS, 2 * n_lanes), F32),
                        pltpu.VMEM((bsz, n_heads, HG_HEAD_DIM, HG_HEAD_DIM), F32)],
        compiler_params=_cparams("arbitrary"),
        name="mixer",
    )(x, row(in_norm_g[0]), w_in[0], bbr, bbi, c_re, c_im, a_re8, a_im8, row(s5_d[0]), s5_w_glu[0], row(s5_b_glu[0]),
      lb, row(hg_norm_g[0]), w_up[0])

    bf = FFN_BLOCK
    assert d_ff % bf == 0
    n_blk = d_ff // bf
    out = pl.pallas_call(
        functools.partial(_ffn_kernel, n_blk=n_blk),
        grid=(bsz, seq_len // tm),
        in_specs=[tok(d_model), tok(width), tok(width), resident((2 * width, d_model)), const2((1, d_model)),
                  resident((d_model, 2 * d_ff)), const2((CONV_WIDTH, 2 * d_ff)),
                  const2((1, 2 * d_ff)), resident((d_ff, d_model)), const2((1, d_model))],
        out_specs=tok(d_model),
        out_shape=jax.ShapeDtypeStruct((bsz, seq_len, d_model), F32),
        scratch_shapes=[pltpu.VMEM((2 * width, d_model), BF16), pltpu.VMEM((d_ff, d_model), BF16),
                        pltpu.VMEM((FFN_ROWS, 2 * d_ff), F32), pltpu.VMEM((tm, d_model), F32),
                        pltpu.VMEM((tm, d_model), BF16), pltpu.VMEM((2, 2, tm + FFN_ROWS, bf), F32),
                        pltpu.VMEM((tm, d_ff), BF16)],
        compiler_params=_cparams("arbitrary", "arbitrary"),
        name="ffn",
    )(x, o_hg, y_s5, w_out[0], row(ffn_norm_g[0]), w_up_bf16, conv_w[0], row(conv_b[0]),
      w_down[0], row(final_norm_g))
    return out
```

```python
import functools

import jax
import jax.numpy as jnp
from jax import lax
from jax.experimental import pallas as pl
from jax.experimental.pallas import tpu as pltpu

EPS = 1e-6
HG_CHUNK = 64
HG_HEAD_DIM = 128
S5_GROUP = 16
S5_STATE = 64
S5_SEG = 64
S5_SEQS = 8
S5_SCAN_BLOCKS = 2
S5_SCAN_UNROLL = True
CONV_WIDTH = 3
FFN_BLOCK = 256
FFN_ROWS = 16
SUBLANES = 8
LANES = 128
VMEM_LIMIT = 60 * 1024 * 1024

F32 = jnp.float32
BF16 = jnp.bfloat16

_NT = (((1,), (1,)), ((), ()))
_TN = (((0,), (0,)), ((), ()))


def _rms(x, g):
    ms = jnp.mean(x * x, axis=-1, keepdims=True)
    return x * lax.rsqrt(ms + EPS) * g


def _cparams(*sem):
    return pltpu.CompilerParams(dimension_semantics=sem, vmem_limit_bytes=VMEM_LIMIT)


def _cumsum_rows(x):
    n, w = x.shape
    top = lax.broadcasted_iota(jnp.int32, (SUBLANES, w), 0)
    sh = 1
    while sh < n:
        if sh < SUBLANES:
            r = pltpu.roll(x, sh, axis=0)
            r = jnp.concatenate([jnp.where(top >= sh, r[0:SUBLANES], 0.0), r[SUBLANES:]], axis=0)
        else:
            r = jnp.concatenate([jnp.zeros((sh, w), x.dtype), x[:n - sh]], axis=0)
        x = x + r
        sh *= 2
    return x


def _hgrn2_rows(q, fz, v, gz, lb, ng, st_ref, o_ref):
    c = HG_CHUNK
    n_chunks = q.shape[0] // c
    n_heads = st_ref.shape[0]
    tri = lax.broadcasted_iota(jnp.int32, (c, c), 0) >= lax.broadcasted_iota(jnp.int32, (c, c), 1)

    f = lb + (1.0 - lb) * jax.nn.sigmoid(fz)
    k = 1.0 - f
    logf = jnp.log(f)
    b = jnp.concatenate([_cumsum_rows(logf[ci * c:(ci + 1) * c]) for ci in range(n_chunks)], axis=0)
    kd_f = k * jnp.exp(-b)
    qd = (q * jnp.exp(b)).astype(BF16)
    kd = kd_f.astype(BF16)
    vb = v.astype(BF16)
    gate = ng * (gz * jax.nn.sigmoid(gz))
    decs = [jnp.exp(b[(ci + 1) * c - 1:(ci + 1) * c, :]) for ci in range(n_chunks)]
    kts = [(kd_f[ci * c:(ci + 1) * c] * decs[ci]).astype(BF16) for ci in range(n_chunks)]

    for hd in range(n_heads):
        cs = slice(hd * HG_HEAD_DIM, (hd + 1) * HG_HEAD_DIM)
        st = st_ref[hd]
        for ci in range(n_chunks):
            rows = slice(ci * c, (ci + 1) * c)
            att = lax.dot_general(qd[rows, cs], kd[rows, cs], _NT, preferred_element_type=F32)
            att = jnp.where(tri, att, 0.0)
            lhs = jnp.concatenate([qd[rows, cs], att.astype(BF16)], axis=1)
            rhs = jnp.concatenate([st.astype(BF16), vb[rows, cs]], axis=0)
            o = jnp.dot(lhs, rhs, preferred_element_type=F32)
            ds = lax.dot_general(kts[ci][:, cs], vb[rows, cs], _TN, preferred_element_type=F32)
            dcol = jnp.transpose(jnp.broadcast_to(decs[ci][:, cs], (HG_HEAD_DIM, HG_HEAD_DIM)))
            st = st * dcol + ds
            o = o * lax.rsqrt(jnp.mean(o * o, axis=-1, keepdims=True) + EPS)
            o_ref[rows, cs] = o * gate[rows, cs]
        st_ref[hd] = st


def _mixer_kernel(x_ref, gin_ref, win32_ref, adt_ref, bre_ref, bim_ref, cre_ref, cim_ref, d_ref,
                  wglu32_ref, bglu_ref, lbz_ref, ng_ref, wup32_ref, ohg_ref, y_ref, wup_ref,
                  win_ref, wglu_ref, wb_ref, wc_ref, are_ref, aim_ref, lb_ref, il_ref, bu_ref, pre_ref, pim_ref,
                  carry_ref, hst_ref, *, n_blocks):
    wup_ref[...] = wup32_ref[...].astype(BF16)
    h = S5_SEG
    s = S5_SEQS
    half = wb_ref.shape[2] // 2
    kw = wb_ref.shape[1]
    bsz, tt, d_model = x_ref.shape
    width = y_ref.shape[-1]

    @pl.when(pl.program_id(0) == 0)
    def _():
        win_ref[...] = win32_ref[...].astype(BF16)
        wglu_ref[...] = wglu32_ref[...].astype(BF16)
        carry_ref[...] = jnp.zeros_like(carry_ref)
        hst_ref[...] = jnp.zeros_like(hst_ref)
        z = lbz_ref[...]
        e = jnp.exp(z - jnp.max(z, axis=0, keepdims=True))
        lb_ref[...] = (e / jnp.sum(e, axis=0, keepdims=True))[0:1, :]

        a_r = adt_ref[0:1, :]
        a_i = adt_ref[1:2, :]
        dt = jnp.exp(adt_ref[2:3, :])
        mag = jnp.exp(dt * a_r)
        abr = mag * jnp.cos(dt * a_i)
        abi = mag * jnp.sin(dt * a_i)
        nr = abr - 1.0
        den = a_r * a_r + a_i * a_i
        zr = (nr * a_r + abi * a_i) / den
        zi = (abi * a_r - nr * a_i) / den
        ar = jnp.broadcast_to(abr, are_ref.shape)
        ai = jnp.broadcast_to(abi, aim_ref.shape)
        are_ref[...] = ar
        aim_ref[...] = ai

        same = (lax.broadcasted_iota(jnp.int32, (kw, half), 0) // S5_GROUP
                == lax.broadcasted_iota(jnp.int32, (kw, half), 1) // S5_STATE)

        def spread(ref, m):
            rows = ref.shape[0] // n_blocks
            t = ref[rows * m:rows * (m + 1), :]
            return jnp.concatenate([t] * (LANES // ref.shape[1]), axis=1)

        for m in range(n_blocks):
            bs = slice(half * m, half * (m + 1))
            b_r = jnp.concatenate([jnp.transpose(bre_ref[bs, :])] * (kw // S5_GROUP), axis=0)
            b_i = jnp.concatenate([jnp.transpose(bim_ref[bs, :])] * (kw // S5_GROUP), axis=0)
            wb_ref[m, :, 0:half] = jnp.where(same, zr[:, bs] * b_r - zi[:, bs] * b_i, 0.0).astype(BF16)
            wb_ref[m, :, half:2 * half] = jnp.where(same, zr[:, bs] * b_i + zi[:, bs] * b_r, 0.0).astype(BF16)
            c_r = jnp.where(same, jnp.concatenate([spread(cre_ref, m)] * (half // LANES), axis=1), 0.0)
            c_i = jnp.where(same, jnp.concatenate([spread(cim_ref, m)] * (half // LANES), axis=1), 0.0)
            wc_ref[m, 0:half, :] = jnp.transpose(c_r).astype(BF16)
            wc_ref[m, half:2 * half, :] = (-jnp.transpose(c_i)).astype(BF16)
        pre_ref[0:s, :] = ar
        pim_ref[0:s, :] = ai

        def powers(i, c):
            pr, pi = c
            npr = pr * ar - pi * ai
            npi = pr * ai + pi * ar
            r = pl.ds(pl.multiple_of(i * s, s), s)
            pre_ref[r, :] = npr
            pim_ref[r, :] = npi
            return npr, npi

        lax.fori_loop(1, h, powers, (ar, ai))

    xb = _rms(x_ref[...].reshape(bsz * tt, d_model), gin_ref[...]).astype(BF16)
    u_nat = jnp.dot(xb, win_ref[:, 4 * width:5 * width], preferred_element_type=F32)

    n_slabs = il_ref.shape[0]
    for bi in range(bsz):
        for k in range(2):
            r0 = bi * tt + k * h
            for j in range(n_slabs):
                il_ref[j, pl.ds(2 * bi + k, h, stride=s), :] = u_nat[r0:r0 + h, j * LANES:(j + 1) * LANES]
    u = jnp.concatenate([il_ref[j] for j in range(n_slabs)], axis=1)
    ub = u.astype(BF16)
    for m in range(n_blocks):
        bu_ref[:, 2 * half * m:2 * half * (m + 1)] = jnp.dot(
            ub[:, kw * m:kw * (m + 1)], wb_ref[m], preferred_element_type=F32)

    q, fz, v, gz = (jnp.dot(xb, win_ref[:, i * width:(i + 1) * width], preferred_element_type=F32)
                    for i in range(4))
    for bi in range(bsz):
        r = slice(bi * tt, (bi + 1) * tt)
        _hgrn2_rows(q[r], fz[r], v[r], gz[r], lb_ref[...], ng_ref[...], hst_ref.at[bi], ohg_ref.at[bi])

    second = lax.broadcasted_iota(jnp.int32, (s, half), 0) % 2 == 1
    last = slice((h - 1) * s, h * s)

    for m0 in range(0, n_blocks, S5_SCAN_BLOCKS):
        blocks = []
        for m in range(m0, m0 + S5_SCAN_BLOCKS):
            lo = 2 * half * m
            blocks.append((slice(lo, lo + half), slice(lo + half, lo + 2 * half),
                           slice(half * m, half * (m + 1))))
        coef = [(are_ref[:, am], aim_ref[:, am]) for _, _, am in blocks]

        def scan(i, c, blocks=blocks, coef=coef):
            r = pl.ds(pl.multiple_of(i * s, s), s)
            out = []
            for (re, im, _), (ar, ai), (xr, xi) in zip(blocks, coef, c):
                nxr = ar * xr - ai * xi + bu_ref[r, re]
                nxi = ar * xi + ai * xr + bu_ref[r, im]
                bu_ref[r, re] = nxr
                bu_ref[r, im] = nxi
                out.append((nxr, nxi))
            return tuple(out)

        init = tuple((carry_ref[:, re], carry_ref[:, im]) for re, im, _ in blocks)
        ends = lax.fori_loop(0, h, scan, init, unroll=S5_SCAN_UNROLL)
        starts = [(jnp.where(second, pltpu.roll(er, 1, axis=0), 0.0),
                   jnp.where(second, pltpu.roll(ei, 1, axis=0), 0.0)) for er, ei in ends]

        def fix(i, c, blocks=blocks, starts=starts):
            r = pl.ds(pl.multiple_of(i * s, s), s)
            for (re, im, am), (cr, ci) in zip(blocks, starts):
                pr = pre_ref[r, am]
                pi = pim_ref[r, am]
                bu_ref[r, re] = bu_ref[r, re] + (pr * cr - pi * ci)
                bu_ref[r, im] = bu_ref[r, im] + (pr * ci + pi * cr)
            return c

        lax.fori_loop(0, h, fix, 0, unroll=S5_SCAN_UNROLL)
        for re, im, _ in blocks:
            carry_ref[:, re] = jnp.where(second, 0.0, pltpu.roll(bu_ref[last, re], s - 1, axis=0))
            carry_ref[:, im] = jnp.where(second, 0.0, pltpu.roll(bu_ref[last, im], s - 1, axis=0))

    ys = [jnp.dot(bu_ref[:, 2 * half * m:2 * half * (m + 1)].astype(BF16), wc_ref[m],
                  preferred_element_type=F32) for m in range(n_blocks)]
    y = jnp.concatenate(ys, axis=1) + d_ref[...] * u
    y = jax.nn.gelu(y)
    z = jnp.dot(y.astype(BF16), wglu_ref[...], preferred_element_type=F32) + bglu_ref[...]
    y = y * jax.nn.sigmoid(z)
    for j in range(n_slabs):
        il_ref[j] = y[:, j * LANES:(j + 1) * LANES]
    for bi in range(y_ref.shape[0]):
        for k in range(2):
            for j in range(n_slabs):
                y_ref[bi, k * h:(k + 1) * h, j * LANES:(j + 1) * LANES] = il_ref[
                    j, pl.ds(2 * bi + k, h, stride=s), :]


def _ffn_kernel(x_ref, ohg_ref, y_ref, wout32_ref, gf_ref, wup_ref, cw_ref, cb_ref, wdn32_ref, g_ref, o_ref,
                wout_ref, wdn_ref, tail_ref, h_ref, hn_ref, pre_ref, act_ref, *, n_blk):
    @pl.when((pl.program_id(0) == 0) & (pl.program_id(1) == 0))
    def _():
        wout_ref[...] = wout32_ref[...].astype(BF16)
        wdn_ref[...] = wdn32_ref[...].astype(BF16)

    @pl.when(pl.program_id(1) == 0)
    def _():
        tail_ref[...] = jnp.zeros_like(tail_ref)

    mixed = jnp.concatenate([ohg_ref[...].astype(BF16), y_ref[...].astype(BF16)], axis=1)
    hres = x_ref[...] + jnp.dot(mixed, wout_ref[...], preferred_element_type=F32)
    h_ref[...] = hres
    hn_ref[...] = _rms(hres, gf_ref[...]).astype(BF16)

    tm = act_ref.shape[0]
    bf = pre_ref.shape[-1]
    rows = FFN_ROWS
    top = lax.broadcasted_iota(jnp.int32, (SUBLANES, bf), 0)

    cols = lambda j, part: slice((part * n_blk + j) * bf, (part * n_blk + j + 1) * bf)

    def up(j, slot):
        for part in range(2):
            pre_ref[slot, part, rows:rows + tm] = jnp.dot(hn_ref[...], wup_ref[:, cols(j, part)],
                                                          preferred_element_type=F32)

    def mid(j, slot):
        taps, prev = [], []
        for part in range(2):
            blk = cols(j, part)
            pre_ref[slot, part, 0:rows] = tail_ref[:, blk]
            tail_ref[:, blk] = pre_ref[slot, part, tm:tm + rows]
            cw = cw_ref[:, blk]
            taps.append((cb_ref[:, blk], cw[0:1, :], cw[1:2, :], cw[2:3, :]))
            p = pre_ref[slot, part, 0:rows]
            prev.append((pltpu.roll(p, 1, axis=0), pltpu.roll(p, 2, axis=0)))
        for g in range(tm // rows):
            r0 = rows * (g + 1)
            outs = []
            for part in range(2):
                cur = pre_ref[slot, part, r0:r0 + rows]
                rc = (pltpu.roll(cur, 1, axis=0), pltpu.roll(cur, 2, axis=0))
                sh = [jnp.concatenate([jnp.where(top < d + 1, prev[part][d][0:SUBLANES], rc[d][0:SUBLANES]),
                                       rc[d][SUBLANES:]], axis=0) for d in range(2)]
                prev[part] = rc
                cb, w0, w1, w2 = taps[part]
                outs.append(cb + w0 * sh[1] + w1 * sh[0] + w2 * cur)
            gate, val = outs
            act_ref[rows * g:rows * (g + 1), j * bf:(j + 1) * bf] = (
                (gate * jax.nn.sigmoid(gate)) * val).astype(BF16)

    up(0, 0)
    for j in range(n_blk):
        if j + 1 < n_blk:
            up(j + 1, (j + 1) % 2)
        mid(j, j % 2)
    out = h_ref[...] + jnp.dot(act_ref[...], wdn_ref[...], preferred_element_type=F32)
    o_ref[...] = _rms(out, g_ref[...])


def _tile_sizes(seq_len):
    tm = min(512, seq_len)
    tt = 2 * S5_SEG
    assert seq_len % tm == 0 and seq_len % tt == 0 and tt % HG_CHUNK == 0 and tm % FFN_ROWS == 0
    return tm, tt


def kernel(x, in_norm_g, w_in, hg_lb, hg_norm_g, s5_a_re, s5_a_im, s5_log_dt, s5_b_re, s5_b_im,
           s5_c_re, s5_c_im, s5_d, s5_w_glu, s5_b_glu, w_out, ffn_norm_g, w_up, conv_w, conv_b,
           w_down, final_norm_g):
    bsz, seq_len, d_model = x.shape
    depth = w_in.shape[0]
    assert depth == 1, "single-layer block"
    width = hg_norm_g.shape[1]
    assert w_in.shape[2] == 5 * width and s5_d.shape[1] == width
    n_heads = width // HG_HEAD_DIM
    groups, n_state, n_chan = s5_b_re.shape[1:]
    assert (n_state, n_chan) == (S5_STATE, S5_GROUP) and groups * n_chan == width
    assert 2 * bsz == S5_SEQS
    d_ff = w_down.shape[1]
    tm, tt = _tile_sizes(seq_len)
    n_tiles = seq_len // tt
    row = lambda a: a.reshape(1, -1)

    gp = groups * n_chan
    gpb = LANES // n_chan
    n_blocks = groups // gpb
    n_lanes = groups * n_state
    a_dt = jnp.stack([s5_a_re[0].reshape(n_lanes), s5_a_im[0].reshape(n_lanes),
                      jnp.repeat(s5_log_dt[0], n_state)])
    b_re, b_im = s5_b_re[0].reshape(n_lanes, n_chan), s5_b_im[0].reshape(n_lanes, n_chan)
    c_re, c_im = s5_c_re[0].reshape(gp, n_state), s5_c_im[0].reshape(gp, n_state)

    tok = lambda w: pl.BlockSpec((None, tm, w), lambda b, t: (b, t, 0))
    const2 = lambda shape: pl.BlockSpec(shape, lambda b, t: (0,) * len(shape))
    resident = lambda shape: pl.BlockSpec(shape, lambda b, t: (0,) * len(shape), pipeline_mode=pl.Buffered(1))

    m_rows = S5_SEG * S5_SEQS
    c1 = lambda shape: pl.BlockSpec(shape, lambda j: (0,) * len(shape))
    res1 = lambda shape: pl.BlockSpec(shape, lambda j: (0,) * len(shape), pipeline_mode=pl.Buffered(1))
    tile_spec = lambda w: pl.BlockSpec((bsz, tt, w), lambda j: (0, j, 0))
    act = lambda dt: jax.ShapeDtypeStruct((bsz, seq_len, width), dt)
    assert d_model % n_tiles == 0
    wup_slab = pl.BlockSpec((d_model // n_tiles, 2 * d_ff), lambda j: (j, 0))
    o_hg, y_s5, w_up_bf16 = pl.pallas_call(
        functools.partial(_mixer_kernel, n_blocks=n_blocks),
        grid=(n_tiles,),
        in_specs=[tile_spec(d_model), c1((1, d_model)), res1((d_model, 5 * width)),
                  c1(a_dt.shape), c1(b_re.shape), c1(b_im.shape), c1(c_re.shape), c1(c_im.shape),
                  c1((1, width)), res1((width, width)), c1((1, width)), c1(hg_lb.shape), c1((1, width)),
                  wup_slab],
        out_specs=[tile_spec(width)] * 2 + [wup_slab],
        out_shape=[act(F32)] * 2 + [jax.ShapeDtypeStruct((d_model, 2 * d_ff), BF16)],
        scratch_shapes=[pltpu.VMEM((d_model, 5 * width), BF16), pltpu.VMEM((width, width), BF16),
                        pltpu.VMEM((n_blocks, LANES, 2 * gpb * n_state), BF16),
                        pltpu.VMEM((n_blocks, 2 * gpb * n_state, LANES), BF16),
                        pltpu.VMEM((S5_SEQS, n_lanes), F32), pltpu.VMEM((S5_SEQS, n_lanes), F32),
                        pltpu.VMEM((1, width), F32),
                        pltpu.VMEM((width // LANES, m_rows, LANES), F32), pltpu.VMEM((m_rows, 2 * n_lanes), F32),
                        pltpu.VMEM((m_rows, n_lanes), F32), pltpu.VMEM((m_rows, n_lanes), F32),
                        pltpu.VMEM((S5_SEQS, 2 * n_lanes), F32),
                        pltpu.VMEM((bsz, n_heads, HG_HEAD_DIM, HG_HEAD_DIM), F32)],
        compiler_params=_cparams("arbitrary"),
        name="mixer",
    )(x, row(in_norm_g[0]), w_in[0], a_dt, b_re, b_im, c_re, c_im, row(s5_d[0]), s5_w_glu[0], row(s5_b_glu[0]),
      hg_lb, row(hg_norm_g[0]), w_up[0])

    bf = FFN_BLOCK
    assert d_ff % bf == 0
    n_blk = d_ff // bf
    out = pl.pallas_call(
        functools.partial(_ffn_kernel, n_blk=n_blk),
        grid=(bsz, seq_len // tm),
        in_specs=[tok(d_model), tok(width), tok(width), resident((2 * width, d_model)), const2((1, d_model)),
                  resident((d_model, 2 * d_ff)), const2((CONV_WIDTH, 2 * d_ff)),
                  const2((1, 2 * d_ff)), resident((d_ff, d_model)), const2((1, d_model))],
        out_specs=tok(d_model),
        out_shape=jax.ShapeDtypeStruct((bsz, seq_len, d_model), F32),
        scratch_shapes=[pltpu.VMEM((2 * width, d_model), BF16), pltpu.VMEM((d_ff, d_model), BF16),
                        pltpu.VMEM((FFN_ROWS, 2 * d_ff), F32), pltpu.VMEM((tm, d_model), F32),
                        pltpu.VMEM((tm, d_model), BF16), pltpu.VMEM((2, 2, tm + FFN_ROWS, bf), F32),
                        pltpu.VMEM((tm, d_ff), BF16)],
        compiler_params=_cparams("arbitrary", "arbitrary"),
        name="ffn",
    )(x, o_hg, y_s5, w_out[0], row(ffn_norm_g[0]), w_up_bf16, conv_w[0], row(conv_b[0]),
      w_down[0], row(final_norm_g))
    return out
```

```python
import functools

import jax
import jax.numpy as jnp
from jax import lax
from jax.experimental import pallas as pl
from jax.experimental.pallas import tpu as pltpu

EPS = 1e-6
HG_CHUNK = 64
HG_HEAD_DIM = 128
S5_GROUP = 16
S5_STATE = 64
S5_SEG = 64
S5_SEQS = 8
S5_SCAN_BLOCKS = 2
S5_SCAN_UNROLL = True
CONV_WIDTH = 3
FFN_BLOCK = 256
FFN_ROWS = 16
SUBLANES = 8
LANES = 128
VMEM_LIMIT = 60 * 1024 * 1024

F32 = jnp.float32
BF16 = jnp.bfloat16

_NT = (((1,), (1,)), ((), ()))
_TN = (((0,), (0,)), ((), ()))


def _rms(x, g):
    ms = jnp.mean(x * x, axis=-1, keepdims=True)
    return x * lax.rsqrt(ms + EPS) * g


def _cparams(*sem):
    return pltpu.CompilerParams(dimension_semantics=sem, vmem_limit_bytes=VMEM_LIMIT)


def _cumsum_rows(x):
    n, w = x.shape
    top = lax.broadcasted_iota(jnp.int32, (SUBLANES, w), 0)
    sh = 1
    while sh < n:
        if sh < SUBLANES:
            r = pltpu.roll(x, sh, axis=0)
            r = jnp.concatenate([jnp.where(top >= sh, r[0:SUBLANES], 0.0), r[SUBLANES:]], axis=0)
        else:
            r = jnp.concatenate([jnp.zeros((sh, w), x.dtype), x[:n - sh]], axis=0)
        x = x + r
        sh *= 2
    return x


def _hgrn2_rows(q, fz, v, gz, lb, ng, st_ref, o_ref):
    c = HG_CHUNK
    n_chunks = q.shape[0] // c
    n_heads = st_ref.shape[0]
    tri = lax.broadcasted_iota(jnp.int32, (c, c), 0) >= lax.broadcasted_iota(jnp.int32, (c, c), 1)

    f = lb + (1.0 - lb) * jax.nn.sigmoid(fz)
    k = 1.0 - f
    logf = jnp.log(f)
    b = jnp.concatenate([_cumsum_rows(logf[ci * c:(ci + 1) * c]) for ci in range(n_chunks)], axis=0)
    kd_f = k * jnp.exp(-b)
    qd = (q * jnp.exp(b)).astype(BF16)
    kd = kd_f.astype(BF16)
    vb = v.astype(BF16)
    gate = ng * (gz * jax.nn.sigmoid(gz))
    decs = [jnp.exp(b[(ci + 1) * c - 1:(ci + 1) * c, :]) for ci in range(n_chunks)]
    kts = [(kd_f[ci * c:(ci + 1) * c] * decs[ci]).astype(BF16) for ci in range(n_chunks)]

    for hd in range(n_heads):
        cs = slice(hd * HG_HEAD_DIM, (hd + 1) * HG_HEAD_DIM)
        st = st_ref[hd]
        for ci in range(n_chunks):
            rows = slice(ci * c, (ci + 1) * c)
            att = lax.dot_general(qd[rows, cs], kd[rows, cs], _NT, preferred_element_type=F32)
            att = jnp.where(tri, att, 0.0)
            lhs = jnp.concatenate([qd[rows, cs], att.astype(BF16)], axis=1)
            rhs = jnp.concatenate([st.astype(BF16), vb[rows, cs]], axis=0)
            o = jnp.dot(lhs, rhs, preferred_element_type=F32)
            ds = lax.dot_general(kts[ci][:, cs], vb[rows, cs], _TN, preferred_element_type=F32)
            dcol = jnp.transpose(jnp.broadcast_to(decs[ci][:, cs], (HG_HEAD_DIM, HG_HEAD_DIM)))
            st = st * dcol + ds
            o = o * lax.rsqrt(jnp.mean(o * o, axis=-1, keepdims=True) + EPS)
            o_ref[rows, cs] = o * gate[rows, cs]
        st_ref[hd] = st


def _mixer_kernel(x_ref, gin_ref, win32_ref, sre_ref, sim_ref, ldt_ref, bre_ref, bim_ref, cre_ref, cim_ref, d_ref,
                  wglu32_ref, bglu_ref, lbz_ref, ng_ref, wup32_ref, ohg_ref, y_ref, wup_ref,
                  win_ref, wglu_ref, wb_ref, wc_ref, are_ref, aim_ref, lb_ref, il_ref, bu_ref, pre_ref, pim_ref,
                  carry_ref, hst_ref, *, n_blocks):
    wup_ref[...] = wup32_ref[...].astype(BF16)
    h = S5_SEG
    s = S5_SEQS
    half = wb_ref.shape[2] // 2
    kw = wb_ref.shape[1]
    bsz, tt, d_model = x_ref.shape
    width = y_ref.shape[-1]

    @pl.when(pl.program_id(0) == 0)
    def _():
        win_ref[...] = win32_ref[...].astype(BF16)
        wglu_ref[...] = wglu32_ref[...].astype(BF16)
        carry_ref[...] = jnp.zeros_like(carry_ref)
        hst_ref[...] = jnp.zeros_like(hst_ref)
        z = lbz_ref[...]
        e = jnp.exp(z - jnp.max(z, axis=0, keepdims=True))
        lb_ref[...] = (e / jnp.sum(e, axis=0, keepdims=True))[0:1, :]

        a_r = sre_ref[...]
        a_i = sim_ref[...]
        dt = jnp.exp(ldt_ref[...])
        mag = jnp.exp(dt * a_r)
        abr = mag * jnp.cos(dt * a_i)
        abi = mag * jnp.sin(dt * a_i)
        nr = abr - 1.0
        den = a_r * a_r + a_i * a_i
        zr = (nr * a_r + abi * a_i) / den
        zi = (abi * a_r - nr * a_i) / den
        ar = jnp.broadcast_to(abr, are_ref.shape)
        ai = jnp.broadcast_to(abi, aim_ref.shape)
        are_ref[...] = ar
        aim_ref[...] = ai

        same = (lax.broadcasted_iota(jnp.int32, (kw, half), 0) // S5_GROUP
                == lax.broadcasted_iota(jnp.int32, (kw, half), 1) // S5_STATE)

        for m in range(n_blocks):
            bs = slice(half * m, half * (m + 1))
            b_r, b_i, c_r, c_i = (jnp.concatenate([ref[kw * m:kw * (m + 1), :]] * (half // S5_STATE), axis=1)
                                  for ref in (bre_ref, bim_ref, cre_ref, cim_ref))
            wb_ref[m, :, 0:half] = jnp.where(same, zr[:, bs] * b_r - zi[:, bs] * b_i, 0.0).astype(BF16)
            wb_ref[m, :, half:2 * half] = jnp.where(same, zr[:, bs] * b_i + zi[:, bs] * b_r, 0.0).astype(BF16)
            c_r = jnp.where(same, c_r, 0.0)
            c_i = jnp.where(same, c_i, 0.0)
            wc_ref[m, 0:half, :] = jnp.transpose(c_r).astype(BF16)
            wc_ref[m, half:2 * half, :] = (-jnp.transpose(c_i)).astype(BF16)
        pre_ref[0:s, :] = ar
        pim_ref[0:s, :] = ai

        def powers(i, c):
            pr, pi = c
            npr = pr * ar - pi * ai
            npi = pr * ai + pi * ar
            r = pl.ds(pl.multiple_of(i * s, s), s)
            pre_ref[r, :] = npr
            pim_ref[r, :] = npi
            return npr, npi

        lax.fori_loop(1, h, powers, (ar, ai))

    xb = _rms(x_ref[...].reshape(bsz * tt, d_model), gin_ref[...]).astype(BF16)
    u_nat = jnp.dot(xb, win_ref[:, 4 * width:5 * width], preferred_element_type=F32)

    n_slabs = il_ref.shape[0]
    for bi in range(bsz):
        for k in range(2):
            r0 = bi * tt + k * h
            for j in range(n_slabs):
                il_ref[j, pl.ds(2 * bi + k, h, stride=s), :] = u_nat[r0:r0 + h, j * LANES:(j + 1) * LANES]
    u = jnp.concatenate([il_ref[j] for j in range(n_slabs)], axis=1)
    ub = u.astype(BF16)
    for m in range(n_blocks):
        bu_ref[:, 2 * half * m:2 * half * (m + 1)] = jnp.dot(
            ub[:, kw * m:kw * (m + 1)], wb_ref[m], preferred_element_type=F32)

    q, fz, v, gz = (jnp.dot(xb, win_ref[:, i * width:(i + 1) * width], preferred_element_type=F32)
                    for i in range(4))
    for bi in range(bsz):
        r = slice(bi * tt, (bi + 1) * tt)
        _hgrn2_rows(q[r], fz[r], v[r], gz[r], lb_ref[...], ng_ref[...], hst_ref.at[bi], ohg_ref.at[bi])

    second = lax.broadcasted_iota(jnp.int32, (s, half), 0) % 2 == 1
    last = slice((h - 1) * s, h * s)

    for m0 in range(0, n_blocks, S5_SCAN_BLOCKS):
        blocks = []
        for m in range(m0, m0 + S5_SCAN_BLOCKS):
            lo = 2 * half * m
            blocks.append((slice(lo, lo + half), slice(lo + half, lo + 2 * half),
                           slice(half * m, half * (m + 1))))
        coef = [(are_ref[:, am], aim_ref[:, am]) for _, _, am in blocks]

        def scan(i, c, blocks=blocks, coef=coef):
            r = pl.ds(pl.multiple_of(i * s, s), s)
            out = []
            for (re, im, _), (ar, ai), (xr, xi) in zip(blocks, coef, c):
                nxr = ar * xr - ai * xi + bu_ref[r, re]
                nxi = ar * xi + ai * xr + bu_ref[r, im]
                bu_ref[r, re] = nxr
                bu_ref[r, im] = nxi
                out.append((nxr, nxi))
            return tuple(out)

        init = tuple((carry_ref[:, re], carry_ref[:, im]) for re, im, _ in blocks)
        ends = lax.fori_loop(0, h, scan, init, unroll=S5_SCAN_UNROLL)
        starts = [(jnp.where(second, pltpu.roll(er, 1, axis=0), 0.0),
                   jnp.where(second, pltpu.roll(ei, 1, axis=0), 0.0)) for er, ei in ends]

        def fix(i, c, blocks=blocks, starts=starts):
            r = pl.ds(pl.multiple_of(i * s, s), s)
            for (re, im, am), (cr, ci) in zip(blocks, starts):
                pr = pre_ref[r, am]
                pi = pim_ref[r, am]
                bu_ref[r, re] = bu_ref[r, re] + (pr * cr - pi * ci)
                bu_ref[r, im] = bu_ref[r, im] + (pr * ci + pi * cr)
            return c

        lax.fori_loop(0, h, fix, 0, unroll=S5_SCAN_UNROLL)
        for re, im, _ in blocks:
            carry_ref[:, re] = jnp.where(second, 0.0, pltpu.roll(bu_ref[last, re], s - 1, axis=0))
            carry_ref[:, im] = jnp.where(second, 0.0, pltpu.roll(bu_ref[last, im], s - 1, axis=0))

    ys = [jnp.dot(bu_ref[:, 2 * half * m:2 * half * (m + 1)].astype(BF16), wc_ref[m],
                  preferred_element_type=F32) for m in range(n_blocks)]
    y = jnp.concatenate(ys, axis=1) + d_ref[...] * u
    y = jax.nn.gelu(y)
    z = jnp.dot(y.astype(BF16), wglu_ref[...], preferred_element_type=F32) + bglu_ref[...]
    y = y * jax.nn.sigmoid(z)
    for j in range(n_slabs):
        il_ref[j] = y[:, j * LANES:(j + 1) * LANES]
    for bi in range(y_ref.shape[0]):
        for k in range(2):
            for j in range(n_slabs):
                y_ref[bi, k * h:(k + 1) * h, j * LANES:(j + 1) * LANES] = il_ref[
                    j, pl.ds(2 * bi + k, h, stride=s), :]


def _ffn_kernel(x_ref, ohg_ref, y_ref, wout32_ref, gf_ref, wup_ref, cw_ref, cb_ref, wdn32_ref, g_ref, o_ref,
                wout_ref, wdn_ref, tail_ref, h_ref, hn_ref, pre_ref, act_ref, *, n_blk):
    @pl.when((pl.program_id(0) == 0) & (pl.program_id(1) == 0))
    def _():
        wout_ref[...] = wout32_ref[...].astype(BF16)
        wdn_ref[...] = wdn32_ref[...].astype(BF16)

    @pl.when(pl.program_id(1) == 0)
    def _():
        tail_ref[...] = jnp.zeros_like(tail_ref)

    mixed = jnp.concatenate([ohg_ref[...].astype(BF16), y_ref[...].astype(BF16)], axis=1)
    hres = x_ref[...] + jnp.dot(mixed, wout_ref[...], preferred_element_type=F32)
    h_ref[...] = hres
    hn_ref[...] = _rms(hres, gf_ref[...]).astype(BF16)

    tm = act_ref.shape[0]
    bf = pre_ref.shape[-1]
    rows = FFN_ROWS
    top = lax.broadcasted_iota(jnp.int32, (SUBLANES, bf), 0)

    cols = lambda j, part: slice((part * n_blk + j) * bf, (part * n_blk + j + 1) * bf)

    def up(j, slot):
        for part in range(2):
            pre_ref[slot, part, rows:rows + tm] = jnp.dot(hn_ref[...], wup_ref[:, cols(j, part)],
                                                          preferred_element_type=F32)

    def mid(j, slot):
        taps, prev = [], []
        for part in range(2):
            blk = cols(j, part)
            pre_ref[slot, part, 0:rows] = tail_ref[:, blk]
            tail_ref[:, blk] = pre_ref[slot, part, tm:tm + rows]
            cw = cw_ref[:, blk]
            taps.append((cb_ref[:, blk], cw[0:1, :], cw[1:2, :], cw[2:3, :]))
            p = pre_ref[slot, part, 0:rows]
            prev.append((pltpu.roll(p, 1, axis=0), pltpu.roll(p, 2, axis=0)))
        for g in range(tm // rows):
            r0 = rows * (g + 1)
            outs = []
            for part in range(2):
                cur = pre_ref[slot, part, r0:r0 + rows]
                rc = (pltpu.roll(cur, 1, axis=0), pltpu.roll(cur, 2, axis=0))
                sh = [jnp.concatenate([jnp.where(top < d + 1, prev[part][d][0:SUBLANES], rc[d][0:SUBLANES]),
                                       rc[d][SUBLANES:]], axis=0) for d in range(2)]
                prev[part] = rc
                cb, w0, w1, w2 = taps[part]
                outs.append(cb + w0 * sh[1] + w1 * sh[0] + w2 * cur)
            gate, val = outs
            act_ref[rows * g:rows * (g + 1), j * bf:(j + 1) * bf] = (
                (gate * jax.nn.sigmoid(gate)) * val).astype(BF16)

    up(0, 0)
    for j in range(n_blk):
        if j + 1 < n_blk:
            up(j + 1, (j + 1) % 2)
        mid(j, j % 2)
    out = h_ref[...] + jnp.dot(act_ref[...], wdn_ref[...], preferred_element_type=F32)
    o_ref[...] = _rms(out, g_ref[...])


def _tile_sizes(seq_len):
    tm = min(512, seq_len)
    tt = 2 * S5_SEG
    assert seq_len % tm == 0 and seq_len % tt == 0 and tt % HG_CHUNK == 0 and tm % FFN_ROWS == 0
    return tm, tt


def kernel(x, in_norm_g, w_in, hg_lb, hg_norm_g, s5_a_re, s5_a_im, s5_log_dt, s5_b_re, s5_b_im,
           s5_c_re, s5_c_im, s5_d, s5_w_glu, s5_b_glu, w_out, ffn_norm_g, w_up, conv_w, conv_b,
           w_down, final_norm_g):
    bsz, seq_len, d_model = x.shape
    depth = w_in.shape[0]
    assert depth == 1, "single-layer block"
    width = hg_norm_g.shape[1]
    assert w_in.shape[2] == 5 * width and s5_d.shape[1] == width
    n_heads = width // HG_HEAD_DIM
    groups, n_state, n_chan = s5_b_re.shape[1:]
    assert (n_state, n_chan) == (S5_STATE, S5_GROUP) and groups * n_chan == width
    assert 2 * bsz == S5_SEQS
    d_ff = w_down.shape[1]
    tm, tt = _tile_sizes(seq_len)
    n_tiles = seq_len // tt
    row = lambda a: a.reshape(1, -1)

    gp = groups * n_chan
    gpb = LANES // n_chan
    n_blocks = groups // gpb
    n_lanes = groups * n_state
    a_re, a_im = row(s5_a_re[0]), row(s5_a_im[0])
    ldt = row(jnp.repeat(s5_log_dt[0], n_state))
    b_t = lambda a: a.transpose(0, 2, 1).reshape(gp, n_state)
    b_re, b_im = b_t(s5_b_re[0]), b_t(s5_b_im[0])
    c_re, c_im = s5_c_re[0].reshape(gp, n_state), s5_c_im[0].reshape(gp, n_state)

    tok = lambda w: pl.BlockSpec((None, tm, w), lambda b, t: (b, t, 0))
    const2 = lambda shape: pl.BlockSpec(shape, lambda b, t: (0,) * len(shape))
    resident = lambda shape: pl.BlockSpec(shape, lambda b, t: (0,) * len(shape), pipeline_mode=pl.Buffered(1))

    m_rows = S5_SEG * S5_SEQS
    c1 = lambda shape: pl.BlockSpec(shape, lambda j: (0,) * len(shape))
    res1 = lambda shape: pl.BlockSpec(shape, lambda j: (0,) * len(shape), pipeline_mode=pl.Buffered(1))
    tile_spec = lambda w: pl.BlockSpec((bsz, tt, w), lambda j: (0, j, 0))
    act = lambda dt: jax.ShapeDtypeStruct((bsz, seq_len, width), dt)
    assert d_model % n_tiles == 0
    wup_slab = pl.BlockSpec((d_model // n_tiles, 2 * d_ff), lambda j: (j, 0))
    o_hg, y_s5, w_up_bf16 = pl.pallas_call(
        functools.partial(_mixer_kernel, n_blocks=n_blocks),
        grid=(n_tiles,),
        in_specs=[tile_spec(d_model), c1((1, d_model)), res1((d_model, 5 * width)),
                  c1(a_re.shape), c1(a_im.shape), c1(ldt.shape), c1(b_re.shape), c1(b_im.shape), c1(c_re.shape), c1(c_im.shape),
                  c1((1, width)), res1((width, width)), c1((1, width)), c1(hg_lb.shape), c1((1, width)),
                  wup_slab],
        out_specs=[tile_spec(width)] * 2 + [wup_slab],
        out_shape=[act(F32)] * 2 + [jax.ShapeDtypeStruct((d_model, 2 * d_ff), BF16)],
        scratch_shapes=[pltpu.VMEM((d_model, 5 * width), BF16), pltpu.VMEM((width, width), BF16),
                        pltpu.VMEM((n_blocks, LANES, 2 * gpb * n_state), BF16),
                        pltpu.VMEM((n_blocks, 2 * gpb * n_state, LANES), BF16),
                        pltpu.VMEM((S5_SEQS, n_lanes), F32), pltpu.VMEM((S5_SEQS, n_lanes), F32),
                        pltpu.VMEM((1, width), F32),
                        pltpu.VMEM((width // LANES, m_rows, LANES), F32), pltpu.VMEM((m_rows, 2 * n_lanes), F32),
                        pltpu.VMEM((m_rows, n_lanes), F32), pltpu.VMEM((m_rows, n_lanes), F32),
                        pltpu.VMEM((S5_SEQS, 2 * n_lanes), F32),
                        pltpu.VMEM((bsz, n_heads, HG_HEAD_DIM, HG_HEAD_DIM), F32)],
        compiler_params=_cparams("arbitrary"),
        name="mixer",
    )(x, row(in_norm_g[0]), w_in[0], a_re, a_im, ldt, b_re, b_im, c_re, c_im, row(s5_d[0]), s5_w_glu[0], row(s5_b_glu[0]),
      hg_lb, row(hg_norm_g[0]), w_up[0])

    bf = FFN_BLOCK
    assert d_ff % bf == 0
    n_blk = d_ff // bf
    out = pl.pallas_call(
        functools.partial(_ffn_kernel, n_blk=n_blk),
        grid=(bsz, seq_len // tm),
        in_specs=[tok(d_model), tok(width), tok(width), resident((2 * width, d_model)), const2((1, d_model)),
                  resident((d_model, 2 * d_ff)), const2((CONV_WIDTH, 2 * d_ff)),
                  const2((1, 2 * d_ff)), resident((d_ff, d_model)), const2((1, d_model))],
        out_specs=tok(d_model),
        out_shape=jax.ShapeDtypeStruct((bsz, seq_len, d_model), F32),
        scratch_shapes=[pltpu.VMEM((2 * width, d_model), BF16), pltpu.VMEM((d_ff, d_model), BF16),
                        pltpu.VMEM((FFN_ROWS, 2 * d_ff), F32), pltpu.VMEM((tm, d_model), F32),
                        pltpu.VMEM((tm, d_model), BF16), pltpu.VMEM((2, 2, tm + FFN_ROWS, bf), F32),
                        pltpu.VMEM((tm, d_ff), BF16)],
        compiler_params=_cparams("arbitrary", "arbitrary"),
        name="ffn",
    )(x, o_hg, y_s5, w_out[0], row(ffn_norm_g[0]), w_up_bf16, conv_w[0], row(conv_b[0]),
      w_down[0], row(final_norm_g))
    return out
```

```python
import functools

import jax
import jax.numpy as jnp
from jax import lax
from jax.experimental import pallas as pl
from jax.experimental.pallas import tpu as pltpu

EPS = 1e-6
HG_CHUNK = 64
HG_HEAD_DIM = 128
S5_GROUP = 16
S5_STATE = 64
S5_SEG = 64
S5_SEQS = 8
S5_SCAN_BLOCKS = 2
S5_SCAN_UNROLL = True
CONV_WIDTH = 3
FFN_BLOCK = 256
FFN_ROWS = 16
SUBLANES = 8
LANES = 128
VMEM_LIMIT = 60 * 1024 * 1024

F32 = jnp.float32
BF16 = jnp.bfloat16

_NT = (((1,), (1,)), ((), ()))
_TN = (((0,), (0,)), ((), ()))


def _rms(x, g):
    ms = jnp.mean(x * x, axis=-1, keepdims=True)
    return x * lax.rsqrt(ms + EPS) * g


def _cparams(*sem):
    return pltpu.CompilerParams(dimension_semantics=sem, vmem_limit_bytes=VMEM_LIMIT)


def _cumsum_rows(x):
    n, w = x.shape
    top = lax.broadcasted_iota(jnp.int32, (SUBLANES, w), 0)
    sh = 1
    while sh < n:
        if sh < SUBLANES:
            r = pltpu.roll(x, sh, axis=0)
            r = jnp.concatenate([jnp.where(top >= sh, r[0:SUBLANES], 0.0), r[SUBLANES:]], axis=0)
        else:
            r = jnp.concatenate([jnp.zeros((sh, w), x.dtype), x[:n - sh]], axis=0)
        x = x + r
        sh *= 2
    return x


def _hgrn2_rows(q, fz, v, gz, lb, ng, st_ref, o_ref):
    c = HG_CHUNK
    n_chunks = q.shape[0] // c
    n_heads = st_ref.shape[0]
    tri = lax.broadcasted_iota(jnp.int32, (c, c), 0) >= lax.broadcasted_iota(jnp.int32, (c, c), 1)

    f = lb + (1.0 - lb) * jax.nn.sigmoid(fz)
    k = 1.0 - f
    logf = jnp.log(f)
    b = jnp.concatenate([_cumsum_rows(logf[ci * c:(ci + 1) * c]) for ci in range(n_chunks)], axis=0)
    kd_f = k * jnp.exp(-b)
    qd = (q * jnp.exp(b)).astype(BF16)
    kd = kd_f.astype(BF16)
    vb = v.astype(BF16)
    gate = ng * (gz * jax.nn.sigmoid(gz))
    decs = [jnp.exp(b[(ci + 1) * c - 1:(ci + 1) * c, :]) for ci in range(n_chunks)]
    kts = [(kd_f[ci * c:(ci + 1) * c] * decs[ci]).astype(BF16) for ci in range(n_chunks)]

    for hd in range(n_heads):
        cs = slice(hd * HG_HEAD_DIM, (hd + 1) * HG_HEAD_DIM)
        st = st_ref[hd]
        for ci in range(n_chunks):
            rows = slice(ci * c, (ci + 1) * c)
            att = lax.dot_general(qd[rows, cs], kd[rows, cs], _NT, preferred_element_type=F32)
            att = jnp.where(tri, att, 0.0)
            lhs = jnp.concatenate([qd[rows, cs], att.astype(BF16)], axis=1)
            rhs = jnp.concatenate([st.astype(BF16), vb[rows, cs]], axis=0)
            o = jnp.dot(lhs, rhs, preferred_element_type=F32)
            ds = lax.dot_general(kts[ci][:, cs], vb[rows, cs], _TN, preferred_element_type=F32)
            dcol = jnp.transpose(jnp.broadcast_to(decs[ci][:, cs], (HG_HEAD_DIM, HG_HEAD_DIM)))
            st = st * dcol + ds
            o = o * lax.rsqrt(jnp.mean(o * o, axis=-1, keepdims=True) + EPS)
            o_ref[rows, cs] = o * gate[rows, cs]
        st_ref[hd] = st


def _mixer_kernel(x_ref, gin_ref, win32_ref, sre_ref, sim_ref, ldt_ref, bre_ref, bim_ref, cre_ref, cim_ref, d_ref,
                  wglu32_ref, bglu_ref, lbz_ref, ng_ref, wup32_ref, ohg_ref, y_ref, wup_ref,
                  win_ref, wglu_ref, wb_ref, wc_ref, are_ref, aim_ref, lb_ref, il_ref, bu_ref, pre_ref, pim_ref,
                  carry_ref, hst_ref, *, n_blocks):
    wup_ref[...] = wup32_ref[...].astype(BF16)
    h = S5_SEG
    s = S5_SEQS
    half = wb_ref.shape[2] // 2
    kw = wb_ref.shape[1]
    bsz, tt, d_model = x_ref.shape
    width = y_ref.shape[-1]

    @pl.when(pl.program_id(0) == 0)
    def _():
        win_ref[...] = win32_ref[...].astype(BF16)
        wglu_ref[...] = wglu32_ref[...].astype(BF16)
        carry_ref[...] = jnp.zeros_like(carry_ref)
        hst_ref[...] = jnp.zeros_like(hst_ref)
        z = lbz_ref[...]
        e = jnp.exp(z - jnp.max(z, axis=0, keepdims=True))
        lb_ref[...] = (e / jnp.sum(e, axis=0, keepdims=True))[0:1, :]

        a_r = sre_ref[...]
        a_i = sim_ref[...]
        dt = jnp.exp(ldt_ref[...])
        mag = jnp.exp(dt * a_r)
        abr = mag * jnp.cos(dt * a_i)
        abi = mag * jnp.sin(dt * a_i)
        nr = abr - 1.0
        den = a_r * a_r + a_i * a_i
        flat = lambda v: jnp.concatenate([v[g:g + 1, :] for g in range(v.shape[0])], axis=1)
        zr = flat((nr * a_r + abi * a_i) / den)
        zi = flat((abi * a_r - nr * a_i) / den)
        abr = flat(abr)
        abi = flat(abi)
        ar = jnp.broadcast_to(abr, are_ref.shape)
        ai = jnp.broadcast_to(abi, aim_ref.shape)
        are_ref[...] = ar
        aim_ref[...] = ai

        same = (lax.broadcasted_iota(jnp.int32, (kw, half), 0) // S5_GROUP
                == lax.broadcasted_iota(jnp.int32, (kw, half), 1) // S5_STATE)

        for m in range(n_blocks):
            bs = slice(half * m, half * (m + 1))
            b_r, b_i, c_r, c_i = (jnp.concatenate([ref[kw * m:kw * (m + 1), :]] * (half // S5_STATE), axis=1)
                                  for ref in (bre_ref, bim_ref, cre_ref, cim_ref))
            wb_ref[m, :, 0:half] = jnp.where(same, zr[:, bs] * b_r - zi[:, bs] * b_i, 0.0).astype(BF16)
            wb_ref[m, :, half:2 * half] = jnp.where(same, zr[:, bs] * b_i + zi[:, bs] * b_r, 0.0).astype(BF16)
            c_r = jnp.where(same, c_r, 0.0)
            c_i = jnp.where(same, c_i, 0.0)
            wc_ref[m, 0:half, :] = jnp.transpose(c_r).astype(BF16)
            wc_ref[m, half:2 * half, :] = (-jnp.transpose(c_i)).astype(BF16)
        pre_ref[0:s, :] = ar
        pim_ref[0:s, :] = ai

        def powers(i, c):
            pr, pi = c
            npr = pr * ar - pi * ai
            npi = pr * ai + pi * ar
            r = pl.ds(pl.multiple_of(i * s, s), s)
            pre_ref[r, :] = npr
            pim_ref[r, :] = npi
            return npr, npi

        lax.fori_loop(1, h, powers, (ar, ai))

    xb = _rms(x_ref[...].reshape(bsz * tt, d_model), gin_ref[...]).astype(BF16)
    u_nat = jnp.dot(xb, win_ref[:, 4 * width:5 * width], preferred_element_type=F32)

    n_slabs = il_ref.shape[0]
    for bi in range(bsz):
        for k in range(2):
            r0 = bi * tt + k * h
            for j in range(n_slabs):
                il_ref[j, pl.ds(2 * bi + k, h, stride=s), :] = u_nat[r0:r0 + h, j * LANES:(j + 1) * LANES]
    u = jnp.concatenate([il_ref[j] for j in range(n_slabs)], axis=1)
    ub = u.astype(BF16)
    for m in range(n_blocks):
        bu_ref[:, 2 * half * m:2 * half * (m + 1)] = jnp.dot(
            ub[:, kw * m:kw * (m + 1)], wb_ref[m], preferred_element_type=F32)

    q, fz, v, gz = (jnp.dot(xb, win_ref[:, i * width:(i + 1) * width], preferred_element_type=F32)
                    for i in range(4))
    for bi in range(bsz):
        r = slice(bi * tt, (bi + 1) * tt)
        _hgrn2_rows(q[r], fz[r], v[r], gz[r], lb_ref[...], ng_ref[...], hst_ref.at[bi], ohg_ref.at[bi])

    second = lax.broadcasted_iota(jnp.int32, (s, half), 0) % 2 == 1
    last = slice((h - 1) * s, h * s)

    for m0 in range(0, n_blocks, S5_SCAN_BLOCKS):
        blocks = []
        for m in range(m0, m0 + S5_SCAN_BLOCKS):
            lo = 2 * half * m
            blocks.append((slice(lo, lo + half), slice(lo + half, lo + 2 * half),
                           slice(half * m, half * (m + 1))))
        coef = [(are_ref[:, am], aim_ref[:, am]) for _, _, am in blocks]

        def scan(i, c, blocks=blocks, coef=coef):
            r = pl.ds(pl.multiple_of(i * s, s), s)
            out = []
            for (re, im, _), (ar, ai), (xr, xi) in zip(blocks, coef, c):
                nxr = ar * xr - ai * xi + bu_ref[r, re]
                nxi = ar * xi + ai * xr + bu_ref[r, im]
                bu_ref[r, re] = nxr
                bu_ref[r, im] = nxi
                out.append((nxr, nxi))
            return tuple(out)

        init = tuple((carry_ref[:, re], carry_ref[:, im]) for re, im, _ in blocks)
        ends = lax.fori_loop(0, h, scan, init, unroll=S5_SCAN_UNROLL)
        starts = [(jnp.where(second, pltpu.roll(er, 1, axis=0), 0.0),
                   jnp.where(second, pltpu.roll(ei, 1, axis=0), 0.0)) for er, ei in ends]

        def fix(i, c, blocks=blocks, starts=starts):
            r = pl.ds(pl.multiple_of(i * s, s), s)
            for (re, im, am), (cr, ci) in zip(blocks, starts):
                pr = pre_ref[r, am]
                pi = pim_ref[r, am]
                bu_ref[r, re] = bu_ref[r, re] + (pr * cr - pi * ci)
                bu_ref[r, im] = bu_ref[r, im] + (pr * ci + pi * cr)
            return c

        lax.fori_loop(0, h, fix, 0, unroll=S5_SCAN_UNROLL)
        for re, im, _ in blocks:
            carry_ref[:, re] = jnp.where(second, 0.0, pltpu.roll(bu_ref[last, re], s - 1, axis=0))
            carry_ref[:, im] = jnp.where(second, 0.0, pltpu.roll(bu_ref[last, im], s - 1, axis=0))

    ys = [jnp.dot(bu_ref[:, 2 * half * m:2 * half * (m + 1)].astype(BF16), wc_ref[m],
                  preferred_element_type=F32) for m in range(n_blocks)]
    y = jnp.concatenate(ys, axis=1) + d_ref[...] * u
    y = jax.nn.gelu(y)
    z = jnp.dot(y.astype(BF16), wglu_ref[...], preferred_element_type=F32) + bglu_ref[...]
    y = y * jax.nn.sigmoid(z)
    for j in range(n_slabs):
        il_ref[j] = y[:, j * LANES:(j + 1) * LANES]
    for bi in range(y_ref.shape[0]):
        for k in range(2):
            for j in range(n_slabs):
                y_ref[bi, k * h:(k + 1) * h, j * LANES:(j + 1) * LANES] = il_ref[
                    j, pl.ds(2 * bi + k, h, stride=s), :]


def _ffn_kernel(x_ref, ohg_ref, y_ref, wout32_ref, gf_ref, wup_ref, cw_ref, cb_ref, wdn32_ref, g_ref, o_ref,
                wout_ref, wdn_ref, tail_ref, h_ref, hn_ref, pre_ref, act_ref, *, n_blk):
    @pl.when((pl.program_id(0) == 0) & (pl.program_id(1) == 0))
    def _():
        wout_ref[...] = wout32_ref[...].astype(BF16)
        wdn_ref[...] = wdn32_ref[...].astype(BF16)

    @pl.when(pl.program_id(1) == 0)
    def _():
        tail_ref[...] = jnp.zeros_like(tail_ref)

    mixed = jnp.concatenate([ohg_ref[...].astype(BF16), y_ref[...].astype(BF16)], axis=1)
    hres = x_ref[...] + jnp.dot(mixed, wout_ref[...], preferred_element_type=F32)
    h_ref[...] = hres
    hn_ref[...] = _rms(hres, gf_ref[...]).astype(BF16)

    tm = act_ref.shape[0]
    bf = pre_ref.shape[-1]
    rows = FFN_ROWS
    top = lax.broadcasted_iota(jnp.int32, (SUBLANES, bf), 0)

    cols = lambda j, part: slice((part * n_blk + j) * bf, (part * n_blk + j + 1) * bf)

    def up(j, slot):
        for part in range(2):
            pre_ref[slot, part, rows:rows + tm] = jnp.dot(hn_ref[...], wup_ref[:, cols(j, part)],
                                                          preferred_element_type=F32)

    def mid(j, slot):
        taps, prev = [], []
        for part in range(2):
            blk = cols(j, part)
            pre_ref[slot, part, 0:rows] = tail_ref[:, blk]
            tail_ref[:, blk] = pre_ref[slot, part, tm:tm + rows]
            cw = cw_ref[:, blk]
            taps.append((cb_ref[:, blk], cw[0:1, :], cw[1:2, :], cw[2:3, :]))
            p = pre_ref[slot, part, 0:rows]
            prev.append((pltpu.roll(p, 1, axis=0), pltpu.roll(p, 2, axis=0)))
        for g in range(tm // rows):
            r0 = rows * (g + 1)
            outs = []
            for part in range(2):
                cur = pre_ref[slot, part, r0:r0 + rows]
                rc = (pltpu.roll(cur, 1, axis=0), pltpu.roll(cur, 2, axis=0))
                sh = [jnp.concatenate([jnp.where(top < d + 1, prev[part][d][0:SUBLANES], rc[d][0:SUBLANES]),
                                       rc[d][SUBLANES:]], axis=0) for d in range(2)]
                prev[part] = rc
                cb, w0, w1, w2 = taps[part]
                outs.append(cb + w0 * sh[1] + w1 * sh[0] + w2 * cur)
            gate, val = outs
            act_ref[rows * g:rows * (g + 1), j * bf:(j + 1) * bf] = (
                (gate * jax.nn.sigmoid(gate)) * val).astype(BF16)

    up(0, 0)
    for j in range(n_blk):
        if j + 1 < n_blk:
            up(j + 1, (j + 1) % 2)
        mid(j, j % 2)
    out = h_ref[...] + jnp.dot(act_ref[...], wdn_ref[...], preferred_element_type=F32)
    o_ref[...] = _rms(out, g_ref[...])


def _tile_sizes(seq_len):
    tm = min(512, seq_len)
    tt = 2 * S5_SEG
    assert seq_len % tm == 0 and seq_len % tt == 0 and tt % HG_CHUNK == 0 and tm % FFN_ROWS == 0
    return tm, tt


def kernel(x, in_norm_g, w_in, hg_lb, hg_norm_g, s5_a_re, s5_a_im, s5_log_dt, s5_b_re, s5_b_im,
           s5_c_re, s5_c_im, s5_d, s5_w_glu, s5_b_glu, w_out, ffn_norm_g, w_up, conv_w, conv_b,
           w_down, final_norm_g):
    bsz, seq_len, d_model = x.shape
    depth = w_in.shape[0]
    assert depth == 1, "single-layer block"
    width = hg_norm_g.shape[1]
    assert w_in.shape[2] == 5 * width and s5_d.shape[1] == width
    n_heads = width // HG_HEAD_DIM
    groups, n_state, n_chan = s5_b_re.shape[1:]
    assert (n_state, n_chan) == (S5_STATE, S5_GROUP) and groups * n_chan == width
    assert 2 * bsz == S5_SEQS
    d_ff = w_down.shape[1]
    tm, tt = _tile_sizes(seq_len)
    n_tiles = seq_len // tt
    row = lambda a: a.reshape(1, -1)

    gp = groups * n_chan
    gpb = LANES // n_chan
    n_blocks = groups // gpb
    n_lanes = groups * n_state
    a_re, a_im = s5_a_re[0], s5_a_im[0]
    ldt = s5_log_dt[0].reshape(groups, 1)
    b_t = lambda a: a.transpose(0, 2, 1).reshape(gp, n_state)
    b_re, b_im = b_t(s5_b_re[0]), b_t(s5_b_im[0])
    c_re, c_im = s5_c_re[0].reshape(gp, n_state), s5_c_im[0].reshape(gp, n_state)

    tok = lambda w: pl.BlockSpec((None, tm, w), lambda b, t: (b, t, 0))
    const2 = lambda shape: pl.BlockSpec(shape, lambda b, t: (0,) * len(shape))
    resident = lambda shape: pl.BlockSpec(shape, lambda b, t: (0,) * len(shape), pipeline_mode=pl.Buffered(1))

    m_rows = S5_SEG * S5_SEQS
    c1 = lambda shape: pl.BlockSpec(shape, lambda j: (0,) * len(shape))
    res1 = lambda shape: pl.BlockSpec(shape, lambda j: (0,) * len(shape), pipeline_mode=pl.Buffered(1))
    tile_spec = lambda w: pl.BlockSpec((bsz, tt, w), lambda j: (0, j, 0))
    act = lambda dt: jax.ShapeDtypeStruct((bsz, seq_len, width), dt)
    assert d_model % n_tiles == 0
    wup_slab = pl.BlockSpec((d_model // n_tiles, 2 * d_ff), lambda j: (j, 0))
    o_hg, y_s5, w_up_bf16 = pl.pallas_call(
        functools.partial(_mixer_kernel, n_blocks=n_blocks),
        grid=(n_tiles,),
        in_specs=[tile_spec(d_model), c1((1, d_model)), res1((d_model, 5 * width)),
                  c1(a_re.shape), c1(a_im.shape), c1(ldt.shape), c1(b_re.shape), c1(b_im.shape), c1(c_re.shape), c1(c_im.shape),
                  c1((1, width)), res1((width, width)), c1((1, width)), c1(hg_lb.shape), c1((1, width)),
                  wup_slab],
        out_specs=[tile_spec(width)] * 2 + [wup_slab],
        out_shape=[act(F32)] * 2 + [jax.ShapeDtypeStruct((d_model, 2 * d_ff), BF16)],
        scratch_shapes=[pltpu.VMEM((d_model, 5 * width), BF16), pltpu.VMEM((width, width), BF16),
                        pltpu.VMEM((n_blocks, LANES, 2 * gpb * n_state), BF16),
                        pltpu.VMEM((n_blocks, 2 * gpb * n_state, LANES), BF16),
                        pltpu.VMEM((S5_SEQS, n_lanes), F32), pltpu.VMEM((S5_SEQS, n_lanes), F32),
                        pltpu.VMEM((1, width), F32),
                        pltpu.VMEM((width // LANES, m_rows, LANES), F32), pltpu.VMEM((m_rows, 2 * n_lanes), F32),
                        pltpu.VMEM((m_rows, n_lanes), F32), pltpu.VMEM((m_rows, n_lanes), F32),
                        pltpu.VMEM((S5_SEQS, 2 * n_lanes), F32),
                        pltpu.VMEM((bsz, n_heads, HG_HEAD_DIM, HG_HEAD_DIM), F32)],
        compiler_params=_cparams("arbitrary"),
        name="mixer",
    )(x, row(in_norm_g[0]), w_in[0], a_re, a_im, ldt, b_re, b_im, c_re, c_im, row(s5_d[0]), s5_w_glu[0], row(s5_b_glu[0]),
      hg_lb, row(hg_norm_g[0]), w_up[0])

    bf = FFN_BLOCK
    assert d_ff % bf == 0
    n_blk = d_ff // bf
    out = pl.pallas_call(
        functools.partial(_ffn_kernel, n_blk=n_blk),
        grid=(bsz, seq_len // tm),
        in_specs=[tok(d_model), tok(width), tok(width), resident((2 * width, d_model)), const2((1, d_model)),
                  resident((d_model, 2 * d_ff)), const2((CONV_WIDTH, 2 * d_ff)),
                  const2((1, 2 * d_ff)), resident((d_ff, d_model)), const2((1, d_model))],
        out_specs=tok(d_model),
        out_shape=jax.ShapeDtypeStruct((bsz, seq_len, d_model), F32),
        scratch_shapes=[pltpu.VMEM((2 * width, d_model), BF16), pltpu.VMEM((d_ff, d_model), BF16),
                        pltpu.VMEM((FFN_ROWS, 2 * d_ff), F32), pltpu.VMEM((tm, d_model), F32),
                        pltpu.VMEM((tm, d_model), BF16), pltpu.VMEM((2, 2, tm + FFN_ROWS, bf), F32),
                        pltpu.VMEM((tm, d_ff), BF16)],
        compiler_params=_cparams("arbitrary", "arbitrary"),
        name="ffn",
    )(x, o_hg, y_s5, w_out[0], row(ffn_norm_g[0]), w_up_bf16, conv_w[0], row(conv_b[0]),
      w_down[0], row(final_norm_g))
    return out
```

```python
import functools

import jax
import jax.numpy as jnp
from jax import lax
from jax.experimental import pallas as pl
from jax.experimental.pallas import tpu as pltpu

EPS = 1e-6
HG_CHUNK = 64
HG_HEAD_DIM = 128
S5_GROUP = 16
S5_STATE = 64
S5_SEG = 64
S5_SEQS = 8
S5_SCAN_BLOCKS = 2
S5_SCAN_UNROLL = True
CONV_WIDTH = 3
FFN_BLOCK = 256
FFN_ROWS = 16
SUBLANES = 8
LANES = 128
VMEM_LIMIT = 60 * 1024 * 1024

F32 = jnp.float32
BF16 = jnp.bfloat16

_NT = (((1,), (1,)), ((), ()))
_TN = (((0,), (0,)), ((), ()))


def _rms(x, g):
    ms = jnp.mean(x * x, axis=-1, keepdims=True)
    return x * lax.rsqrt(ms + EPS) * g


def _cparams(*sem):
    return pltpu.CompilerParams(dimension_semantics=sem, vmem_limit_bytes=VMEM_LIMIT)


def _cumsum_rows(x):
    n, w = x.shape
    top = lax.broadcasted_iota(jnp.int32, (SUBLANES, w), 0)
    sh = 1
    while sh < n:
        if sh < SUBLANES:
            r = pltpu.roll(x, sh, axis=0)
            r = jnp.concatenate([jnp.where(top >= sh, r[0:SUBLANES], 0.0), r[SUBLANES:]], axis=0)
        else:
            r = jnp.concatenate([jnp.zeros((sh, w), x.dtype), x[:n - sh]], axis=0)
        x = x + r
        sh *= 2
    return x


def _hgrn2_rows(q, fz, v, gz, lb, ng, st_ref, o_ref):
    c = HG_CHUNK
    n_chunks = q.shape[0] // c
    n_heads = st_ref.shape[0]
    tri = lax.broadcasted_iota(jnp.int32, (c, c), 0) >= lax.broadcasted_iota(jnp.int32, (c, c), 1)

    f = lb + (1.0 - lb) * jax.nn.sigmoid(fz)
    k = 1.0 - f
    logf = jnp.log(f)
    b = jnp.concatenate([_cumsum_rows(logf[ci * c:(ci + 1) * c]) for ci in range(n_chunks)], axis=0)
    kd_f = k * jnp.exp(-b)
    qd = (q * jnp.exp(b)).astype(BF16)
    kd = kd_f.astype(BF16)
    vb = v.astype(BF16)
    gate = ng * (gz * jax.nn.sigmoid(gz))
    decs = [jnp.exp(b[(ci + 1) * c - 1:(ci + 1) * c, :]) for ci in range(n_chunks)]
    kts = [(kd_f[ci * c:(ci + 1) * c] * decs[ci]).astype(BF16) for ci in range(n_chunks)]

    for hd in range(n_heads):
        cs = slice(hd * HG_HEAD_DIM, (hd + 1) * HG_HEAD_DIM)
        st = st_ref[hd]
        for ci in range(n_chunks):
            rows = slice(ci * c, (ci + 1) * c)
            att = lax.dot_general(qd[rows, cs], kd[rows, cs], _NT, preferred_element_type=F32)
            att = jnp.where(tri, att, 0.0)
            lhs = jnp.concatenate([qd[rows, cs], att.astype(BF16)], axis=1)
            rhs = jnp.concatenate([st.astype(BF16), vb[rows, cs]], axis=0)
            o = jnp.dot(lhs, rhs, preferred_element_type=F32)
            ds = lax.dot_general(kts[ci][:, cs], vb[rows, cs], _TN, preferred_element_type=F32)
            dcol = jnp.transpose(jnp.broadcast_to(decs[ci][:, cs], (HG_HEAD_DIM, HG_HEAD_DIM)))
            st = st * dcol + ds
            o = o * lax.rsqrt(jnp.mean(o * o, axis=-1, keepdims=True) + EPS)
            o_ref[rows, cs] = o * gate[rows, cs]
        st_ref[hd] = st


def _mixer_kernel(x_ref, gin_ref, win32_ref, sre_ref, sim_ref, ldt_ref, bre_ref, bim_ref, cre_ref, cim_ref, d_ref,
                  wglu32_ref, bglu_ref, lbz_ref, ng_ref, wup32_ref, ohg_ref, y_ref, wup_ref,
                  win_ref, wglu_ref, wb_ref, wc_ref, are_ref, aim_ref, lb_ref, il_ref, bu_ref, pre_ref, pim_ref,
                  carry_ref, hst_ref, *, n_blocks):
    wup_ref[...] = wup32_ref[...].astype(BF16)
    h = S5_SEG
    s = S5_SEQS
    half = wb_ref.shape[2] // 2
    kw = wb_ref.shape[1]
    bsz, tt, d_model = x_ref.shape
    width = y_ref.shape[-1]

    @pl.when(pl.program_id(0) == 0)
    def _():
        win_ref[...] = win32_ref[...].astype(BF16)
        wglu_ref[...] = wglu32_ref[...].astype(BF16)
        carry_ref[...] = jnp.zeros_like(carry_ref)
        hst_ref[...] = jnp.zeros_like(hst_ref)
        z = lbz_ref[...]
        e = jnp.exp(z - jnp.max(z, axis=0, keepdims=True))
        lb_ref[...] = (e / jnp.sum(e, axis=0, keepdims=True))[0:1, :]

        a_r = sre_ref[...]
        a_i = sim_ref[...]
        n_g = ldt_ref.shape[1]
        diag = (lax.broadcasted_iota(jnp.int32, (n_g, n_g), 0)
                == lax.broadcasted_iota(jnp.int32, (n_g, n_g), 1))
        dt = jnp.exp(jnp.sum(jnp.where(diag, ldt_ref[...], 0.0), axis=1, keepdims=True))
        mag = jnp.exp(dt * a_r)
        abr = mag * jnp.cos(dt * a_i)
        abi = mag * jnp.sin(dt * a_i)
        nr = abr - 1.0
        den = a_r * a_r + a_i * a_i
        flat = lambda v: jnp.concatenate([v[g:g + 1, :] for g in range(v.shape[0])], axis=1)
        zr = flat((nr * a_r + abi * a_i) / den)
        zi = flat((abi * a_r - nr * a_i) / den)
        abr = flat(abr)
        abi = flat(abi)
        ar = jnp.broadcast_to(abr, are_ref.shape)
        ai = jnp.broadcast_to(abi, aim_ref.shape)
        are_ref[...] = ar
        aim_ref[...] = ai

        same = (lax.broadcasted_iota(jnp.int32, (kw, half), 0) // S5_GROUP
                == lax.broadcasted_iota(jnp.int32, (kw, half), 1) // S5_STATE)

        for m in range(n_blocks):
            bs = slice(half * m, half * (m + 1))
            b_r, b_i, c_r, c_i = (jnp.concatenate([ref[kw * m:kw * (m + 1), :]] * (half // S5_STATE), axis=1)
                                  for ref in (bre_ref, bim_ref, cre_ref, cim_ref))
            wb_ref[m, :, 0:half] = jnp.where(same, zr[:, bs] * b_r - zi[:, bs] * b_i, 0.0).astype(BF16)
            wb_ref[m, :, half:2 * half] = jnp.where(same, zr[:, bs] * b_i + zi[:, bs] * b_r, 0.0).astype(BF16)
            c_r = jnp.where(same, c_r, 0.0)
            c_i = jnp.where(same, c_i, 0.0)
            wc_ref[m, 0:half, :] = jnp.transpose(c_r).astype(BF16)
            wc_ref[m, half:2 * half, :] = (-jnp.transpose(c_i)).astype(BF16)
        pre_ref[0:s, :] = ar
        pim_ref[0:s, :] = ai

        def powers(i, c):
            pr, pi = c
            npr = pr * ar - pi * ai
            npi = pr * ai + pi * ar
            r = pl.ds(pl.multiple_of(i * s, s), s)
            pre_ref[r, :] = npr
            pim_ref[r, :] = npi
            return npr, npi

        lax.fori_loop(1, h, powers, (ar, ai))

    xb = _rms(x_ref[...].reshape(bsz * tt, d_model), gin_ref[...]).astype(BF16)
    u_nat = jnp.dot(xb, win_ref[:, 4 * width:5 * width], preferred_element_type=F32)

    n_slabs = il_ref.shape[0]
    for bi in range(bsz):
        for k in range(2):
            r0 = bi * tt + k * h
            for j in range(n_slabs):
                il_ref[j, pl.ds(2 * bi + k, h, stride=s), :] = u_nat[r0:r0 + h, j * LANES:(j + 1) * LANES]
    u = jnp.concatenate([il_ref[j] for j in range(n_slabs)], axis=1)
    ub = u.astype(BF16)
    for m in range(n_blocks):
        bu_ref[:, 2 * half * m:2 * half * (m + 1)] = jnp.dot(
            ub[:, kw * m:kw * (m + 1)], wb_ref[m], preferred_element_type=F32)

    q, fz, v, gz = (jnp.dot(xb, win_ref[:, i * width:(i + 1) * width], preferred_element_type=F32)
                    for i in range(4))
    for bi in range(bsz):
        r = slice(bi * tt, (bi + 1) * tt)
        _hgrn2_rows(q[r], fz[r], v[r], gz[r], lb_ref[...], ng_ref[...], hst_ref.at[bi], ohg_ref.at[bi])

    second = lax.broadcasted_iota(jnp.int32, (s, half), 0) % 2 == 1
    last = slice((h - 1) * s, h * s)

    for m0 in range(0, n_blocks, S5_SCAN_BLOCKS):
        blocks = []
        for m in range(m0, m0 + S5_SCAN_BLOCKS):
            lo = 2 * half * m
            blocks.append((slice(lo, lo + half), slice(lo + half, lo + 2 * half),
                           slice(half * m, half * (m + 1))))
        coef = [(are_ref[:, am], aim_ref[:, am]) for _, _, am in blocks]

        def scan(i, c, blocks=blocks, coef=coef):
            r = pl.ds(pl.multiple_of(i * s, s), s)
            out = []
            for (re, im, _), (ar, ai), (xr, xi) in zip(blocks, coef, c):
                nxr = ar * xr - ai * xi + bu_ref[r, re]
                nxi = ar * xi + ai * xr + bu_ref[r, im]
                bu_ref[r, re] = nxr
                bu_ref[r, im] = nxi
                out.append((nxr, nxi))
            return tuple(out)

        init = tuple((carry_ref[:, re], carry_ref[:, im]) for re, im, _ in blocks)
        ends = lax.fori_loop(0, h, scan, init, unroll=S5_SCAN_UNROLL)
        starts = [(jnp.where(second, pltpu.roll(er, 1, axis=0), 0.0),
                   jnp.where(second, pltpu.roll(ei, 1, axis=0), 0.0)) for er, ei in ends]

        def fix(i, c, blocks=blocks, starts=starts):
            r = pl.ds(pl.multiple_of(i * s, s), s)
            for (re, im, am), (cr, ci) in zip(blocks, starts):
                pr = pre_ref[r, am]
                pi = pim_ref[r, am]
                bu_ref[r, re] = bu_ref[r, re] + (pr * cr - pi * ci)
                bu_ref[r, im] = bu_ref[r, im] + (pr * ci + pi * cr)
            return c

        lax.fori_loop(0, h, fix, 0, unroll=S5_SCAN_UNROLL)
        for re, im, _ in blocks:
            carry_ref[:, re] = jnp.where(second, 0.0, pltpu.roll(bu_ref[last, re], s - 1, axis=0))
            carry_ref[:, im] = jnp.where(second, 0.0, pltpu.roll(bu_ref[last, im], s - 1, axis=0))

    ys = [jnp.dot(bu_ref[:, 2 * half * m:2 * half * (m + 1)].astype(BF16), wc_ref[m],
                  preferred_element_type=F32) for m in range(n_blocks)]
    y = jnp.concatenate(ys, axis=1) + d_ref[...] * u
    y = jax.nn.gelu(y)
    z = jnp.dot(y.astype(BF16), wglu_ref[...], preferred_element_type=F32) + bglu_ref[...]
    y = y * jax.nn.sigmoid(z)
    for j in range(n_slabs):
        il_ref[j] = y[:, j * LANES:(j + 1) * LANES]
    for bi in range(y_ref.shape[0]):
        for k in range(2):
            for j in range(n_slabs):
                y_ref[bi, k * h:(k + 1) * h, j * LANES:(j + 1) * LANES] = il_ref[
                    j, pl.ds(2 * bi + k, h, stride=s), :]


def _ffn_kernel(x_ref, ohg_ref, y_ref, wout32_ref, gf_ref, wup_ref, cw_ref, cb_ref, wdn32_ref, g_ref, o_ref,
                wout_ref, wdn_ref, tail_ref, h_ref, hn_ref, pre_ref, act_ref, *, n_blk):
    @pl.when((pl.program_id(0) == 0) & (pl.program_id(1) == 0))
    def _():
        wout_ref[...] = wout32_ref[...].astype(BF16)
        wdn_ref[...] = wdn32_ref[...].astype(BF16)

    @pl.when(pl.program_id(1) == 0)
    def _():
        tail_ref[...] = jnp.zeros_like(tail_ref)

    mixed = jnp.concatenate([ohg_ref[...].astype(BF16), y_ref[...].astype(BF16)], axis=1)
    hres = x_ref[...] + jnp.dot(mixed, wout_ref[...], preferred_element_type=F32)
    h_ref[...] = hres
    hn_ref[...] = _rms(hres, gf_ref[...]).astype(BF16)

    tm = act_ref.shape[0]
    bf = pre_ref.shape[-1]
    rows = FFN_ROWS
    top = lax.broadcasted_iota(jnp.int32, (SUBLANES, bf), 0)

    cols = lambda j, part: slice((part * n_blk + j) * bf, (part * n_blk + j + 1) * bf)

    def up(j, slot):
        for part in range(2):
            pre_ref[slot, part, rows:rows + tm] = jnp.dot(hn_ref[...], wup_ref[:, cols(j, part)],
                                                          preferred_element_type=F32)

    def mid(j, slot):
        taps, prev = [], []
        for part in range(2):
            blk = cols(j, part)
            pre_ref[slot, part, 0:rows] = tail_ref[:, blk]
            tail_ref[:, blk] = pre_ref[slot, part, tm:tm + rows]
            cw = cw_ref[:, blk]
            taps.append((cb_ref[:, blk], cw[0:1, :], cw[1:2, :], cw[2:3, :]))
            p = pre_ref[slot, part, 0:rows]
            prev.append((pltpu.roll(p, 1, axis=0), pltpu.roll(p, 2, axis=0)))
        for g in range(tm // rows):
            r0 = rows * (g + 1)
            outs = []
            for part in range(2):
                cur = pre_ref[slot, part, r0:r0 + rows]
                rc = (pltpu.roll(cur, 1, axis=0), pltpu.roll(cur, 2, axis=0))
                sh = [jnp.concatenate([jnp.where(top < d + 1, prev[part][d][0:SUBLANES], rc[d][0:SUBLANES]),
                                       rc[d][SUBLANES:]], axis=0) for d in range(2)]
                prev[part] = rc
                cb, w0, w1, w2 = taps[part]
                outs.append(cb + w0 * sh[1] + w1 * sh[0] + w2 * cur)
            gate, val = outs
            act_ref[rows * g:rows * (g + 1), j * bf:(j + 1) * bf] = (
                (gate * jax.nn.sigmoid(gate)) * val).astype(BF16)

    up(0, 0)
    for j in range(n_blk):
        if j + 1 < n_blk:
            up(j + 1, (j + 1) % 2)
        mid(j, j % 2)
    out = h_ref[...] + jnp.dot(act_ref[...], wdn_ref[...], preferred_element_type=F32)
    o_ref[...] = _rms(out, g_ref[...])


def _tile_sizes(seq_len):
    tm = min(512, seq_len)
    tt = 2 * S5_SEG
    assert seq_len % tm == 0 and seq_len % tt == 0 and tt % HG_CHUNK == 0 and tm % FFN_ROWS == 0
    return tm, tt


def kernel(x, in_norm_g, w_in, hg_lb, hg_norm_g, s5_a_re, s5_a_im, s5_log_dt, s5_b_re, s5_b_im,
           s5_c_re, s5_c_im, s5_d, s5_w_glu, s5_b_glu, w_out, ffn_norm_g, w_up, conv_w, conv_b,
           w_down, final_norm_g):
    bsz, seq_len, d_model = x.shape
    depth = w_in.shape[0]
    assert depth == 1, "single-layer block"
    width = hg_norm_g.shape[1]
    assert w_in.shape[2] == 5 * width and s5_d.shape[1] == width
    n_heads = width // HG_HEAD_DIM
    groups, n_state, n_chan = s5_b_re.shape[1:]
    assert (n_state, n_chan) == (S5_STATE, S5_GROUP) and groups * n_chan == width
    assert 2 * bsz == S5_SEQS
    d_ff = w_down.shape[1]
    tm, tt = _tile_sizes(seq_len)
    n_tiles = seq_len // tt
    row = lambda a: a.reshape(1, -1)

    gp = groups * n_chan
    gpb = LANES // n_chan
    n_blocks = groups // gpb
    n_lanes = groups * n_state
    a_re, a_im = s5_a_re[0], s5_a_im[0]
    ldt = row(s5_log_dt[0])
    b_t = lambda a: a.transpose(0, 2, 1).reshape(gp, n_state)
    b_re, b_im = b_t(s5_b_re[0]), b_t(s5_b_im[0])
    c_re, c_im = s5_c_re[0].reshape(gp, n_state), s5_c_im[0].reshape(gp, n_state)

    tok = lambda w: pl.BlockSpec((None, tm, w), lambda b, t: (b, t, 0))
    const2 = lambda shape: pl.BlockSpec(shape, lambda b, t: (0,) * len(shape))
    resident = lambda shape: pl.BlockSpec(shape, lambda b, t: (0,) * len(shape), pipeline_mode=pl.Buffered(1))

    m_rows = S5_SEG * S5_SEQS
    c1 = lambda shape: pl.BlockSpec(shape, lambda j: (0,) * len(shape))
    res1 = lambda shape: pl.BlockSpec(shape, lambda j: (0,) * len(shape), pipeline_mode=pl.Buffered(1))
    tile_spec = lambda w: pl.BlockSpec((bsz, tt, w), lambda j: (0, j, 0))
    act = lambda dt: jax.ShapeDtypeStruct((bsz, seq_len, width), dt)
    assert d_model % n_tiles == 0
    wup_slab = pl.BlockSpec((d_model // n_tiles, 2 * d_ff), lambda j: (j, 0))
    o_hg, y_s5, w_up_bf16 = pl.pallas_call(
        functools.partial(_mixer_kernel, n_blocks=n_blocks),
        grid=(n_tiles,),
        in_specs=[tile_spec(d_model), c1((1, d_model)), res1((d_model, 5 * width)),
                  c1(a_re.shape), c1(a_im.shape), c1(ldt.shape), c1(b_re.shape), c1(b_im.shape), c1(c_re.shape), c1(c_im.shape),
                  c1((1, width)), res1((width, width)), c1((1, width)), c1(hg_lb.shape), c1((1, width)),
                  wup_slab],
        out_specs=[tile_spec(width)] * 2 + [wup_slab],
        out_shape=[act(F32)] * 2 + [jax.ShapeDtypeStruct((d_model, 2 * d_ff), BF16)],
        scratch_shapes=[pltpu.VMEM((d_model, 5 * width), BF16), pltpu.VMEM((width, width), BF16),
                        pltpu.VMEM((n_blocks, LANES, 2 * gpb * n_state), BF16),
                        pltpu.VMEM((n_blocks, 2 * gpb * n_state, LANES), BF16),
                        pltpu.VMEM((S5_SEQS, n_lanes), F32), pltpu.VMEM((S5_SEQS, n_lanes), F32),
                        pltpu.VMEM((1, width), F32),
                        pltpu.VMEM((width // LANES, m_rows, LANES), F32), pltpu.VMEM((m_rows, 2 * n_lanes), F32),
                        pltpu.VMEM((m_rows, n_lanes), F32), pltpu.VMEM((m_rows, n_lanes), F32),
                        pltpu.VMEM((S5_SEQS, 2 * n_lanes), F32),
                        pltpu.VMEM((bsz, n_heads, HG_HEAD_DIM, HG_HEAD_DIM), F32)],
        compiler_params=_cparams("arbitrary"),
        name="mixer",
    )(x, row(in_norm_g[0]), w_in[0], a_re, a_im, ldt, b_re, b_im, c_re, c_im, row(s5_d[0]), s5_w_glu[0], row(s5_b_glu[0]),
      hg_lb, row(hg_norm_g[0]), w_up[0])

    bf = FFN_BLOCK
    assert d_ff % bf == 0
    n_blk = d_ff // bf
    out = pl.pallas_call(
        functools.partial(_ffn_kernel, n_blk=n_blk),
        grid=(bsz, seq_len // tm),
        in_specs=[tok(d_model), tok(width), tok(width), resident((2 * width, d_model)), const2((1, d_model)),
                  resident((d_model, 2 * d_ff)), const2((CONV_WIDTH, 2 * d_ff)),
                  const2((1, 2 * d_ff)), resident((d_ff, d_model)), const2((1, d_model))],
        out_specs=tok(d_model),
        out_shape=jax.ShapeDtypeStruct((bsz, seq_len, d_model), F32),
        scratch_shapes=[pltpu.VMEM((2 * width, d_model), BF16), pltpu.VMEM((d_ff, d_model), BF16),
                        pltpu.VMEM((FFN_ROWS, 2 * d_ff), F32), pltpu.VMEM((tm, d_model), F32),
                        pltpu.VMEM((tm, d_model), BF16), pltpu.VMEM((2, 2, tm + FFN_ROWS, bf), F32),
                        pltpu.VMEM((tm, d_ff), BF16)],
        compiler_params=_cparams("arbitrary", "arbitrary"),
        name="ffn",
    )(x, o_hg, y_s5, w_out[0], row(ffn_norm_g[0]), w_up_bf16, conv_w[0], row(conv_b[0]),
      w_down[0], row(final_norm_g))
    return out
```

```python
import functools

import jax
import jax.numpy as jnp
from jax import lax
from jax.experimental import pallas as pl
from jax.experimental.pallas import tpu as pltpu

EPS = 1e-6
HG_CHUNK = 64
HG_HEAD_DIM = 128
S5_GROUP = 16
S5_STATE = 64
S5_SEG = 64
S5_SEQS = 8
S5_SCAN_BLOCKS = 2
S5_SCAN_UNROLL = True
CONV_WIDTH = 3
FFN_BLOCK = 256
FFN_ROWS = 16
SUBLANES = 8
LANES = 128
VMEM_LIMIT = 60 * 1024 * 1024

F32 = jnp.float32
BF16 = jnp.bfloat16

_NT = (((1,), (1,)), ((), ()))
_TN = (((0,), (0,)), ((), ()))


def _rms(x, g):
    ms = jnp.mean(x * x, axis=-1, keepdims=True)
    return x * lax.rsqrt(ms + EPS) * g


def _cparams(*sem):
    return pltpu.CompilerParams(dimension_semantics=sem, vmem_limit_bytes=VMEM_LIMIT)


def _cumsum_rows(x):
    n, w = x.shape
    top = lax.broadcasted_iota(jnp.int32, (SUBLANES, w), 0)
    sh = 1
    while sh < n:
        if sh < SUBLANES:
            r = pltpu.roll(x, sh, axis=0)
            r = jnp.concatenate([jnp.where(top >= sh, r[0:SUBLANES], 0.0), r[SUBLANES:]], axis=0)
        else:
            r = jnp.concatenate([jnp.zeros((sh, w), x.dtype), x[:n - sh]], axis=0)
        x = x + r
        sh *= 2
    return x


def _hgrn2_rows(q, fz, v, gz, lb, ng, st_ref, o_ref):
    c = HG_CHUNK
    n_chunks = q.shape[0] // c
    n_heads = st_ref.shape[0]
    tri = lax.broadcasted_iota(jnp.int32, (c, c), 0) >= lax.broadcasted_iota(jnp.int32, (c, c), 1)

    f = lb + (1.0 - lb) * jax.nn.sigmoid(fz)
    k = 1.0 - f
    logf = jnp.log(f)
    b = jnp.concatenate([_cumsum_rows(logf[ci * c:(ci + 1) * c]) for ci in range(n_chunks)], axis=0)
    kd_f = k * jnp.exp(-b)
    qd = (q * jnp.exp(b)).astype(BF16)
    kd = kd_f.astype(BF16)
    vb = v.astype(BF16)
    gate = ng * (gz * jax.nn.sigmoid(gz))
    decs = [jnp.exp(b[(ci + 1) * c - 1:(ci + 1) * c, :]) for ci in range(n_chunks)]
    kts = [(kd_f[ci * c:(ci + 1) * c] * decs[ci]).astype(BF16) for ci in range(n_chunks)]

    for hd in range(n_heads):
        cs = slice(hd * HG_HEAD_DIM, (hd + 1) * HG_HEAD_DIM)
        st = st_ref[hd]
        for ci in range(n_chunks):
            rows = slice(ci * c, (ci + 1) * c)
            att = lax.dot_general(qd[rows, cs], kd[rows, cs], _NT, preferred_element_type=F32)
            att = jnp.where(tri, att, 0.0)
            lhs = jnp.concatenate([qd[rows, cs], att.astype(BF16)], axis=1)
            rhs = jnp.concatenate([st.astype(BF16), vb[rows, cs]], axis=0)
            o = jnp.dot(lhs, rhs, preferred_element_type=F32)
            ds = lax.dot_general(kts[ci][:, cs], vb[rows, cs], _TN, preferred_element_type=F32)
            dcol = jnp.transpose(jnp.broadcast_to(decs[ci][:, cs], (HG_HEAD_DIM, HG_HEAD_DIM)))
            st = st * dcol + ds
            o = o * lax.rsqrt(jnp.mean(o * o, axis=-1, keepdims=True) + EPS)
            o_ref[rows, cs] = o * gate[rows, cs]
        st_ref[hd] = st


def _mixer_kernel(x_ref, gin_ref, win32_ref, sre_ref, sim_ref, ldt_ref, bre_ref, bim_ref, cre_ref, cim_ref, d_ref,
                  wglu32_ref, bglu_ref, lbz_ref, ng_ref, wup32_ref, ohg_ref, y_ref, wup_ref,
                  win_ref, wglu_ref, wb_ref, wc_ref, are_ref, aim_ref, lb_ref, il_ref, bu_ref, pre_ref, pim_ref,
                  carry_ref, hst_ref, *, n_blocks):
    wup_ref[...] = wup32_ref[...].astype(BF16)
    h = S5_SEG
    s = S5_SEQS
    half = wb_ref.shape[2] // 2
    kw = wb_ref.shape[1]
    bsz, tt, d_model = x_ref.shape
    width = y_ref.shape[-1]

    @pl.when(pl.program_id(0) == 0)
    def _():
        win_ref[...] = win32_ref[...].astype(BF16)
        wglu_ref[...] = wglu32_ref[...].astype(BF16)
        carry_ref[...] = jnp.zeros_like(carry_ref)
        hst_ref[...] = jnp.zeros_like(hst_ref)
        z = lbz_ref[...]
        e = jnp.exp(z - jnp.max(z, axis=0, keepdims=True))
        lb_ref[...] = (e / jnp.sum(e, axis=0, keepdims=True))[0:1, :]

        a_r = sre_ref[...]
        a_i = sim_ref[...]
        n_g = ldt_ref.shape[1]
        diag = (lax.broadcasted_iota(jnp.int32, (n_g, n_g), 0)
                == lax.broadcasted_iota(jnp.int32, (n_g, n_g), 1))
        dt = jnp.exp(jnp.sum(jnp.where(diag, ldt_ref[...], 0.0), axis=1, keepdims=True))
        mag = jnp.exp(dt * a_r)
        abr = mag * jnp.cos(dt * a_i)
        abi = mag * jnp.sin(dt * a_i)
        nr = abr - 1.0
        den = a_r * a_r + a_i * a_i
        flat = lambda v: jnp.concatenate([v[g:g + 1, :] for g in range(v.shape[0])], axis=1)
        zr = flat((nr * a_r + abi * a_i) / den)
        zi = flat((abi * a_r - nr * a_i) / den)
        abr = flat(abr)
        abi = flat(abi)
        ar = jnp.broadcast_to(abr, are_ref.shape)
        ai = jnp.broadcast_to(abi, aim_ref.shape)
        are_ref[...] = ar
        aim_ref[...] = ai

        same = (lax.broadcasted_iota(jnp.int32, (kw, half), 0) // S5_GROUP
                == lax.broadcasted_iota(jnp.int32, (kw, half), 1) // S5_STATE)

        for m in range(n_blocks):
            bs = slice(half * m, half * (m + 1))
            b_r, b_i, c_r, c_i = (jnp.concatenate([ref[kw * m:kw * (m + 1), :]] * (half // S5_STATE), axis=1)
                                  for ref in (bre_ref, bim_ref, cre_ref, cim_ref))
            wb_ref[m, :, 0:half] = jnp.where(same, zr[:, bs] * b_r - zi[:, bs] * b_i, 0.0).astype(BF16)
            wb_ref[m, :, half:2 * half] = jnp.where(same, zr[:, bs] * b_i + zi[:, bs] * b_r, 0.0).astype(BF16)
            c_r = jnp.where(same, c_r, 0.0)
            c_i = jnp.where(same, c_i, 0.0)
            wc_ref[m, 0:half, :] = jnp.transpose(c_r).astype(BF16)
            wc_ref[m, half:2 * half, :] = (-jnp.transpose(c_i)).astype(BF16)
        pre_ref[0:s, :] = ar
        pim_ref[0:s, :] = ai

        def powers(i, c):
            pr, pi = c
            npr = pr * ar - pi * ai
            npi = pr * ai + pi * ar
            r = pl.ds(pl.multiple_of(i * s, s), s)
            pre_ref[r, :] = npr
            pim_ref[r, :] = npi
            return npr, npi

        lax.fori_loop(1, h, powers, (ar, ai))

    xb = _rms(x_ref[...].reshape(bsz * tt, d_model), gin_ref[...]).astype(BF16)
    u_nat = jnp.dot(xb, win_ref[:, 4 * width:5 * width], preferred_element_type=F32)

    n_slabs = il_ref.shape[0]
    for bi in range(bsz):
        for k in range(2):
            r0 = bi * tt + k * h
            for j in range(n_slabs):
                il_ref[j, pl.ds(2 * bi + k, h, stride=s), :] = u_nat[r0:r0 + h, j * LANES:(j + 1) * LANES]
    u = jnp.concatenate([il_ref[j] for j in range(n_slabs)], axis=1)
    ub = u.astype(BF16)
    for m in range(n_blocks):
        bu_ref[:, 2 * half * m:2 * half * (m + 1)] = jnp.dot(
            ub[:, kw * m:kw * (m + 1)], wb_ref[m], preferred_element_type=F32)

    q, fz, v, gz = (jnp.dot(xb, win_ref[:, i * width:(i + 1) * width], preferred_element_type=F32)
                    for i in range(4))
    for bi in range(bsz):
        r = slice(bi * tt, (bi + 1) * tt)
        _hgrn2_rows(q[r], fz[r], v[r], gz[r], lb_ref[...], ng_ref[...], hst_ref.at[bi], ohg_ref.at[bi])

    second = lax.broadcasted_iota(jnp.int32, (s, half), 0) % 2 == 1
    last = slice((h - 1) * s, h * s)

    for m0 in range(0, n_blocks, S5_SCAN_BLOCKS):
        blocks = []
        for m in range(m0, m0 + S5_SCAN_BLOCKS):
            lo = 2 * half * m
            blocks.append((slice(lo, lo + half), slice(lo + half, lo + 2 * half),
                           slice(half * m, half * (m + 1))))
        coef = [(are_ref[:, am], aim_ref[:, am]) for _, _, am in blocks]

        def scan(i, c, blocks=blocks, coef=coef):
            r = pl.ds(pl.multiple_of(i * s, s), s)
            out = []
            for (re, im, _), (ar, ai), (xr, xi) in zip(blocks, coef, c):
                nxr = ar * xr - ai * xi + bu_ref[r, re]
                nxi = ar * xi + ai * xr + bu_ref[r, im]
                bu_ref[r, re] = nxr
                bu_ref[r, im] = nxi
                out.append((nxr, nxi))
            return tuple(out)

        init = tuple((carry_ref[:, re], carry_ref[:, im]) for re, im, _ in blocks)
        ends = lax.fori_loop(0, h, scan, init, unroll=S5_SCAN_UNROLL)
        starts = [(jnp.where(second, pltpu.roll(er, 1, axis=0), 0.0),
                   jnp.where(second, pltpu.roll(ei, 1, axis=0), 0.0)) for er, ei in ends]

        def fix(i, c, blocks=blocks, starts=starts):
            r = pl.ds(pl.multiple_of(i * s, s), s)
            for (re, im, am), (cr, ci) in zip(blocks, starts):
                pr = pre_ref[r, am]
                pi = pim_ref[r, am]
                bu_ref[r, re] = bu_ref[r, re] + (pr * cr - pi * ci)
                bu_ref[r, im] = bu_ref[r, im] + (pr * ci + pi * cr)
            return c

        lax.fori_loop(0, h, fix, 0, unroll=S5_SCAN_UNROLL)
        for re, im, _ in blocks:
            carry_ref[:, re] = jnp.where(second, 0.0, pltpu.roll(bu_ref[last, re], s - 1, axis=0))
            carry_ref[:, im] = jnp.where(second, 0.0, pltpu.roll(bu_ref[last, im], s - 1, axis=0))

    ys = [jnp.dot(bu_ref[:, 2 * half * m:2 * half * (m + 1)].astype(BF16), wc_ref[m],
                  preferred_element_type=F32) for m in range(n_blocks)]
    y = jnp.concatenate(ys, axis=1) + d_ref[...] * u
    y = jax.nn.gelu(y)
    z = jnp.dot(y.astype(BF16), wglu_ref[...], preferred_element_type=F32) + bglu_ref[...]
    y = y * jax.nn.sigmoid(z)
    for j in range(n_slabs):
        il_ref[j] = y[:, j * LANES:(j + 1) * LANES]
    for bi in range(y_ref.shape[0]):
        for k in range(2):
            for j in range(n_slabs):
                y_ref[bi, k * h:(k + 1) * h, j * LANES:(j + 1) * LANES] = il_ref[
                    j, pl.ds(2 * bi + k, h, stride=s), :]


def _ffn_kernel(x_ref, ohg_ref, y_ref, wout32_ref, gf_ref, wup_ref, cw_ref, cb_ref, wdn32_ref, g_ref, o_ref,
                wout_ref, wdn_ref, tail_ref, h_ref, hn_ref, pre_ref, act_ref, wdn32_buf, wdn_sem, *, n_blk):
    first = (pl.program_id(0) == 0) & (pl.program_id(1) == 0)
    wdn_copy = pltpu.make_async_copy(wdn32_ref, wdn32_buf, wdn_sem)

    @pl.when(first)
    def _():
        wdn_copy.start()
        wout_ref[...] = wout32_ref[...].astype(BF16)

    @pl.when(pl.program_id(1) == 0)
    def _():
        tail_ref[...] = jnp.zeros_like(tail_ref)

    mixed = jnp.concatenate([ohg_ref[...].astype(BF16), y_ref[...].astype(BF16)], axis=1)
    hres = x_ref[...] + jnp.dot(mixed, wout_ref[...], preferred_element_type=F32)
    h_ref[...] = hres
    hn_ref[...] = _rms(hres, gf_ref[...]).astype(BF16)

    tm = act_ref.shape[0]
    bf = pre_ref.shape[-1]
    rows = FFN_ROWS
    top = lax.broadcasted_iota(jnp.int32, (SUBLANES, bf), 0)

    cols = lambda j, part: slice((part * n_blk + j) * bf, (part * n_blk + j + 1) * bf)

    def up(j, slot):
        for part in range(2):
            pre_ref[slot, part, rows:rows + tm] = jnp.dot(hn_ref[...], wup_ref[:, cols(j, part)],
                                                          preferred_element_type=F32)

    def mid(j, slot):
        taps, prev = [], []
        for part in range(2):
            blk = cols(j, part)
            pre_ref[slot, part, 0:rows] = tail_ref[:, blk]
            tail_ref[:, blk] = pre_ref[slot, part, tm:tm + rows]
            cw = cw_ref[:, blk]
            taps.append((cb_ref[:, blk], cw[0:1, :], cw[1:2, :], cw[2:3, :]))
            p = pre_ref[slot, part, 0:rows]
            prev.append((pltpu.roll(p, 1, axis=0), pltpu.roll(p, 2, axis=0)))
        for g in range(tm // rows):
            r0 = rows * (g + 1)
            outs = []
            for part in range(2):
                cur = pre_ref[slot, part, r0:r0 + rows]
                rc = (pltpu.roll(cur, 1, axis=0), pltpu.roll(cur, 2, axis=0))
                sh = [jnp.concatenate([jnp.where(top < d + 1, prev[part][d][0:SUBLANES], rc[d][0:SUBLANES]),
                                       rc[d][SUBLANES:]], axis=0) for d in range(2)]
                prev[part] = rc
                cb, w0, w1, w2 = taps[part]
                outs.append(cb + w0 * sh[1] + w1 * sh[0] + w2 * cur)
            gate, val = outs
            act_ref[rows * g:rows * (g + 1), j * bf:(j + 1) * bf] = (
                (gate * jax.nn.sigmoid(gate)) * val).astype(BF16)

    up(0, 0)
    for j in range(n_blk):
        if j + 1 < n_blk:
            up(j + 1, (j + 1) % 2)
        mid(j, j % 2)

    @pl.when(first)
    def _():
        wdn_copy.wait()
        wdn_ref[...] = wdn32_buf[...].astype(BF16)

    out = h_ref[...] + jnp.dot(act_ref[...], wdn_ref[...], preferred_element_type=F32)
    o_ref[...] = _rms(out, g_ref[...])


def _tile_sizes(seq_len):
    tm = min(512, seq_len)
    tt = 2 * S5_SEG
    assert seq_len % tm == 0 and seq_len % tt == 0 and tt % HG_CHUNK == 0 and tm % FFN_ROWS == 0
    return tm, tt


def kernel(x, in_norm_g, w_in, hg_lb, hg_norm_g, s5_a_re, s5_a_im, s5_log_dt, s5_b_re, s5_b_im,
           s5_c_re, s5_c_im, s5_d, s5_w_glu, s5_b_glu, w_out, ffn_norm_g, w_up, conv_w, conv_b,
           w_down, final_norm_g):
    bsz, seq_len, d_model = x.shape
    depth = w_in.shape[0]
    assert depth == 1, "single-layer block"
    width = hg_norm_g.shape[1]
    assert w_in.shape[2] == 5 * width and s5_d.shape[1] == width
    n_heads = width // HG_HEAD_DIM
    groups, n_state, n_chan = s5_b_re.shape[1:]
    assert (n_state, n_chan) == (S5_STATE, S5_GROUP) and groups * n_chan == width
    assert 2 * bsz == S5_SEQS
    d_ff = w_down.shape[1]
    tm, tt = _tile_sizes(seq_len)
    n_tiles = seq_len // tt
    row = lambda a: a.reshape(1, -1)

    gp = groups * n_chan
    gpb = LANES // n_chan
    n_blocks = groups // gpb
    n_lanes = groups * n_state
    a_re, a_im = s5_a_re[0], s5_a_im[0]
    ldt = row(s5_log_dt[0])
    b_t = lambda a: a.transpose(0, 2, 1).reshape(gp, n_state)
    b_re, b_im = b_t(s5_b_re[0]), b_t(s5_b_im[0])
    c_re, c_im = s5_c_re[0].reshape(gp, n_state), s5_c_im[0].reshape(gp, n_state)

    tok = lambda w: pl.BlockSpec((None, tm, w), lambda b, t: (b, t, 0))
    const2 = lambda shape: pl.BlockSpec(shape, lambda b, t: (0,) * len(shape))
    resident = lambda shape: pl.BlockSpec(shape, lambda b, t: (0,) * len(shape), pipeline_mode=pl.Buffered(1))

    m_rows = S5_SEG * S5_SEQS
    c1 = lambda shape: pl.BlockSpec(shape, lambda j: (0,) * len(shape))
    res1 = lambda shape: pl.BlockSpec(shape, lambda j: (0,) * len(shape), pipeline_mode=pl.Buffered(1))
    tile_spec = lambda w: pl.BlockSpec((bsz, tt, w), lambda j: (0, j, 0))
    act = lambda dt: jax.ShapeDtypeStruct((bsz, seq_len, width), dt)
    assert d_model % n_tiles == 0
    wup_slab = pl.BlockSpec((d_model // n_tiles, 2 * d_ff), lambda j: (j, 0))
    o_hg, y_s5, w_up_bf16 = pl.pallas_call(
        functools.partial(_mixer_kernel, n_blocks=n_blocks),
        grid=(n_tiles,),
        in_specs=[tile_spec(d_model), c1((1, d_model)), res1((d_model, 5 * width)),
                  c1(a_re.shape), c1(a_im.shape), c1(ldt.shape), c1(b_re.shape), c1(b_im.shape), c1(c_re.shape), c1(c_im.shape),
                  c1((1, width)), res1((width, width)), c1((1, width)), c1(hg_lb.shape), c1((1, width)),
                  wup_slab],
        out_specs=[tile_spec(width)] * 2 + [wup_slab],
        out_shape=[act(F32)] * 2 + [jax.ShapeDtypeStruct((d_model, 2 * d_ff), BF16)],
        scratch_shapes=[pltpu.VMEM((d_model, 5 * width), BF16), pltpu.VMEM((width, width), BF16),
                        pltpu.VMEM((n_blocks, LANES, 2 * gpb * n_state), BF16),
                        pltpu.VMEM((n_blocks, 2 * gpb * n_state, LANES), BF16),
                        pltpu.VMEM((S5_SEQS, n_lanes), F32), pltpu.VMEM((S5_SEQS, n_lanes), F32),
                        pltpu.VMEM((1, width), F32),
                        pltpu.VMEM((width // LANES, m_rows, LANES), F32), pltpu.VMEM((m_rows, 2 * n_lanes), F32),
                        pltpu.VMEM((m_rows, n_lanes), F32), pltpu.VMEM((m_rows, n_lanes), F32),
                        pltpu.VMEM((S5_SEQS, 2 * n_lanes), F32),
                        pltpu.VMEM((bsz, n_heads, HG_HEAD_DIM, HG_HEAD_DIM), F32)],
        compiler_params=_cparams("arbitrary"),
        name="mixer",
    )(x, row(in_norm_g[0]), w_in[0], a_re, a_im, ldt, b_re, b_im, c_re, c_im, row(s5_d[0]), s5_w_glu[0], row(s5_b_glu[0]),
      hg_lb, row(hg_norm_g[0]), w_up[0])

    bf = FFN_BLOCK
    assert d_ff % bf == 0
    n_blk = d_ff // bf
    out = pl.pallas_call(
        functools.partial(_ffn_kernel, n_blk=n_blk),
        grid=(bsz, seq_len // tm),
        in_specs=[tok(d_model), tok(width), tok(width), resident((2 * width, d_model)), const2((1, d_model)),
                  resident((d_model, 2 * d_ff)), const2((CONV_WIDTH, 2 * d_ff)),
                  const2((1, 2 * d_ff)), pl.BlockSpec(memory_space=pl.ANY), const2((1, d_model))],
        out_specs=tok(d_model),
        out_shape=jax.ShapeDtypeStruct((bsz, seq_len, d_model), F32),
        scratch_shapes=[pltpu.VMEM((2 * width, d_model), BF16), pltpu.VMEM((d_ff, d_model), BF16),
                        pltpu.VMEM((FFN_ROWS, 2 * d_ff), F32), pltpu.VMEM((tm, d_model), F32),
                        pltpu.VMEM((tm, d_model), BF16), pltpu.VMEM((2, 2, tm + FFN_ROWS, bf), F32),
                        pltpu.VMEM((tm, d_ff), BF16), pltpu.VMEM((d_ff, d_model), F32),
                        pltpu.SemaphoreType.DMA(())],
        compiler_params=_cparams("arbitrary", "arbitrary"),
        name="ffn",
    )(x, o_hg, y_s5, w_out[0], row(ffn_norm_g[0]), w_up_bf16, conv_w[0], row(conv_b[0]),
      w_down[0], row(final_norm_g))
    return out
```
